```python
import math
import jax, jax.numpy as jnp
from jax import lax
import numpy as np

D_MODEL = 1024
BATCH = 4
SEQ = 4096
DEPTH = 2

GRID_W = 64
CTX_LEN = 256
HEAD_DIM = 64
N_Q_HEADS = 8
N_KV_HEADS = 2
Q_PER_KV = N_Q_HEADS // N_KV_HEADS
ATTN_WIDTH = N_Q_HEADS * HEAD_DIM
KV_WIDTH = N_KV_HEADS * HEAD_DIM
N_FOURIER_GROUPS = 8
FOURIER_GROUP_DIM = 64
FOURIER_WIDTH = N_FOURIER_GROUPS * FOURIER_GROUP_DIM
MIX_WIDTH = ATTN_WIDTH + FOURIER_WIDTH
IN_WIDTH = ATTN_WIDTH + 2 * KV_WIDTH + FOURIER_WIDTH
WINDOW = 128
BLOCK = 128
FFN_HIDDEN = -(-8 * D_MODEL // (3 * 256)) * 256
ROPE_THETA = 10000.0
ROPE_PAIRS_PER_AXIS = HEAD_DIM // 4
RMS_EPS = 1e-6
N_MOD = 6
NEG_INF = -1e30

kernel_name = 'hybrid_fourier_windowed_gqa_dit_block'


def rmsnorm(x, g):
    xf = x.astype(jnp.float32)
    y = xf * lax.rsqrt(jnp.mean(xf * xf, axis=-1, keepdims=True) + RMS_EPS)
    return (y * g.astype(jnp.float32)).astype(x.dtype)


def modulate(h, shift, scale):
    return h * (1.0 + scale) + shift


def axial_rope_tables(rows):
    row = jnp.broadcast_to(jnp.arange(rows)[:, None], (rows, GRID_W)).reshape(-1).astype(jnp.float32)
    col = jnp.broadcast_to(jnp.arange(GRID_W)[None, :], (rows, GRID_W)).reshape(-1).astype(jnp.float32)
    freqs = ROPE_THETA ** (-jnp.arange(ROPE_PAIRS_PER_AXIS, dtype=jnp.float32) / ROPE_PAIRS_PER_AXIS)
    ang = jnp.concatenate([row[:, None] * freqs, col[:, None] * freqs], axis=-1)
    return jnp.cos(ang), jnp.sin(ang)


def apply_axial_rope(x, cos, sin):
    xf = x.astype(jnp.float32).reshape(*x.shape[:-1], HEAD_DIM // 2, 2)
    x1, x2 = xf[..., 0], xf[..., 1]
    c = cos[None, :, None, :]
    s = sin[None, :, None, :]
    out = jnp.stack([x1 * c - x2 * s, x1 * s + x2 * c], axis=-1).reshape(x.shape)
    return out.astype(x.dtype)


def sink_column(sink, lead_shape):
    sk = sink.astype(jnp.float32).reshape(N_KV_HEADS, Q_PER_KV)[:, :, None, None]
    return jnp.broadcast_to(sk, lead_shape + (1,))


def windowed_gqa_with_context(q, k, v, kc, vc, sink):
    B, S = q.shape[0], q.shape[1]
    nb = S // BLOCK
    scale = HEAD_DIM ** -0.5
    qb = q.reshape(B, nb, BLOCK, N_KV_HEADS, Q_PER_KV, HEAD_DIM)
    pad = ((0, 0), (BLOCK, BLOCK), (0, 0), (0, 0))
    kp = jnp.pad(k, pad).reshape(B, nb + 2, BLOCK, N_KV_HEADS, HEAD_DIM)
    vp = jnp.pad(v, pad).reshape(B, nb + 2, BLOCK, N_KV_HEADS, HEAD_DIM)
    kb = jnp.concatenate([kp[:, :-2], kp[:, 1:-1], kp[:, 2:]], axis=2)
    vb = jnp.concatenate([vp[:, :-2], vp[:, 1:-1], vp[:, 2:]], axis=2)
    s_loc = jnp.einsum('bnqkgd,bnjkd->bnkgqj', qb, kb).astype(jnp.float32) * scale
    q_pos = jnp.arange(nb)[:, None] * BLOCK + jnp.arange(BLOCK)[None, :]
    k_pos = (jnp.arange(nb)[:, None] - 1) * BLOCK + jnp.arange(3 * BLOCK)[None, :]
    valid = (jnp.abs(q_pos[:, :, None] - k_pos[:, None, :]) <= WINDOW) & (k_pos[:, None, :] >= 0) & (k_pos[:, None, :] < S)
    s_loc = jnp.where(valid[None, :, None, None], s_loc, NEG_INF)
    s_ctx = jnp.einsum('bnqkgd,bckd->bnkgqc', qb, kc).astype(jnp.float32) * scale
    logits = jnp.concatenate([s_loc, s_ctx, sink_column(sink, s_loc.shape[:-1])], axis=-1)
    p = jax.nn.softmax(logits, axis=-1)
    n_loc = 3 * BLOCK
    n_ctx = kc.shape[1]
    p_loc = p[..., :n_loc].astype(v.dtype)
    p_ctx = p[..., n_loc:n_loc + n_ctx].astype(v.dtype)
    out = jnp.einsum('bnkgqj,bnjkd->bnqkgd', p_loc, vb) + jnp.einsum('bnkgqc,bckd->bnqkgd', p_ctx, vc)
    return out.reshape(B, S, ATTN_WIDTH)


def context_gqa(qc, kc, vc, sink):
    B, C = qc.shape[0], qc.shape[1]
    scale = HEAD_DIM ** -0.5
    qg = qc.reshape(B, C, N_KV_HEADS, Q_PER_KV, HEAD_DIM)
    s = jnp.einsum('bqkgd,bckd->bkgqc', qg, kc).astype(jnp.float32) * scale
    p = jax.nn.softmax(jnp.concatenate([s, sink_column(sink, s.shape[:-1])], axis=-1), axis=-1)[..., :C]
    out = jnp.einsum('bkgqc,bckd->bqkgd', p.astype(vc.dtype), vc)
    return out.reshape(B, C, ATTN_WIDTH)


def fourier_mix(u, w_four):
    B, N = u.shape[0], u.shape[1]
    ug = u.reshape(B, N, N_FOURIER_GROUPS, FOURIER_GROUP_DIM).astype(jnp.float32)
    mixed = jnp.fft.fft2(ug, axes=(1, 3), norm='ortho').real.astype(u.dtype)
    return jnp.einsum('bngc,gcd->bngd', mixed, w_four).reshape(B, N, FOURIER_WIDTH)


def swiglu(h, w_gate, w_up, w_down):
    return (jax.nn.silu(h @ w_gate) * (h @ w_up)) @ w_down


def setup_inputs(seed: int = 0) -> dict:
    key = jax.random.key(seed)
    ks = jax.random.split(key, 17)
    f32 = jnp.float32
    D = D_MODEL

    def nrm(k, shape, s):
        return jax.random.normal(k, shape, f32) * s

    return {
        'x': nrm(ks[0], (BATCH, SEQ, D), 1.0),
        'c': nrm(ks[1], (BATCH, D), 1.0),
        'ctx': nrm(ks[2], (BATCH, CTX_LEN, D), 1.0),
        'c_ctx': nrm(ks[3], (D,), 1.0),
        'w_ada': nrm(ks[4], (DEPTH, D, N_MOD * D), 0.5 * D ** -0.5),
        'b_ada': nrm(ks[5], (DEPTH, N_MOD * D), 0.02),
        'norm_pre_mix': 1.0 + nrm(ks[6], (DEPTH, D), 0.05),
        'norm_post_mix': 1.0 + nrm(ks[7], (DEPTH, D), 0.05),
        'norm_pre_ffn': 1.0 + nrm(ks[8], (DEPTH, D), 0.05),
        'norm_post_ffn': 1.0 + nrm(ks[9], (DEPTH, D), 0.05),
        'w_in': nrm(ks[10], (DEPTH, D, IN_WIDTH), D ** -0.5),
        'w_out': nrm(ks[11], (DEPTH, MIX_WIDTH, D), MIX_WIDTH ** -0.5),
        'w_fourier': nrm(ks[12], (DEPTH, N_FOURIER_GROUPS, FOURIER_GROUP_DIM, FOURIER_GROUP_DIM), FOURIER_GROUP_DIM ** -0.5),
        'sink': nrm(ks[13], (DEPTH, N_Q_HEADS), 0.5),
        'w_gate': nrm(ks[14], (DEPTH, D, FFN_HIDDEN), D ** -0.5),
        'w_up': nrm(ks[15], (DEPTH, D, FFN_HIDDEN), D ** -0.5),
        'w_down': nrm(ks[16], (DEPTH, FFN_HIDDEN, D), FFN_HIDDEN ** -0.5),
    }


def reference(x, c, ctx, c_ctx, w_ada, b_ada, norm_pre_mix, norm_post_mix, norm_pre_ffn, norm_post_ffn,
              w_in, w_out, w_fourier, sink, w_gate, w_up, w_down):
    B, S, D = x.shape
    ROWS = S // GRID_W
    cos, sin = axial_rope_tables(ROWS)
    xc = ctx
    kv_lo, kv_hi = ATTN_WIDTH, ATTN_WIDTH + 2 * KV_WIDTH
    for i in range(DEPTH):
        update_ctx = i < DEPTH - 1
        mod_lat = jax.nn.silu(c) @ w_ada[i] + b_ada[i]
        mod_ctx = jax.nn.silu(c_ctx) @ w_ada[i] + b_ada[i]
        sh1, sc1, g1, sh2, sc2, g2 = [m[:, None, :] for m in jnp.split(mod_lat, N_MOD, axis=-1)]
        csh1, csc1, cg1, csh2, csc2, cg2 = jnp.split(mod_ctx, N_MOD, axis=-1)

        h = modulate(rmsnorm(x, norm_pre_mix[i]), sh1, sc1)
        hc = modulate(rmsnorm(xc, norm_pre_mix[i]), csh1, csc1)
        p = h @ w_in[i]
        q = apply_axial_rope(p[..., :ATTN_WIDTH].reshape(B, S, N_Q_HEADS, HEAD_DIM), cos, sin)
        k = apply_axial_rope(p[..., kv_lo:kv_lo + KV_WIDTH].reshape(B, S, N_KV_HEADS, HEAD_DIM), cos, sin)
        v = p[..., kv_lo + KV_WIDTH:kv_hi].reshape(B, S, N_KV_HEADS, HEAD_DIM)
        u = p[..., kv_hi:]
        kvc = hc @ w_in[i][:, kv_lo:kv_hi]
        C = xc.shape[1]
        kc = kvc[..., :KV_WIDTH].reshape(B, C, N_KV_HEADS, HEAD_DIM)
        vc = kvc[..., KV_WIDTH:].reshape(B, C, N_KV_HEADS, HEAD_DIM)

        attn = windowed_gqa_with_context(q, k, v, kc, vc, sink[i])
        four = fourier_mix(u, w_fourier[i])
        mix = jnp.concatenate([attn, four], axis=-1) @ w_out[i]
        x_mid = x + g1 * rmsnorm(mix, norm_post_mix[i])

        hf = modulate(rmsnorm(x_mid, norm_pre_ffn[i]), sh2, sc2)
        x = x_mid + g2 * rmsnorm(swiglu(hf, w_gate[i], w_up[i], w_down[i]), norm_post_ffn[i])

        if update_ctx:
            qc = (hc @ w_in[i][:, :ATTN_WIDTH]).reshape(B, C, N_Q_HEADS, HEAD_DIM)
            uc = hc @ w_in[i][:, kv_hi:]
            attn_c = context_gqa(qc, kc, vc, sink[i])
            four_c = fourier_mix(uc, w_fourier[i])
            mix_c = jnp.concatenate([attn_c, four_c], axis=-1) @ w_out[i]
            xc_mid = xc + cg1 * rmsnorm(mix_c, norm_post_mix[i])
            hfc = modulate(rmsnorm(xc_mid, norm_pre_ffn[i]), csh2, csc2)
            xc = xc_mid + cg2 * rmsnorm(swiglu(hfc, w_gate[i], w_up[i], w_down[i]), norm_post_ffn[i])
    return x
```

```python
import functools
import math

import numpy as np
import jax
import jax.numpy as jnp
from jax import lax
from jax.experimental import pallas as pl
from jax.experimental.pallas import tpu as pltpu

F32 = jnp.float32
BF16 = jnp.bfloat16
HIGHEST = lax.Precision.HIGHEST

GRID_W = 64
HEAD_DIM = 64
N_Q_HEADS = 8
N_KV_HEADS = 2
Q_PER_KV = N_Q_HEADS // N_KV_HEADS
ATTN_WIDTH = N_Q_HEADS * HEAD_DIM
KV_WIDTH = N_KV_HEADS * HEAD_DIM
N_FOURIER_GROUPS = 8
FOURIER_GROUP_DIM = 64
FOURIER_WIDTH = N_FOURIER_GROUPS * FOURIER_GROUP_DIM
WINDOW = 128
ROPE_THETA = 10000.0
ROPE_PAIRS_PER_AXIS = HEAD_DIM // 4
RMS_EPS = 1e-6
N_MOD = 6
NEG_INF = -1e30

LANES = 128
VMEM_LIMIT_BYTES = 52 * 1024 * 1024

DFT_A = 256
DFT_C = 16

MOD_ROWS = 8
TOKEN_TILE = 512
FFN_CHUNK = 512


def _resident(shape):
    zeros = (0,) * len(shape)
    return pl.BlockSpec(shape, lambda *_: zeros, pipeline_mode=pl.Buffered(1))


def _rms(xf, g):
    ms = jnp.mean(xf * xf, axis=-1, keepdims=True)
    return xf * lax.rsqrt(ms + RMS_EPS) * g


def _ada_kernel(c_ref, w_ref, b_ref, o_ref):
    c = c_ref[...]
    a = c * jax.nn.sigmoid(c)
    o_ref[0] = jnp.dot(a, w_ref[0], precision=HIGHEST, preferred_element_type=F32) + b_ref[0]


def _ada(cc, w_ada, b_ada):
    depth, d, n = w_ada.shape
    tn = 1536
    return pl.pallas_call(
        _ada_kernel,
        grid=(depth, n // tn),
        in_specs=[
            pl.BlockSpec((MOD_ROWS, d), lambda i, j: (0, 0)),
            pl.BlockSpec((1, d, tn), lambda i, j: (i, 0, j)),
            pl.BlockSpec((1, 1, tn), lambda i, j: (i, 0, j)),
        ],
        out_specs=pl.BlockSpec((1, MOD_ROWS, tn), lambda i, j: (i, 0, j)),
        out_shape=jax.ShapeDtypeStruct((depth, MOD_ROWS, n), F32),
        compiler_params=pltpu.CompilerParams(vmem_limit_bytes=VMEM_LIMIT_BYTES),
        name="ada",
    )(cc, w_ada, b_ada.reshape(depth, 1, n))


def _fold_kernel(c64_ref, s64_ref, wf_ref, wo_ref, wr_ref, wi_ref):
    t = jnp.dot(wf_ref[0, 0], wo_ref[0], precision=HIGHEST, preferred_element_type=F32)
    wr_ref[0] = jnp.dot(c64_ref[...], t, precision=HIGHEST, preferred_element_type=F32).astype(BF16)
    wi_ref[0] = jnp.dot(s64_ref[...], t, precision=HIGHEST, preferred_element_type=F32).astype(BF16)


def _fold_fourier_weights(w_fourier, w_out):
    depth, groups, gd, _ = w_fourier.shape
    d = w_out.shape[-1]
    j = np.arange(gd)
    ang = 2.0 * np.pi * ((j[:, None] * j[None, :]) % gd) / gd
    c64 = jnp.asarray(np.cos(ang), F32)
    s64 = jnp.asarray(np.sin(ang), F32)
    first = ATTN_WIDTH // gd
    out = jax.ShapeDtypeStruct((depth, groups * gd, d), BF16)
    return pl.pallas_call(
        _fold_kernel,
        grid=(depth, groups),
        in_specs=[
            pl.BlockSpec((gd, gd), lambda i, g: (0, 0)),
            pl.BlockSpec((gd, gd), lambda i, g: (0, 0)),
            pl.BlockSpec((1, 1, gd, gd), lambda i, g: (i, g, 0, 0)),
            pl.BlockSpec((1, gd, d), lambda i, g: (i, first + g, 0)),
        ],
        out_specs=[pl.BlockSpec((1, gd, d), lambda i, g: (i, g, 0))] * 2,
        out_shape=[out, out],
        name="fold_fourier",
    )(c64, s64, w_fourier, w_out)


def _proj_kernel(x_ref, sh_ref, sc_ref, g_ref, w_ref, *rest, rope):
    if rope:
        cos_ref, sin_ref, q_ref, k_ref, v_ref, u_ref = rest
    else:
        q_ref, k_ref, v_ref, u_ref = rest
    h = _rms(x_ref[0], g_ref[...]) * (1.0 + sc_ref[0]) + sh_ref[0]
    p = jnp.dot(h.astype(BF16), w_ref[...], preferred_element_type=F32)
    k_lo = ATTN_WIDTH
    v_lo = ATTN_WIDTH + KV_WIDTH
    u_lo = ATTN_WIDTH + 2 * KV_WIDTH
    if rope:
        cos = cos_ref[...]
        sin = sin_ref[...]
        lane = lax.broadcasted_iota(jnp.int32, cos.shape, 1)
        half = HEAD_DIM // 2
        first_half = (lane & (HEAD_DIM - 1)) < half

        def rot(xs):
            partner = jnp.where(first_half, pltpu.roll(xs, LANES - half, 1), pltpu.roll(xs, half, 1))
            return xs * cos + partner * sin

        for j in range(ATTN_WIDTH // LANES):
            q_ref[0, :, j * LANES:(j + 1) * LANES] = rot(p[:, j * LANES:(j + 1) * LANES]).astype(BF16)
        k_ref[0] = rot(p[:, k_lo:v_lo]).astype(BF16)
    else:
        q_ref[0] = p[:, :k_lo].astype(BF16)
        k_ref[0] = p[:, k_lo:v_lo].astype(BF16)
    v_ref[0] = p[:, v_lo:u_lo].astype(BF16)
    u_ref[0] = p[:, u_lo:].astype(BF16)


def _project(x, sh, sc, g, w_in, rope_tables):
    bx, t, d = x.shape
    tm = min(TOKEN_TILE, t)
    n_in = w_in.shape[1]
    rope = rope_tables is not None
    tok = lambda width: pl.BlockSpec((1, tm, width), lambda b, i: (b, i, 0))
    vec = pl.BlockSpec((1, 1, d), lambda b, i: (b, 0, 0))
    in_specs = [tok(d), vec, vec, _resident((1, d)), _resident((d, n_in))]
    args = [x, sh, sc, g, w_in]
    if rope:
        in_specs += [pl.BlockSpec((tm, LANES), lambda b, i: (i, 0))] * 2
        args += list(rope_tables)
    widths = (ATTN_WIDTH, KV_WIDTH, KV_WIDTH, FOURIER_WIDTH)
    return pl.pallas_call(
        functools.partial(_proj_kernel, rope=rope),
        grid=(bx, t // tm),
        in_specs=in_specs,
        out_specs=[tok(w) for w in widths],
        out_shape=[jax.ShapeDtypeStruct((bx, t, w), BF16) for w in widths],
        compiler_params=pltpu.CompilerParams(vmem_limit_bytes=VMEM_LIMIT_BYTES),
        name="project_rope" if rope else "project",
    )(*args)


def _attn_kernel(sink_ref, q_ref, *rest, local):
    if local:
        kp_ref, kc_ref, kn_ref, vp_ref, vc_ref, vn_ref, kx_ref, vx_ref, o_ref = rest
        kcat = jnp.concatenate([kp_ref[0], kc_ref[0], kn_ref[0], kx_ref[0]], axis=0)
        vcat = jnp.concatenate([vp_ref[0], vc_ref[0], vn_ref[0], vx_ref[0]], axis=0)
    else:
        kx_ref, vx_ref, o_ref = rest
        kcat = kx_ref[0]
        vcat = vx_ref[0]
    q = q_ref[0]
    tq = q.shape[0]
    rows = Q_PER_KV * tq
    assert tq & (tq - 1) == 0
    group = lax.shift_right_logical(lax.broadcasted_iota(jnp.int32, (rows, 1), 0), int(math.log2(tq)))
    if local:
        n = pl.program_id(1)
        last = pl.num_programs(1) - 1
        far = 1 << 20
        qi = lax.broadcasted_iota(jnp.int32, (rows, WINDOW), 0) & (tq - 1)
        kj = lax.broadcasted_iota(jnp.int32, (rows, WINDOW), 1)
        ok_prev = kj >= qi + jnp.where(n > 0, 0, far)
        ok_next = kj <= qi - jnp.where(n < last, 0, far)
    for h in range(N_KV_HEADS):
        heads = [Q_PER_KV * h + g for g in range(Q_PER_KV)]
        qs = jnp.concatenate([q[:, a * HEAD_DIM:(a + 1) * HEAD_DIM] for a in heads], axis=0)
        kh = kcat[:, h * HEAD_DIM:(h + 1) * HEAD_DIM]
        vh = vcat[:, h * HEAD_DIM:(h + 1) * HEAD_DIM]
        s = lax.dot_general(qs, kh, (((1,), (1,)), ((), ())), preferred_element_type=F32)
        if local:
            s = jnp.concatenate([
                jnp.where(ok_prev, s[:, :WINDOW], NEG_INF),
                s[:, WINDOW:2 * WINDOW],
                jnp.where(ok_next, s[:, 2 * WINDOW:3 * WINDOW], NEG_INF),
                s[:, 3 * WINDOW:],
            ], axis=1)
        sink = jnp.full((rows, 1), sink_ref[heads[0]], F32)
        for g in range(1, Q_PER_KV):
            sink = jnp.where(group == g, sink_ref[heads[g]], sink)
        m = jnp.maximum(jnp.max(s, axis=-1, keepdims=True), sink)
        p = jnp.exp(s - m)
        denom = jnp.sum(p, axis=-1, keepdims=True) + jnp.exp(sink - m)
        acc = jnp.dot(p.astype(BF16), vh, preferred_element_type=F32)
        o = acc / denom
        lo = h * Q_PER_KV * HEAD_DIM
        o_ref[0, :, lo:lo + Q_PER_KV * HEAD_DIM] = jnp.concatenate(
            [o[g * tq:(g + 1) * tq] for g in range(Q_PER_KV)], axis=1).astype(BF16)


def _attention(q, k, v, kx, vx, sink, local):
    bx, t, _ = q.shape
    cx = kx.shape[1]
    smem = pl.BlockSpec(memory_space=pltpu.SMEM)
    ctx = pl.BlockSpec((1, cx, KV_WIDTH), lambda b, n: (b, 0, 0))
    if local:
        tq = WINDOW
        nb = t // tq
        blk = lambda f: pl.BlockSpec((1, tq, KV_WIDTH), f)
        prev = blk(lambda b, n: (b, jnp.maximum(n - 1, 0), 0))
        cur = blk(lambda b, n: (b, n, 0))
        nxt = blk(lambda b, n: (b, jnp.minimum(n + 1, nb - 1), 0))
        in_specs = [smem, pl.BlockSpec((1, tq, ATTN_WIDTH), lambda b, n: (b, n, 0)),
                    prev, cur, nxt, prev, cur, nxt, ctx, ctx]
        args = (sink, q, k, k, k, v, v, v, kx, vx)
    else:
        tq = t
        nb = 1
        in_specs = [smem, pl.BlockSpec((1, tq, ATTN_WIDTH), lambda b, n: (b, n, 0)), ctx, ctx]
        args = (sink, q, kx, vx)
    return pl.pallas_call(
        functools.partial(_attn_kernel, local=local),
        grid=(bx, nb),
        in_specs=in_specs,
        out_specs=pl.BlockSpec((1, tq, ATTN_WIDTH), lambda b, n: (b, n, 0)),
        out_shape=jax.ShapeDtypeStruct((bx, t, ATTN_WIDTH), BF16),
        compiler_params=pltpu.CompilerParams(vmem_limit_bytes=VMEM_LIMIT_BYTES),
        name="attn_window" if local else "attn_context",
    )(*args)


def _dft_a_kernel(fa_ref, u_ref, *rest, twiddle):
    y = jnp.dot(fa_ref[...].astype(BF16), u_ref[0], preferred_element_type=F32)
    yr = y[:DFT_A]
    yi = y[DFT_A:]
    if twiddle:
        tr_ref, ti_ref, yr_ref, yi_ref = rest
        reps = y.shape[1] // LANES
        tr = jnp.concatenate([tr_ref[0]] * reps, axis=1)
        ti = jnp.concatenate([ti_ref[0]] * reps, axis=1)
        yr, yi = yr * tr - yi * ti, yr * ti + yi * tr
    else:
        yr_ref, yi_ref = rest
    yr_ref[0] = yr.astype(BF16)
    yi_ref[0] = yi.astype(BF16)


def _dft_c_kernel(g_ref, yr_ref, yi_ref, ur_ref, ui_ref):
    y = jnp.concatenate([yr_ref[0], yi_ref[0]], axis=0)
    r = jnp.dot(g_ref[...].astype(BF16), y, preferred_element_type=F32).astype(BF16)
    half = DFT_C * DFT_C
    for k2 in range(DFT_C):
        ur_ref[0, k2] = r[k2 * DFT_C:(k2 + 1) * DFT_C]
        ui_ref[0, k2] = r[half + k2 * DFT_C:half + (k2 + 1) * DFT_C]


def _dft_tables(n_pos):
    a = np.arange(DFT_A)
    ang = 2.0 * np.pi * ((a[:, None] * a[None, :]) % DFT_A) / DFT_A
    norm = 1.0 / math.sqrt(n_pos * FOURIER_GROUP_DIM)
    fa = jnp.asarray(np.concatenate([np.cos(ang), -np.sin(ang)], axis=0) * norm, F32)
    if n_pos == DFT_A:
        return fa, None, None, None
    assert n_pos == DFT_A * DFT_C
    n2 = np.arange(DFT_C)
    tw = 2.0 * np.pi * (n2[:, None] * a[None, :]) / n_pos
    tr = jnp.asarray(np.repeat(np.cos(tw)[:, :, None], LANES, axis=2), F32)
    ti = jnp.asarray(np.repeat(-np.sin(tw)[:, :, None], LANES, axis=2), F32)
    ang16 = 2.0 * np.pi * ((n2[:, None] * n2[None, :]) % DFT_C) / DFT_C
    c16, s16 = np.cos(ang16), np.sin(ang16)
    eye = np.eye(DFT_C)
    kron = lambda f: np.einsum("kn,ab->kabn", f, eye).reshape(DFT_C * DFT_C, DFT_C * DFT_C)
    gmat = np.block([[kron(c16), kron(s16)], [kron(-s16), kron(c16)]])
    return fa, tr, ti, jnp.asarray(gmat, F32)


def _position_dft(u):
    bx, n_pos, w = u.shape
    fa, tr, ti, gmat = _dft_tables(n_pos)
    out = jax.ShapeDtypeStruct((bx, DFT_A, (n_pos // DFT_A) * w), BF16)
    blk = pl.BlockSpec((1, DFT_A, w), lambda b, j: (b, 0, j))
    fa_spec = _resident((2 * DFT_A, DFT_A))
    if n_pos == DFT_A:
        return pl.pallas_call(
            functools.partial(_dft_a_kernel, twiddle=False),
            grid=(bx, 1), in_specs=[fa_spec, blk], out_specs=[blk, blk], out_shape=[out, out],
            name="dft_context",
        )(fa, u)
    tw_spec = pl.BlockSpec((1, DFT_A, LANES), lambda b, j: (j, 0, 0))
    yr, yi = pl.pallas_call(
        functools.partial(_dft_a_kernel, twiddle=True),
        grid=(bx, DFT_C), in_specs=[fa_spec, blk, tw_spec, tw_spec], out_specs=[blk, blk],
        out_shape=[out, out], name="dft_stage_a",
    )(fa, u.reshape(bx, DFT_A, DFT_C * w), tr, ti)
    rows = DFT_C * DFT_C
    yblk = pl.BlockSpec((1, rows, w), lambda b, j: (b, j, 0))
    ublk = pl.BlockSpec((1, DFT_C, DFT_C, w), lambda b, j: (b, 0, j, 0))
    uout = jax.ShapeDtypeStruct((bx, DFT_C, DFT_A, w), BF16)
    ur, ui = pl.pallas_call(
        _dft_c_kernel,
        grid=(bx, DFT_A // DFT_C), in_specs=[_resident((2 * rows, 2 * rows)), yblk, yblk],
        out_specs=[ublk, ublk], out_shape=[uout, uout], name="dft_stage_c",
    )(gmat, yr.reshape(bx, n_pos, w), yi.reshape(bx, n_pos, w))
    return ur.reshape(bx, n_pos, w), ui.reshape(bx, n_pos, w)


def _out_kernel(a_ref, ur_ref, ui_ref, x_ref, wa_ref, wr_ref, wi_ref, gpost_ref, gate_ref, o_ref):
    mix = jnp.dot(a_ref[0], wa_ref[...], preferred_element_type=F32)
    mix += jnp.dot(ur_ref[0], wr_ref[...], preferred_element_type=F32)
    mix += jnp.dot(ui_ref[0], wi_ref[...], preferred_element_type=F32)
    o_ref[0] = x_ref[0] + gate_ref[0] * _rms(mix, gpost_ref[...])


def _mix_out(attn, ur, ui, x, wa, wr, wi, gpost, gate):
    bx, t, d = x.shape
    tm = min(TOKEN_TILE, t)
    tok = lambda width: pl.BlockSpec((1, tm, width), lambda b, i: (b, i, 0))
    vec = pl.BlockSpec((1, 1, d), lambda b, i: (b, 0, 0))
    wspec = _resident(wa.shape)
    return pl.pallas_call(
        _out_kernel,
        grid=(bx, t // tm),
        in_specs=[tok(ATTN_WIDTH), tok(FOURIER_WIDTH), tok(FOURIER_WIDTH), tok(d),
                  wspec, wspec, wspec, _resident((1, d)), vec],
        out_specs=tok(d),
        out_shape=jax.ShapeDtypeStruct((bx, t, d), F32),
        compiler_params=pltpu.CompilerParams(vmem_limit_bytes=VMEM_LIMIT_BYTES),
        name="mix_out",
    )(attn, ur, ui, x, wa, wr, wi, gpost, gate)


def _ffn_kernel(x_ref, sh_ref, sc_ref, gate_ref, gpre_ref, gpost_ref, wg_ref, wu_ref, wd_ref, o_ref):
    x = x_ref[0]
    h = (_rms(x, gpre_ref[...]) * (1.0 + sc_ref[0]) + sh_ref[0]).astype(BF16)
    hidden = wg_ref.shape[1]
    acc = jnp.zeros(x.shape, F32)
    for lo in range(0, hidden, FFN_CHUNK):
        hi = min(lo + FFN_CHUNK, hidden)
        g = jnp.dot(h, wg_ref[:, lo:hi], preferred_element_type=F32)
        u = jnp.dot(h, wu_ref[:, lo:hi], preferred_element_type=F32)
        a = (g * jax.nn.sigmoid(g) * u).astype(BF16)
        acc += jnp.dot(a, wd_ref[lo:hi, :], preferred_element_type=F32)
    o_ref[0] = x + gate_ref[0] * _rms(acc, gpost_ref[...])


def _ffn(x, sh, sc, gate, gpre, gpost, wg, wu, wd):
    bx, t, d = x.shape
    tm = min(TOKEN_TILE, t)
    tok = pl.BlockSpec((1, tm, d), lambda b, i: (b, i, 0))
    vec = pl.BlockSpec((1, 1, d), lambda b, i: (b, 0, 0))
    return pl.pallas_call(
        _ffn_kernel,
        grid=(bx, t // tm),
        in_specs=[tok, vec, vec, vec, _resident((1, d)), _resident((1, d)),
                  _resident(wg.shape), _resident(wu.shape), _resident(wd.shape)],
        out_specs=tok,
        out_shape=jax.ShapeDtypeStruct((bx, t, d), F32),
        compiler_params=pltpu.CompilerParams(vmem_limit_bytes=VMEM_LIMIT_BYTES),
        name="ffn",
    )(x, sh, sc, gate, gpre, gpost, wg, wu, wd)


def _rope_tables(seq):
    t = jnp.arange(seq)
    row = (t // GRID_W).astype(F32)
    col = (t % GRID_W).astype(F32)
    freqs = ROPE_THETA ** (-jnp.arange(ROPE_PAIRS_PER_AXIS, dtype=F32) / ROPE_PAIRS_PER_AXIS)
    ang = jnp.concatenate([row[:, None] * freqs, col[:, None] * freqs], axis=-1)
    cos, sin = jnp.cos(ang), jnp.sin(ang)
    heads_per_row = LANES // HEAD_DIM
    return (jnp.tile(jnp.concatenate([cos, cos], axis=-1), (1, heads_per_row)),
            jnp.tile(jnp.concatenate([-sin, sin], axis=-1), (1, heads_per_row)))


def _prepare_w_in(w_in):
    d = w_in.shape[0]
    rot_width = ATTN_WIDTH + KV_WIDTH
    pairs = w_in[:, :rot_width].reshape(d, rot_width // HEAD_DIM, HEAD_DIM // 2, 2)
    rot = jnp.swapaxes(pairs, 2, 3).reshape(d, rot_width)
    scale = np.ones((w_in.shape[1],), np.float32)
    scale[:ATTN_WIDTH] = HEAD_DIM ** -0.5
    return (jnp.concatenate([rot, w_in[:, rot_width:]], axis=1) * scale).astype(BF16)


def kernel(x, c, ctx, c_ctx, w_ada, b_ada, norm_pre_mix, norm_post_mix, norm_pre_ffn, norm_post_ffn,
           w_in, w_out, w_fourier, sink, w_gate, w_up, w_down):
    batch, seq, d = x.shape
    depth = w_ada.shape[0]
    assert batch + 1 <= MOD_ROWS

    cc = jnp.zeros((MOD_ROWS, d), F32).at[:batch].set(c).at[batch].set(c_ctx)
    mod = _ada(cc, w_ada, b_ada)
    w_four_r, w_four_i = _fold_fourier_weights(w_fourier, w_out)
    rope = _rope_tables(seq)

    xc = ctx
    for i in range(depth):
        lat = [m.reshape(batch, 1, d) for m in jnp.split(mod[i, :batch], N_MOD, axis=-1)]
        cmod = [jnp.broadcast_to(m.reshape(1, 1, d), (batch, 1, d))
                for m in jnp.split(mod[i, batch], N_MOD, axis=-1)]
        sh1, sc1, g1, sh2, sc2, g2 = lat
        csh1, csc1, cg1, csh2, csc2, cg2 = cmod
        g_pre_mix = norm_pre_mix[i].reshape(1, d)
        g_post_mix = norm_post_mix[i].reshape(1, d)
        g_pre_ffn = norm_pre_ffn[i].reshape(1, d)
        g_post_ffn = norm_post_ffn[i].reshape(1, d)
        w_in_i = _prepare_w_in(w_in[i])
        w_attn_out = w_out[i, :ATTN_WIDTH].astype(BF16)
        wg, wu, wd = w_gate[i].astype(BF16), w_up[i].astype(BF16), w_down[i].astype(BF16)

        def finish(stream, attn, ur, ui, gate1, sh, sc, gate2):
            mid = _mix_out(attn, ur, ui, stream, w_attn_out, w_four_r[i], w_four_i[i], g_post_mix, gate1)
            return _ffn(mid, sh, sc, gate2, g_pre_ffn, g_post_ffn, wg, wu, wd)

        q, k, v, u = _project(x, sh1, sc1, g_pre_mix, w_in_i, rope)
        qc, kc, vc, uc = _project(xc, csh1, csc1, g_pre_mix, w_in_i, None)
        attn = _attention(q, k, v, kc, vc, sink[i], local=True)
        ur, ui = _position_dft(u)
        x = finish(x, attn, ur, ui, g1, sh2, sc2, g2)

        if i < depth - 1:
            attn_c = _attention(qc, None, None, kc, vc, sink[i], local=False)
            urc, uic = _position_dft(uc)
            xc = finish(xc, attn_c, urc, uic, cg1, csh2, csc2, cg2)
    return x
```

```python
import functools
import math

import numpy as np
import jax
import jax.numpy as jnp
from jax import lax
from jax.experimental import pallas as pl
from jax.experimental.pallas import tpu as pltpu

F32 = jnp.float32
BF16 = jnp.bfloat16
HIGHEST = lax.Precision.HIGHEST

GRID_W = 64
HEAD_DIM = 64
N_Q_HEADS = 8
N_KV_HEADS = 2
Q_PER_KV = N_Q_HEADS // N_KV_HEADS
ATTN_WIDTH = N_Q_HEADS * HEAD_DIM
KV_WIDTH = N_KV_HEADS * HEAD_DIM
N_FOURIER_GROUPS = 8
FOURIER_GROUP_DIM = 64
FOURIER_WIDTH = N_FOURIER_GROUPS * FOURIER_GROUP_DIM
WINDOW = 128
ROPE_THETA = 10000.0
ROPE_PAIRS_PER_AXIS = HEAD_DIM // 4
RMS_EPS = 1e-6
N_MOD = 6
NEG_INF = -1e30

LANES = 128
VMEM_LIMIT_BYTES = 52 * 1024 * 1024

DFT_A = 256
DFT_C = 16

MOD_ROWS = 8
TOKEN_TILE = 512
FFN_CHUNK = 512


def _resident(shape):
    zeros = (0,) * len(shape)
    return pl.BlockSpec(shape, lambda *_: zeros, pipeline_mode=pl.Buffered(1))


def _rms(xf, g):
    ms = jnp.mean(xf * xf, axis=-1, keepdims=True)
    return xf * lax.rsqrt(ms + RMS_EPS) * g


def _ada_kernel(c_ref, w_ref, b_ref, o_ref):
    c = c_ref[...]
    a = c * jax.nn.sigmoid(c)
    o_ref[0] = jnp.dot(a, w_ref[0], precision=HIGHEST, preferred_element_type=F32) + b_ref[0]


def _ada(cc, w_ada, b_ada):
    depth, d, n = w_ada.shape
    tn = 1536
    return pl.pallas_call(
        _ada_kernel,
        grid=(depth, n // tn),
        in_specs=[
            pl.BlockSpec((MOD_ROWS, d), lambda i, j: (0, 0)),
            pl.BlockSpec((1, d, tn), lambda i, j: (i, 0, j)),
            pl.BlockSpec((1, 1, tn), lambda i, j: (i, 0, j)),
        ],
        out_specs=pl.BlockSpec((1, MOD_ROWS, tn), lambda i, j: (i, 0, j)),
        out_shape=jax.ShapeDtypeStruct((depth, MOD_ROWS, n), F32),
        compiler_params=pltpu.CompilerParams(vmem_limit_bytes=VMEM_LIMIT_BYTES),
        name="ada",
    )(cc, w_ada, b_ada.reshape(depth, 1, n))


def _fold_kernel(c64_ref, s64_ref, wf_ref, wo_ref, wr_ref, wi_ref):
    t = jnp.dot(wf_ref[0, 0], wo_ref[0], precision=HIGHEST, preferred_element_type=F32)
    wr_ref[0] = jnp.dot(c64_ref[...], t, precision=HIGHEST, preferred_element_type=F32).astype(BF16)
    wi_ref[0] = jnp.dot(s64_ref[...], t, precision=HIGHEST, preferred_element_type=F32).astype(BF16)


def _fold_fourier_weights(w_fourier, w_out):
    depth, groups, gd, _ = w_fourier.shape
    d = w_out.shape[-1]
    j = np.arange(gd)
    ang = 2.0 * np.pi * ((j[:, None] * j[None, :]) % gd) / gd
    c64 = jnp.asarray(np.cos(ang), F32)
    s64 = jnp.asarray(np.sin(ang), F32)
    first = ATTN_WIDTH // gd
    out = jax.ShapeDtypeStruct((depth, groups * gd, d), BF16)
    return pl.pallas_call(
        _fold_kernel,
        grid=(depth, groups),
        in_specs=[
            pl.BlockSpec((gd, gd), lambda i, g: (0, 0)),
            pl.BlockSpec((gd, gd), lambda i, g: (0, 0)),
            pl.BlockSpec((1, 1, gd, gd), lambda i, g: (i, g, 0, 0)),
            pl.BlockSpec((1, gd, d), lambda i, g: (i, first + g, 0)),
        ],
        out_specs=[pl.BlockSpec((1, gd, d), lambda i, g: (i, g, 0))] * 2,
        out_shape=[out, out],
        name="fold_fourier",
    )(c64, s64, w_fourier, w_out)


def _proj_kernel(x_ref, sh_ref, sc_ref, g_ref, w_ref, *rest, rope):
    if rope:
        cos_ref, sin_ref, q_ref, k_ref, v_ref, u_ref, u_scratch = rest
    else:
        q_ref, k_ref, v_ref, u_ref = rest
    h = _rms(x_ref[0], g_ref[...]) * (1.0 + sc_ref[0]) + sh_ref[0]
    p = jnp.dot(h.astype(BF16), w_ref[...], preferred_element_type=F32)
    k_lo = ATTN_WIDTH
    v_lo = ATTN_WIDTH + KV_WIDTH
    u_lo = ATTN_WIDTH + 2 * KV_WIDTH
    if rope:
        cos = cos_ref[...]
        sin = sin_ref[...]
        lane = lax.broadcasted_iota(jnp.int32, cos.shape, 1)
        half = HEAD_DIM // 2
        first_half = (lane & (HEAD_DIM - 1)) < half

        def rot(xs):
            partner = jnp.where(first_half, pltpu.roll(xs, LANES - half, 1), pltpu.roll(xs, half, 1))
            return xs * cos + partner * sin

        for j in range(ATTN_WIDTH // LANES):
            q_ref[0, :, j * LANES:(j + 1) * LANES] = rot(p[:, j * LANES:(j + 1) * LANES]).astype(BF16)
        k_ref[0] = rot(p[:, k_lo:v_lo]).astype(BF16)
    else:
        q_ref[0] = p[:, :k_lo].astype(BF16)
        k_ref[0] = p[:, k_lo:v_lo].astype(BF16)
    v_ref[0] = p[:, v_lo:u_lo].astype(BF16)
    if rope:
        per = u_scratch.shape[1] // DFT_C
        for c in range(u_scratch.shape[0]):
            u_scratch[c] = p[:, u_lo + c * LANES:u_lo + (c + 1) * LANES]
            for n2 in range(DFT_C):
                u_ref[0, n2, :, c * LANES:(c + 1) * LANES] = (
                    u_scratch[c, pl.ds(n2, per, stride=DFT_C), :].astype(BF16))
    else:
        u_ref[0] = p[:, u_lo:].astype(BF16)


def _project(x, sh, sc, g, w_in, rope_tables):
    bx, t, d = x.shape
    tm = min(TOKEN_TILE, t)
    n_in = w_in.shape[1]
    rope = rope_tables is not None
    tok = lambda width: pl.BlockSpec((1, tm, width), lambda b, i: (b, i, 0))
    vec = pl.BlockSpec((1, 1, d), lambda b, i: (b, 0, 0))
    in_specs = [tok(d), vec, vec, _resident((1, d)), _resident((d, n_in))]
    args = [x, sh, sc, g, w_in]
    if rope:
        in_specs += [pl.BlockSpec((tm, LANES), lambda b, i: (i, 0))] * 2
        args += list(rope_tables)
    widths = (ATTN_WIDTH, KV_WIDTH, KV_WIDTH, FOURIER_WIDTH)
    out_specs = [tok(w) for w in widths]
    out_shape = [jax.ShapeDtypeStruct((bx, t, w), BF16) for w in widths]
    scratch = []
    if rope:
        assert t == DFT_A * DFT_C and tm % DFT_C == 0
        out_specs[3] = pl.BlockSpec((1, DFT_C, tm // DFT_C, FOURIER_WIDTH), lambda b, i: (b, 0, i, 0))
        out_shape[3] = jax.ShapeDtypeStruct((bx, DFT_C, DFT_A, FOURIER_WIDTH), BF16)
        scratch = [pltpu.VMEM((FOURIER_WIDTH // LANES, tm, LANES), F32)]
    return pl.pallas_call(
        functools.partial(_proj_kernel, rope=rope),
        grid=(bx, t // tm),
        in_specs=in_specs,
        out_specs=out_specs,
        out_shape=out_shape,
        scratch_shapes=scratch,
        compiler_params=pltpu.CompilerParams(vmem_limit_bytes=VMEM_LIMIT_BYTES),
        name="project_rope" if rope else "project",
    )(*args)


def _attn_kernel(sink_ref, q_ref, *rest, local):
    if local:
        kp_ref, kc_ref, kn_ref, vp_ref, vc_ref, vn_ref, kx_ref, vx_ref, o_ref = rest
        kcat = jnp.concatenate([kp_ref[0], kc_ref[0], kn_ref[0], kx_ref[0]], axis=0)
        vcat = jnp.concatenate([vp_ref[0], vc_ref[0], vn_ref[0], vx_ref[0]], axis=0)
    else:
        kx_ref, vx_ref, o_ref = rest
        kcat = kx_ref[0]
        vcat = vx_ref[0]
    q = q_ref[0]
    tq = q.shape[0]
    rows = Q_PER_KV * tq
    assert tq & (tq - 1) == 0
    group = lax.shift_right_logical(lax.broadcasted_iota(jnp.int32, (rows, 1), 0), int(math.log2(tq)))
    if local:
        n = pl.program_id(1)
        last = pl.num_programs(1) - 1
        far = 1 << 20
        qi = lax.broadcasted_iota(jnp.int32, (rows, WINDOW), 0) & (tq - 1)
        kj = lax.broadcasted_iota(jnp.int32, (rows, WINDOW), 1)
        ok_prev = kj >= qi + jnp.where(n > 0, 0, far)
        ok_next = kj <= qi - jnp.where(n < last, 0, far)
    for h in range(N_KV_HEADS):
        heads = [Q_PER_KV * h + g for g in range(Q_PER_KV)]
        qs = jnp.concatenate([q[:, a * HEAD_DIM:(a + 1) * HEAD_DIM] for a in heads], axis=0)
        kh = kcat[:, h * HEAD_DIM:(h + 1) * HEAD_DIM]
        vh = vcat[:, h * HEAD_DIM:(h + 1) * HEAD_DIM]
        s = lax.dot_general(qs, kh, (((1,), (1,)), ((), ())), preferred_element_type=F32)
        if local:
            s = jnp.concatenate([
                jnp.where(ok_prev, s[:, :WINDOW], NEG_INF),
                s[:, WINDOW:2 * WINDOW],
                jnp.where(ok_next, s[:, 2 * WINDOW:3 * WINDOW], NEG_INF),
                s[:, 3 * WINDOW:],
            ], axis=1)
        sink = jnp.full((rows, 1), sink_ref[heads[0]], F32)
        for g in range(1, Q_PER_KV):
            sink = jnp.where(group == g, sink_ref[heads[g]], sink)
        m = jnp.maximum(jnp.max(s, axis=-1, keepdims=True), sink)
        p = jnp.exp(s - m)
        denom = jnp.sum(p, axis=-1, keepdims=True) + jnp.exp(sink - m)
        acc = jnp.dot(p.astype(BF16), vh, preferred_element_type=F32)
        o = acc / denom
        lo = h * Q_PER_KV * HEAD_DIM
        o_ref[0, :, lo:lo + Q_PER_KV * HEAD_DIM] = jnp.concatenate(
            [o[g * tq:(g + 1) * tq] for g in range(Q_PER_KV)], axis=1).astype(BF16)


def _attention(q, k, v, kx, vx, sink, local):
    bx, t, _ = q.shape
    cx = kx.shape[1]
    smem = pl.BlockSpec(memory_space=pltpu.SMEM)
    ctx = pl.BlockSpec((1, cx, KV_WIDTH), lambda b, n: (b, 0, 0))
    if local:
        tq = WINDOW
        nb = t // tq
        blk = lambda f: pl.BlockSpec((1, tq, KV_WIDTH), f)
        prev = blk(lambda b, n: (b, jnp.maximum(n - 1, 0), 0))
        cur = blk(lambda b, n: (b, n, 0))
        nxt = blk(lambda b, n: (b, jnp.minimum(n + 1, nb - 1), 0))
        in_specs = [smem, pl.BlockSpec((1, tq, ATTN_WIDTH), lambda b, n: (b, n, 0)),
                    prev, cur, nxt, prev, cur, nxt, ctx, ctx]
        args = (sink, q, k, k, k, v, v, v, kx, vx)
    else:
        tq = t
        nb = 1
        in_specs = [smem, pl.BlockSpec((1, tq, ATTN_WIDTH), lambda b, n: (b, n, 0)), ctx, ctx]
        args = (sink, q, kx, vx)
    return pl.pallas_call(
        functools.partial(_attn_kernel, local=local),
        grid=(bx, nb),
        in_specs=in_specs,
        out_specs=pl.BlockSpec((1, tq, ATTN_WIDTH), lambda b, n: (b, n, 0)),
        out_shape=jax.ShapeDtypeStruct((bx, t, ATTN_WIDTH), BF16),
        compiler_params=pltpu.CompilerParams(vmem_limit_bytes=VMEM_LIMIT_BYTES),
        name="attn_window" if local else "attn_context",
    )(*args)


def _dft_short_kernel(fa_ref, u_ref, ur_ref, ui_ref):
    y = jnp.dot(fa_ref[...].astype(BF16), u_ref[0], preferred_element_type=F32)
    ur_ref[0] = y[:DFT_A].astype(BF16)
    ui_ref[0] = y[DFT_A:].astype(BF16)


def _dft_long_kernel(fa_ref, g_ref, tr_ref, ti_ref, u_ref, ur_ref, ui_ref, yr_s, yi_s):
    fa = fa_ref[...].astype(BF16)
    for n2 in range(DFT_C):
        y = jnp.dot(fa, u_ref[0, n2], preferred_element_type=F32)
        tr = tr_ref[n2]
        ti = ti_ref[n2]
        for c in range(y.shape[1] // LANES):
            sl = slice(c * LANES, (c + 1) * LANES)
            yr = y[:DFT_A, sl]
            yi = y[DFT_A:, sl]
            yr_s[n2, :, sl] = (yr * tr - yi * ti).astype(BF16)
            yi_s[n2, :, sl] = (yr * ti + yi * tr).astype(BF16)
    g = g_ref[...].astype(BF16)
    half = DFT_C * DFT_C
    for j in range(DFT_A // DFT_C):
        sl = slice(j * DFT_C, (j + 1) * DFT_C)
        y = jnp.concatenate([yr_s[n2, sl, :] for n2 in range(DFT_C)]
                            + [yi_s[n2, sl, :] for n2 in range(DFT_C)], axis=0)
        r = jnp.dot(g, y, preferred_element_type=F32).astype(BF16)
        for k2 in range(DFT_C):
            ur_ref[0, k2, sl, :] = r[k2 * DFT_C:(k2 + 1) * DFT_C]
            ui_ref[0, k2, sl, :] = r[half + k2 * DFT_C:half + (k2 + 1) * DFT_C]


def _dft_tables(n_pos):
    a = np.arange(DFT_A)
    ang = 2.0 * np.pi * ((a[:, None] * a[None, :]) % DFT_A) / DFT_A
    norm = 1.0 / math.sqrt(n_pos * FOURIER_GROUP_DIM)
    fa = jnp.asarray(np.concatenate([np.cos(ang), -np.sin(ang)], axis=0) * norm, F32)
    if n_pos == DFT_A:
        return fa, None, None, None
    assert n_pos == DFT_A * DFT_C
    n2 = np.arange(DFT_C)
    tw = 2.0 * np.pi * (n2[:, None] * a[None, :]) / n_pos
    tr = jnp.asarray(np.repeat(np.cos(tw)[:, :, None], LANES, axis=2), F32)
    ti = jnp.asarray(np.repeat(-np.sin(tw)[:, :, None], LANES, axis=2), F32)
    ang16 = 2.0 * np.pi * ((n2[:, None] * n2[None, :]) % DFT_C) / DFT_C
    c16, s16 = np.cos(ang16), np.sin(ang16)
    eye = np.eye(DFT_C)
    kron = lambda f: np.einsum("kn,ab->kanb", f, eye).reshape(DFT_C * DFT_C, DFT_C * DFT_C)
    gmat = np.block([[kron(c16), kron(s16)], [kron(-s16), kron(c16)]])
    return fa, tr, ti, jnp.asarray(gmat, F32)


def _position_dft(u):
    bx, w = u.shape[0], u.shape[-1]
    n_pos = math.prod(u.shape[1:-1])
    fa, tr, ti, gmat = _dft_tables(n_pos)
    fa_spec = _resident((2 * DFT_A, DFT_A))
    if n_pos == DFT_A:
        blk = pl.BlockSpec((1, DFT_A, w), lambda b: (b, 0, 0))
        out = jax.ShapeDtypeStruct((bx, DFT_A, w), BF16)
        return pl.pallas_call(
            _dft_short_kernel,
            grid=(bx,), in_specs=[fa_spec, blk], out_specs=[blk, blk], out_shape=[out, out],
            name="dft_context",
        )(fa, u)
    rows = DFT_C * DFT_C
    blk = pl.BlockSpec((1, DFT_C, DFT_A, w), lambda b: (b, 0, 0, 0))
    out = jax.ShapeDtypeStruct((bx, DFT_C, DFT_A, w), BF16)
    ur, ui = pl.pallas_call(
        _dft_long_kernel,
        grid=(bx,),
        in_specs=[fa_spec, _resident((2 * rows, 2 * rows)), _resident(tr.shape), _resident(ti.shape), blk],
        out_specs=[blk, blk], out_shape=[out, out],
        scratch_shapes=[pltpu.VMEM((DFT_C, DFT_A, w), BF16)] * 2,
        compiler_params=pltpu.CompilerParams(vmem_limit_bytes=VMEM_LIMIT_BYTES),
        name="dft_long",
    )(fa, gmat, tr, ti, u)
    return ur.reshape(bx, n_pos, w), ui.reshape(bx, n_pos, w)


def _out_kernel(a_ref, ur_ref, ui_ref, x_ref, wa_ref, wr_ref, wi_ref, gpost_ref, gate_ref, o_ref):
    mix = jnp.dot(a_ref[0], wa_ref[...], preferred_element_type=F32)
    mix += jnp.dot(ur_ref[0], wr_ref[...], preferred_element_type=F32)
    mix += jnp.dot(ui_ref[0], wi_ref[...], preferred_element_type=F32)
    o_ref[0] = x_ref[0] + gate_ref[0] * _rms(mix, gpost_ref[...])


def _mix_out(attn, ur, ui, x, wa, wr, wi, gpost, gate):
    bx, t, d = x.shape
    tm = min(TOKEN_TILE, t)
    tok = lambda width: pl.BlockSpec((1, tm, width), lambda b, i: (b, i, 0))
    vec = pl.BlockSpec((1, 1, d), lambda b, i: (b, 0, 0))
    wspec = _resident(wa.shape)
    return pl.pallas_call(
        _out_kernel,
        grid=(bx, t // tm),
        in_specs=[tok(ATTN_WIDTH), tok(FOURIER_WIDTH), tok(FOURIER_WIDTH), tok(d),
                  wspec, wspec, wspec, _resident((1, d)), vec],
        out_specs=tok(d),
        out_shape=jax.ShapeDtypeStruct((bx, t, d), F32),
        compiler_params=pltpu.CompilerParams(vmem_limit_bytes=VMEM_LIMIT_BYTES),
        name="mix_out",
    )(attn, ur, ui, x, wa, wr, wi, gpost, gate)


def _ffn_kernel(x_ref, sh_ref, sc_ref, gate_ref, gpre_ref, gpost_ref, wg_ref, wu_ref, wd_ref, o_ref):
    x = x_ref[0]
    h = (_rms(x, gpre_ref[...]) * (1.0 + sc_ref[0]) + sh_ref[0]).astype(BF16)
    hidden = wg_ref.shape[1]
    acc = jnp.zeros(x.shape, F32)
    for lo in range(0, hidden, FFN_CHUNK):
        hi = min(lo + FFN_CHUNK, hidden)
        g = jnp.dot(h, wg_ref[:, lo:hi], preferred_element_type=F32)
        u = jnp.dot(h, wu_ref[:, lo:hi], preferred_element_type=F32)
        a = (g * jax.nn.sigmoid(g) * u).astype(BF16)
        acc += jnp.dot(a, wd_ref[lo:hi, :], preferred_element_type=F32)
    o_ref[0] = x + gate_ref[0] * _rms(acc, gpost_ref[...])


def _ffn(x, sh, sc, gate, gpre, gpost, wg, wu, wd):
    bx, t, d = x.shape
    tm = min(TOKEN_TILE, t)
    tok = pl.BlockSpec((1, tm, d), lambda b, i: (b, i, 0))
    vec = pl.BlockSpec((1, 1, d), lambda b, i: (b, 0, 0))
    return pl.pallas_call(
        _ffn_kernel,
        grid=(bx, t // tm),
        in_specs=[tok, vec, vec, vec, _resident((1, d)), _resident((1, d)),
                  _resident(wg.shape), _resident(wu.shape), _resident(wd.shape)],
        out_specs=tok,
        out_shape=jax.ShapeDtypeStruct((bx, t, d), F32),
        compiler_params=pltpu.CompilerParams(vmem_limit_bytes=VMEM_LIMIT_BYTES),
        name="ffn",
    )(x, sh, sc, gate, gpre, gpost, wg, wu, wd)


def _rope_tables(seq):
    t = jnp.arange(seq)
    row = (t // GRID_W).astype(F32)
    col = (t % GRID_W).astype(F32)
    freqs = ROPE_THETA ** (-jnp.arange(ROPE_PAIRS_PER_AXIS, dtype=F32) / ROPE_PAIRS_PER_AXIS)
    ang = jnp.concatenate([row[:, None] * freqs, col[:, None] * freqs], axis=-1)
    cos, sin = jnp.cos(ang), jnp.sin(ang)
    heads_per_row = LANES // HEAD_DIM
    return (jnp.tile(jnp.concatenate([cos, cos], axis=-1), (1, heads_per_row)),
            jnp.tile(jnp.concatenate([-sin, sin], axis=-1), (1, heads_per_row)))


def _prepare_w_in(w_in):
    d = w_in.shape[0]
    rot_width = ATTN_WIDTH + KV_WIDTH
    pairs = w_in[:, :rot_width].reshape(d, rot_width // HEAD_DIM, HEAD_DIM // 2, 2)
    rot = jnp.swapaxes(pairs, 2, 3).reshape(d, rot_width)
    scale = np.ones((w_in.shape[1],), np.float32)
    scale[:ATTN_WIDTH] = HEAD_DIM ** -0.5
    return (jnp.concatenate([rot, w_in[:, rot_width:]], axis=1) * scale).astype(BF16)


def kernel(x, c, ctx, c_ctx, w_ada, b_ada, norm_pre_mix, norm_post_mix, norm_pre_ffn, norm_post_ffn,
           w_in, w_out, w_fourier, sink, w_gate, w_up, w_down):
    batch, seq, d = x.shape
    depth = w_ada.shape[0]
    assert batch + 1 <= MOD_ROWS

    cc = jnp.zeros((MOD_ROWS, d), F32).at[:batch].set(c).at[batch].set(c_ctx)
    mod = _ada(cc, w_ada, b_ada)
    w_four_r, w_four_i = _fold_fourier_weights(w_fourier, w_out)
    rope = _rope_tables(seq)

    xc = ctx
    for i in range(depth):
        lat = [m.reshape(batch, 1, d) for m in jnp.split(mod[i, :batch], N_MOD, axis=-1)]
        cmod = [jnp.broadcast_to(m.reshape(1, 1, d), (batch, 1, d))
                for m in jnp.split(mod[i, batch], N_MOD, axis=-1)]
        sh1, sc1, g1, sh2, sc2, g2 = lat
        csh1, csc1, cg1, csh2, csc2, cg2 = cmod
        g_pre_mix = norm_pre_mix[i].reshape(1, d)
        g_post_mix = norm_post_mix[i].reshape(1, d)
        g_pre_ffn = norm_pre_ffn[i].reshape(1, d)
        g_post_ffn = norm_post_ffn[i].reshape(1, d)
        w_in_i = _prepare_w_in(w_in[i])
        w_attn_out = w_out[i, :ATTN_WIDTH].astype(BF16)
        wg, wu, wd = w_gate[i].astype(BF16), w_up[i].astype(BF16), w_down[i].astype(BF16)

        def finish(stream, attn, ur, ui, gate1, sh, sc, gate2):
            mid = _mix_out(attn, ur, ui, stream, w_attn_out, w_four_r[i], w_four_i[i], g_post_mix, gate1)
            return _ffn(mid, sh, sc, gate2, g_pre_ffn, g_post_ffn, wg, wu, wd)

        q, k, v, u = _project(x, sh1, sc1, g_pre_mix, w_in_i, rope)
        qc, kc, vc, uc = _project(xc, csh1, csc1, g_pre_mix, w_in_i, None)
        attn = _attention(q, k, v, kc, vc, sink[i], local=True)
        ur, ui = _position_dft(u)
        x = finish(x, attn, ur, ui, g1, sh2, sc2, g2)

        if i < depth - 1:
            attn_c = _attention(qc, None, None, kc, vc, sink[i], local=False)
            urc, uic = _position_dft(uc)
            xc = finish(xc, attn_c, urc, uic, cg1, csh2, csc2, cg2)
    return x
```

```python
import functools
import math

import numpy as np
import jax
import jax.numpy as jnp
from jax import lax
from jax.experimental import pallas as pl
from jax.experimental.pallas import tpu as pltpu

F32 = jnp.float32
BF16 = jnp.bfloat16
HIGHEST = lax.Precision.HIGHEST

GRID_W = 64
HEAD_DIM = 64
N_Q_HEADS = 8
N_KV_HEADS = 2
Q_PER_KV = N_Q_HEADS // N_KV_HEADS
ATTN_WIDTH = N_Q_HEADS * HEAD_DIM
KV_WIDTH = N_KV_HEADS * HEAD_DIM
N_FOURIER_GROUPS = 8
FOURIER_GROUP_DIM = 64
FOURIER_WIDTH = N_FOURIER_GROUPS * FOURIER_GROUP_DIM
WINDOW = 128
ROPE_THETA = 10000.0
ROPE_PAIRS_PER_AXIS = HEAD_DIM // 4
RMS_EPS = 1e-6
N_MOD = 6
NEG_INF = -1e30

LANES = 128
VMEM_LIMIT_BYTES = 52 * 1024 * 1024

DFT_A = 256
DFT_C = 16

MOD_ROWS = 8
TOKEN_TILE = 512
ATTN_Q_TILE = 512
BF16_SUBLANES = 16
VT_HEAD_ROWS = HEAD_DIM + BF16_SUBLANES
VT_ROWS = N_KV_HEADS * VT_HEAD_ROWS
LOG2_E = math.log2(math.e)
FFN_CHUNK = 512


def _resident(shape):
    zeros = (0,) * len(shape)
    return pl.BlockSpec(shape, lambda *_: zeros, pipeline_mode=pl.Buffered(1))


def _rms(xf, g):
    ms = jnp.mean(xf * xf, axis=-1, keepdims=True)
    return xf * lax.rsqrt(ms + RMS_EPS) * g


def _ada_kernel(c_ref, w_ref, b_ref, o_ref):
    c = c_ref[...]
    a = c * jax.nn.sigmoid(c)
    o_ref[0] = jnp.dot(a, w_ref[0], precision=HIGHEST, preferred_element_type=F32) + b_ref[0]


def _ada(cc, w_ada, b_ada):
    depth, d, n = w_ada.shape
    tn = 1536
    return pl.pallas_call(
        _ada_kernel,
        grid=(depth, n // tn),
        in_specs=[
            pl.BlockSpec((MOD_ROWS, d), lambda i, j: (0, 0)),
            pl.BlockSpec((1, d, tn), lambda i, j: (i, 0, j)),
            pl.BlockSpec((1, 1, tn), lambda i, j: (i, 0, j)),
        ],
        out_specs=pl.BlockSpec((1, MOD_ROWS, tn), lambda i, j: (i, 0, j)),
        out_shape=jax.ShapeDtypeStruct((depth, MOD_ROWS, n), F32),
        compiler_params=pltpu.CompilerParams(vmem_limit_bytes=VMEM_LIMIT_BYTES),
        name="ada",
    )(cc, w_ada, b_ada.reshape(depth, 1, n))


def _fold_kernel(c64_ref, s64_ref, wf_ref, wo_ref, wr_ref, wi_ref):
    t = jnp.dot(wf_ref[0, 0], wo_ref[0], precision=HIGHEST, preferred_element_type=F32)
    wr_ref[0] = jnp.dot(c64_ref[...], t, precision=HIGHEST, preferred_element_type=F32).astype(BF16)
    wi_ref[0] = jnp.dot(s64_ref[...], t, precision=HIGHEST, preferred_element_type=F32).astype(BF16)


def _fold_fourier_weights(w_fourier, w_out):
    depth, groups, gd, _ = w_fourier.shape
    d = w_out.shape[-1]
    j = np.arange(gd)
    ang = 2.0 * np.pi * ((j[:, None] * j[None, :]) % gd) / gd
    c64 = jnp.asarray(np.cos(ang), F32)
    s64 = jnp.asarray(np.sin(ang), F32)
    first = ATTN_WIDTH // gd
    out = jax.ShapeDtypeStruct((depth, groups * gd, d), BF16)
    return pl.pallas_call(
        _fold_kernel,
        grid=(depth, groups),
        in_specs=[
            pl.BlockSpec((gd, gd), lambda i, g: (0, 0)),
            pl.BlockSpec((gd, gd), lambda i, g: (0, 0)),
            pl.BlockSpec((1, 1, gd, gd), lambda i, g: (i, g, 0, 0)),
            pl.BlockSpec((1, gd, d), lambda i, g: (i, first + g, 0)),
        ],
        out_specs=[pl.BlockSpec((1, gd, d), lambda i, g: (i, g, 0))] * 2,
        out_shape=[out, out],
        name="fold_fourier",
    )(c64, s64, w_fourier, w_out)


def _proj_kernel(x_ref, sh_ref, sc_ref, g_ref, w_ref, *rest, rope):
    if rope:
        cos_ref, sin_ref, q_ref, k_ref, v_ref, u_ref, u_scratch = rest
    else:
        q_ref, k_ref, v_ref, u_ref = rest
    h = _rms(x_ref[0], g_ref[...]) * (1.0 + sc_ref[0]) + sh_ref[0]
    p = jnp.dot(h.astype(BF16), w_ref[...], preferred_element_type=F32)
    k_lo = ATTN_WIDTH
    v_lo = ATTN_WIDTH + KV_WIDTH
    u_lo = ATTN_WIDTH + 2 * KV_WIDTH
    if rope:
        cos = cos_ref[...]
        sin = sin_ref[...]
        lane = lax.broadcasted_iota(jnp.int32, cos.shape, 1)
        half = HEAD_DIM // 2
        first_half = (lane & (HEAD_DIM - 1)) < half

        def rot(xs):
            partner = jnp.where(first_half, pltpu.roll(xs, LANES - half, 1), pltpu.roll(xs, half, 1))
            return xs * cos + partner * sin

        for j in range(ATTN_WIDTH // LANES):
            q_ref[0, :, j * LANES:(j + 1) * LANES] = rot(p[:, j * LANES:(j + 1) * LANES]).astype(BF16)
        k_ref[0] = rot(p[:, k_lo:v_lo]).astype(BF16)
    else:
        q_ref[0] = p[:, :k_lo].astype(BF16)
        k_ref[0] = p[:, k_lo:v_lo].astype(BF16)
    vt = p[:, v_lo:u_lo].T.astype(BF16)
    ones = jnp.ones((VT_HEAD_ROWS - HEAD_DIM, vt.shape[1]), BF16)
    for h in range(N_KV_HEADS):
        v_ref[0, h * VT_HEAD_ROWS:h * VT_HEAD_ROWS + HEAD_DIM] = vt[h * HEAD_DIM:(h + 1) * HEAD_DIM]
        v_ref[0, h * VT_HEAD_ROWS + HEAD_DIM:(h + 1) * VT_HEAD_ROWS] = ones
    if rope:
        per = u_scratch.shape[1] // DFT_C
        for c in range(u_scratch.shape[0]):
            u_scratch[c] = p[:, u_lo + c * LANES:u_lo + (c + 1) * LANES]
            for n2 in range(DFT_C):
                u_ref[0, n2, :, c * LANES:(c + 1) * LANES] = (
                    u_scratch[c, pl.ds(n2, per, stride=DFT_C), :].astype(BF16))
    else:
        u_ref[0] = p[:, u_lo:].astype(BF16)


def _project(x, sh, sc, g, w_in, rope_tables):
    bx, t, d = x.shape
    tm = min(TOKEN_TILE, t)
    n_in = w_in.shape[1]
    rope = rope_tables is not None
    tok = lambda width: pl.BlockSpec((1, tm, width), lambda b, i: (b, i, 0))
    vec = pl.BlockSpec((1, 1, d), lambda b, i: (b, 0, 0))
    in_specs = [tok(d), vec, vec, _resident((1, d)), _resident((d, n_in))]
    args = [x, sh, sc, g, w_in]
    if rope:
        in_specs += [pl.BlockSpec((tm, LANES), lambda b, i: (i, 0))] * 2
        args += list(rope_tables)
    widths = (ATTN_WIDTH, KV_WIDTH, KV_WIDTH, FOURIER_WIDTH)
    out_specs = [tok(w) for w in widths]
    out_shape = [jax.ShapeDtypeStruct((bx, t, w), BF16) for w in widths]
    out_specs[2] = pl.BlockSpec((1, VT_ROWS, tm), lambda b, i: (b, 0, i))
    out_shape[2] = jax.ShapeDtypeStruct((bx, VT_ROWS, t), BF16)
    scratch = []
    if rope:
        assert t == DFT_A * DFT_C and tm % DFT_C == 0
        out_specs[3] = pl.BlockSpec((1, DFT_C, tm // DFT_C, FOURIER_WIDTH), lambda b, i: (b, 0, i, 0))
        out_shape[3] = jax.ShapeDtypeStruct((bx, DFT_C, DFT_A, FOURIER_WIDTH), BF16)
        scratch = [pltpu.VMEM((FOURIER_WIDTH // LANES, tm, LANES), F32)]
    return pl.pallas_call(
        functools.partial(_proj_kernel, rope=rope),
        grid=(bx, t // tm),
        in_specs=in_specs,
        out_specs=out_specs,
        out_shape=out_shape,
        scratch_shapes=scratch,
        compiler_params=pltpu.CompilerParams(vmem_limit_bytes=VMEM_LIMIT_BYTES),
        name="project_rope" if rope else "project",
    )(*args)


def _attn_kernel(sink_ref, q_ref, *rest, local):
    w = WINDOW
    if local:
        kp_ref, kc_ref, kn_ref, vp_ref, vc_ref, vn_ref, kx_ref, vx_ref, o_ref = rest
        k_all = jnp.concatenate([kp_ref[0], kc_ref[0], kn_ref[0]], axis=0)
        vt_all = jnp.concatenate([vp_ref[0], vc_ref[0], vn_ref[0]], axis=1)
        step = pl.program_id(1)
        last = pl.num_programs(1) - 1
    else:
        kx_ref, vx_ref, o_ref = rest
    kx = kx_ref[0]
    vxt = vx_ref[0]
    n_sub = q_ref.shape[1] // w
    cols = Q_PER_KV * w
    group = lax.shift_right_logical(lax.broadcasted_iota(jnp.int32, (1, cols), 1), int(math.log2(w)))
    if local:
        kj = lax.broadcasted_iota(jnp.int32, (w, cols), 0)
        qi = lax.broadcasted_iota(jnp.int32, (w, cols), 1) & (w - 1)
        far = 1 << 20
        bias_prev = jnp.where(kj >= qi, 0.0, NEG_INF)
        bias_next = jnp.where(kj <= qi, 0.0, NEG_INF)
        bias_prev_first = jnp.where(kj >= qi + jnp.where(step > 0, 0, far), 0.0, NEG_INF)
        bias_next_last = jnp.where(kj <= qi - jnp.where(step < last, 0, far), 0.0, NEG_INF)
    contract_last = (((1,), (1,)), ((), ()))

    def sink_row(h):
        sink = jnp.full((1, cols), sink_ref[Q_PER_KV * h] * LOG2_E, F32)
        for g in range(1, Q_PER_KV):
            sink = jnp.where(group == g, sink_ref[Q_PER_KV * h + g] * LOG2_E, sink)
        return sink

    sinks = [sink_row(h) for h in range(N_KV_HEADS)]

    def scores(h, s):
        hs = slice(h * HEAD_DIM, (h + 1) * HEAD_DIM)
        q = q_ref[0, s * w:(s + 1) * w, :]
        qs = jnp.concatenate([q[:, (Q_PER_KV * h + g) * HEAD_DIM:(Q_PER_KV * h + g + 1) * HEAD_DIM]
                              for g in range(Q_PER_KV)], axis=0)
        pieces = []
        if local:
            sl = lax.dot_general(k_all[s * w:(s + 3) * w, hs], qs, contract_last, preferred_element_type=F32)
            pieces += [sl[:w] + (bias_prev_first if s == 0 else bias_prev), sl[w:2 * w],
                       sl[2 * w:] + (bias_next_last if s == n_sub - 1 else bias_next)]
        pieces.append(lax.dot_general(kx[:, hs], qs, contract_last, preferred_element_type=F32))
        m = sinks[h]
        for x in pieces:
            m = jnp.maximum(m, jnp.max(x, axis=0, keepdims=True))
        return pieces, m

    def finish(h, s, pieces, m):
        vs = slice(h * VT_HEAD_ROWS, (h + 1) * VT_HEAD_ROWS)
        probs = [jnp.exp2(x - m).astype(BF16) for x in pieces]
        ot = jnp.dot(vxt[vs, :], probs[-1], preferred_element_type=F32)
        if local:
            ot = ot + jnp.dot(vt_all[vs, s * w:(s + 3) * w], jnp.concatenate(probs[:3], axis=0),
                              preferred_element_type=F32)
        denom = ot[HEAD_DIM:HEAD_DIM + 1] + jnp.exp2(sinks[h] - m)
        ot = ot[:HEAD_DIM] / denom
        o = jnp.concatenate([ot[:, g * w:(g + 1) * w].T for g in range(Q_PER_KV)], axis=1)
        lo = h * Q_PER_KV * HEAD_DIM
        o_ref[0, s * w:(s + 1) * w, lo:lo + Q_PER_KV * HEAD_DIM] = o.astype(BF16)

    work = [(h, s) for h in range(N_KV_HEADS) for s in range(n_sub)]
    ready = scores(*work[0])
    for i, (h, s) in enumerate(work):
        upcoming = scores(*work[i + 1]) if i + 1 < len(work) else None
        finish(h, s, *ready)
        ready = upcoming


def _attention(q, k, v, kx, vx, sink, local):
    bx, t, _ = q.shape
    cx = kx.shape[1]
    smem = pl.BlockSpec(memory_space=pltpu.SMEM)
    kx_spec = pl.BlockSpec((1, cx, KV_WIDTH), lambda b, n: (b, 0, 0))
    vx_spec = pl.BlockSpec((1, VT_ROWS, cx), lambda b, n: (b, 0, 0))
    tq = min(ATTN_Q_TILE, t)
    nb = t // tq
    q_spec = pl.BlockSpec((1, tq, ATTN_WIDTH), lambda b, n: (b, n, 0))
    if local:
        per = tq // WINDOW
        n_win = t // WINDOW
        before = lambda n: jnp.maximum(n * per - 1, 0)
        after = lambda n: jnp.minimum((n + 1) * per, n_win - 1)
        k_specs = [pl.BlockSpec((1, WINDOW, KV_WIDTH), lambda b, n: (b, before(n), 0)),
                   pl.BlockSpec((1, tq, KV_WIDTH), lambda b, n: (b, n, 0)),
                   pl.BlockSpec((1, WINDOW, KV_WIDTH), lambda b, n: (b, after(n), 0))]
        v_specs = [pl.BlockSpec((1, VT_ROWS, WINDOW), lambda b, n: (b, 0, before(n))),
                   pl.BlockSpec((1, VT_ROWS, tq), lambda b, n: (b, 0, n)),
                   pl.BlockSpec((1, VT_ROWS, WINDOW), lambda b, n: (b, 0, after(n)))]
        in_specs = [smem, q_spec] + k_specs + v_specs + [kx_spec, vx_spec]
        args = (sink, q, k, k, k, v, v, v, kx, vx)
    else:
        in_specs = [smem, q_spec, kx_spec, vx_spec]
        args = (sink, q, kx, vx)
    return pl.pallas_call(
        functools.partial(_attn_kernel, local=local),
        grid=(bx, nb),
        in_specs=in_specs,
        out_specs=pl.BlockSpec((1, tq, ATTN_WIDTH), lambda b, n: (b, n, 0)),
        out_shape=jax.ShapeDtypeStruct((bx, t, ATTN_WIDTH), BF16),
        compiler_params=pltpu.CompilerParams(vmem_limit_bytes=VMEM_LIMIT_BYTES),
        name="attn_window" if local else "attn_context",
    )(*args)


def _dft_short_kernel(fa_ref, u_ref, ur_ref, ui_ref):
    y = jnp.dot(fa_ref[...].astype(BF16), u_ref[0], preferred_element_type=F32)
    ur_ref[0] = y[:DFT_A].astype(BF16)
    ui_ref[0] = y[DFT_A:].astype(BF16)


def _dft_long_kernel(fa_ref, g_ref, tr_ref, ti_ref, u_ref, ur_ref, ui_ref, yr_s, yi_s):
    fa = fa_ref[...].astype(BF16)
    for n2 in range(DFT_C):
        y = jnp.dot(fa, u_ref[0, n2], preferred_element_type=F32)
        tr = tr_ref[n2]
        ti = ti_ref[n2]
        for c in range(y.shape[1] // LANES):
            sl = slice(c * LANES, (c + 1) * LANES)
            yr = y[:DFT_A, sl]
            yi = y[DFT_A:, sl]
            yr_s[n2, :, sl] = (yr * tr - yi * ti).astype(BF16)
            yi_s[n2, :, sl] = (yr * ti + yi * tr).astype(BF16)
    g = g_ref[...].astype(BF16)
    half = DFT_C * DFT_C
    for j in range(DFT_A // DFT_C):
        sl = slice(j * DFT_C, (j + 1) * DFT_C)
        y = jnp.concatenate([yr_s[n2, sl, :] for n2 in range(DFT_C)]
                            + [yi_s[n2, sl, :] for n2 in range(DFT_C)], axis=0)
        r = jnp.dot(g, y, preferred_element_type=F32).astype(BF16)
        for k2 in range(DFT_C):
            ur_ref[0, k2, sl, :] = r[k2 * DFT_C:(k2 + 1) * DFT_C]
            ui_ref[0, k2, sl, :] = r[half + k2 * DFT_C:half + (k2 + 1) * DFT_C]


def _dft_tables(n_pos):
    a = np.arange(DFT_A)
    ang = 2.0 * np.pi * ((a[:, None] * a[None, :]) % DFT_A) / DFT_A
    norm = 1.0 / math.sqrt(n_pos * FOURIER_GROUP_DIM)
    fa = jnp.asarray(np.concatenate([np.cos(ang), -np.sin(ang)], axis=0) * norm, F32)
    if n_pos == DFT_A:
        return fa, None, None, None
    assert n_pos == DFT_A * DFT_C
    n2 = np.arange(DFT_C)
    tw = 2.0 * np.pi * (n2[:, None] * a[None, :]) / n_pos
    tr = jnp.asarray(np.repeat(np.cos(tw)[:, :, None], LANES, axis=2), F32)
    ti = jnp.asarray(np.repeat(-np.sin(tw)[:, :, None], LANES, axis=2), F32)
    ang16 = 2.0 * np.pi * ((n2[:, None] * n2[None, :]) % DFT_C) / DFT_C
    c16, s16 = np.cos(ang16), np.sin(ang16)
    eye = np.eye(DFT_C)
    kron = lambda f: np.einsum("kn,ab->kanb", f, eye).reshape(DFT_C * DFT_C, DFT_C * DFT_C)
    gmat = np.block([[kron(c16), kron(s16)], [kron(-s16), kron(c16)]])
    return fa, tr, ti, jnp.asarray(gmat, F32)


def _position_dft(u):
    bx, w = u.shape[0], u.shape[-1]
    n_pos = math.prod(u.shape[1:-1])
    fa, tr, ti, gmat = _dft_tables(n_pos)
    fa_spec = _resident((2 * DFT_A, DFT_A))
    if n_pos == DFT_A:
        blk = pl.BlockSpec((1, DFT_A, w), lambda b: (b, 0, 0))
        out = jax.ShapeDtypeStruct((bx, DFT_A, w), BF16)
        return pl.pallas_call(
            _dft_short_kernel,
            grid=(bx,), in_specs=[fa_spec, blk], out_specs=[blk, blk], out_shape=[out, out],
            name="dft_context",
        )(fa, u)
    rows = DFT_C * DFT_C
    blk = pl.BlockSpec((1, DFT_C, DFT_A, w), lambda b: (b, 0, 0, 0))
    out = jax.ShapeDtypeStruct((bx, DFT_C, DFT_A, w), BF16)
    ur, ui = pl.pallas_call(
        _dft_long_kernel,
        grid=(bx,),
        in_specs=[fa_spec, _resident((2 * rows, 2 * rows)), _resident(tr.shape), _resident(ti.shape), blk],
        out_specs=[blk, blk], out_shape=[out, out],
        scratch_shapes=[pltpu.VMEM((DFT_C, DFT_A, w), BF16)] * 2,
        compiler_params=pltpu.CompilerParams(vmem_limit_bytes=VMEM_LIMIT_BYTES),
        name="dft_long",
    )(fa, gmat, tr, ti, u)
    return ur.reshape(bx, n_pos, w), ui.reshape(bx, n_pos, w)


def _out_kernel(a_ref, ur_ref, ui_ref, x_ref, wa_ref, wr_ref, wi_ref, gpost_ref, gate_ref, o_ref):
    mix = jnp.dot(a_ref[0], wa_ref[...], preferred_element_type=F32)
    mix += jnp.dot(ur_ref[0], wr_ref[...], preferred_element_type=F32)
    mix += jnp.dot(ui_ref[0], wi_ref[...], preferred_element_type=F32)
    o_ref[0] = x_ref[0] + gate_ref[0] * _rms(mix, gpost_ref[...])


def _mix_out(attn, ur, ui, x, wa, wr, wi, gpost, gate):
    bx, t, d = x.shape
    tm = min(TOKEN_TILE, t)
    tok = lambda width: pl.BlockSpec((1, tm, width), lambda b, i: (b, i, 0))
    vec = pl.BlockSpec((1, 1, d), lambda b, i: (b, 0, 0))
    wspec = _resident(wa.shape)
    return pl.pallas_call(
        _out_kernel,
        grid=(bx, t // tm),
        in_specs=[tok(ATTN_WIDTH), tok(FOURIER_WIDTH), tok(FOURIER_WIDTH), tok(d),
                  wspec, wspec, wspec, _resident((1, d)), vec],
        out_specs=tok(d),
        out_shape=jax.ShapeDtypeStruct((bx, t, d), F32),
        compiler_params=pltpu.CompilerParams(vmem_limit_bytes=VMEM_LIMIT_BYTES),
        name="mix_out",
    )(attn, ur, ui, x, wa, wr, wi, gpost, gate)


def _ffn_kernel(x_ref, sh_ref, sc_ref, gate_ref, gpre_ref, gpost_ref, wg_ref, wu_ref, wd_ref, o_ref):
    x = x_ref[0]
    h = (_rms(x, gpre_ref[...]) * (1.0 + sc_ref[0]) + sh_ref[0]).astype(BF16)
    hidden = wg_ref.shape[1]
    acc = jnp.zeros(x.shape, F32)
    for lo in range(0, hidden, FFN_CHUNK):
        hi = min(lo + FFN_CHUNK, hidden)
        g = jnp.dot(h, wg_ref[:, lo:hi], preferred_element_type=F32)
        u = jnp.dot(h, wu_ref[:, lo:hi], preferred_element_type=F32)
        a = (g * jax.nn.sigmoid(g) * u).astype(BF16)
        acc += jnp.dot(a, wd_ref[lo:hi, :], preferred_element_type=F32)
    o_ref[0] = x + gate_ref[0] * _rms(acc, gpost_ref[...])


def _ffn(x, sh, sc, gate, gpre, gpost, wg, wu, wd):
    bx, t, d = x.shape
    tm = min(TOKEN_TILE, t)
    tok = pl.BlockSpec((1, tm, d), lambda b, i: (b, i, 0))
    vec = pl.BlockSpec((1, 1, d), lambda b, i: (b, 0, 0))
    return pl.pallas_call(
        _ffn_kernel,
        grid=(bx, t // tm),
        in_specs=[tok, vec, vec, vec, _resident((1, d)), _resident((1, d)),
                  _resident(wg.shape), _resident(wu.shape), _resident(wd.shape)],
        out_specs=tok,
        out_shape=jax.ShapeDtypeStruct((bx, t, d), F32),
        compiler_params=pltpu.CompilerParams(vmem_limit_bytes=VMEM_LIMIT_BYTES),
        name="ffn",
    )(x, sh, sc, gate, gpre, gpost, wg, wu, wd)


def _rope_tables(seq):
    t = jnp.arange(seq)
    row = (t // GRID_W).astype(F32)
    col = (t % GRID_W).astype(F32)
    freqs = ROPE_THETA ** (-jnp.arange(ROPE_PAIRS_PER_AXIS, dtype=F32) / ROPE_PAIRS_PER_AXIS)
    ang = jnp.concatenate([row[:, None] * freqs, col[:, None] * freqs], axis=-1)
    cos, sin = jnp.cos(ang), jnp.sin(ang)
    heads_per_row = LANES // HEAD_DIM
    return (jnp.tile(jnp.concatenate([cos, cos], axis=-1), (1, heads_per_row)),
            jnp.tile(jnp.concatenate([-sin, sin], axis=-1), (1, heads_per_row)))


def _prepare_w_in(w_in):
    d = w_in.shape[0]
    rot_width = ATTN_WIDTH + KV_WIDTH
    pairs = w_in[:, :rot_width].reshape(d, rot_width // HEAD_DIM, HEAD_DIM // 2, 2)
    rot = jnp.swapaxes(pairs, 2, 3).reshape(d, rot_width)
    scale = np.ones((w_in.shape[1],), np.float32)
    scale[:ATTN_WIDTH] = HEAD_DIM ** -0.5 * LOG2_E
    return (jnp.concatenate([rot, w_in[:, rot_width:]], axis=1) * scale).astype(BF16)


def kernel(x, c, ctx, c_ctx, w_ada, b_ada, norm_pre_mix, norm_post_mix, norm_pre_ffn, norm_post_ffn,
           w_in, w_out, w_fourier, sink, w_gate, w_up, w_down):
    batch, seq, d = x.shape
    depth = w_ada.shape[0]
    assert batch + 1 <= MOD_ROWS

    cc = jnp.zeros((MOD_ROWS, d), F32).at[:batch].set(c).at[batch].set(c_ctx)
    mod = _ada(cc, w_ada, b_ada)
    w_four_r, w_four_i = _fold_fourier_weights(w_fourier, w_out)
    rope = _rope_tables(seq)

    xc = ctx
    for i in range(depth):
        lat = [m.reshape(batch, 1, d) for m in jnp.split(mod[i, :batch], N_MOD, axis=-1)]
        cmod = [jnp.broadcast_to(m.reshape(1, 1, d), (batch, 1, d))
                for m in jnp.split(mod[i, batch], N_MOD, axis=-1)]
        sh1, sc1, g1, sh2, sc2, g2 = lat
        csh1, csc1, cg1, csh2, csc2, cg2 = cmod
        g_pre_mix = norm_pre_mix[i].reshape(1, d)
        g_post_mix = norm_post_mix[i].reshape(1, d)
        g_pre_ffn = norm_pre_ffn[i].reshape(1, d)
        g_post_ffn = norm_post_ffn[i].reshape(1, d)
        w_in_i = _prepare_w_in(w_in[i])
        w_attn_out = w_out[i, :ATTN_WIDTH].astype(BF16)
        wg, wu, wd = w_gate[i].astype(BF16), w_up[i].astype(BF16), w_down[i].astype(BF16)

        def finish(stream, attn, ur, ui, gate1, sh, sc, gate2):
            mid = _mix_out(attn, ur, ui, stream, w_attn_out, w_four_r[i], w_four_i[i], g_post_mix, gate1)
            return _ffn(mid, sh, sc, gate2, g_pre_ffn, g_post_ffn, wg, wu, wd)

        q, k, v, u = _project(x, sh1, sc1, g_pre_mix, w_in_i, rope)
        qc, kc, vc, uc = _project(xc, csh1, csc1, g_pre_mix, w_in_i, None)
        attn = _attention(q, k, v, kc, vc, sink[i], local=True)
        ur, ui = _position_dft(u)
        x = finish(x, attn, ur, ui, g1, sh2, sc2, g2)

        if i < depth - 1:
            attn_c = _attention(qc, None, None, kc, vc, sink[i], local=False)
            urc, uic = _position_dft(uc)
            xc = finish(xc, attn_c, urc, uic, cg1, csh2, csc2, cg2)
    return x
```

```python
import functools
import math

import numpy as np
import jax
import jax.numpy as jnp
from jax import lax
from jax.experimental import pallas as pl
from jax.experimental.pallas import tpu as pltpu

F32 = jnp.float32
BF16 = jnp.bfloat16
HIGHEST = lax.Precision.HIGHEST

GRID_W = 64
HEAD_DIM = 64
N_Q_HEADS = 8
N_KV_HEADS = 2
Q_PER_KV = N_Q_HEADS // N_KV_HEADS
ATTN_WIDTH = N_Q_HEADS * HEAD_DIM
KV_WIDTH = N_KV_HEADS * HEAD_DIM
N_FOURIER_GROUPS = 8
FOURIER_GROUP_DIM = 64
FOURIER_WIDTH = N_FOURIER_GROUPS * FOURIER_GROUP_DIM
WINDOW = 128
ROPE_THETA = 10000.0
ROPE_PAIRS_PER_AXIS = HEAD_DIM // 4
RMS_EPS = 1e-6
N_MOD = 6
NEG_INF = -1e30
LOG2_E = math.log2(math.e)

LANES = 128
BF16_SUBLANES = 16
VMEM_LIMIT_BYTES = 52 * 1024 * 1024

DFT_A = 256
DFT_C = 16

MOD_ROWS = 8
TOKEN_TILE = 512
FFN_CHUNK = 512
FFN_ROWS = 512
ATTN_Q_TILE = 512
VT_HEAD_ROWS = HEAD_DIM + BF16_SUBLANES
VT_ROWS = N_KV_HEADS * VT_HEAD_ROWS


def _resident(shape):
    zeros = (0,) * len(shape)
    return pl.BlockSpec(shape, lambda *_: zeros, pipeline_mode=pl.Buffered(1))


def _layer_resident(layer, shape, index=None):
    index = (0,) * len(shape) if index is None else index
    return pl.BlockSpec((None,) + tuple(shape), lambda *_: (layer,) + tuple(index), pipeline_mode=pl.Buffered(1))


def _mod_spec(layer, which, d):
    return pl.BlockSpec((None, MOD_ROWS, d), lambda *_: (layer, 0, which))


def _mod_row(ref, ctx_row):
    row = pl.program_id(0) if ctx_row is None else ctx_row
    return ref[pl.ds(row, 1), :]


def _rms(xf, g):
    ms = jnp.mean(xf * xf, axis=-1, keepdims=True)
    return xf * lax.rsqrt(ms + RMS_EPS) * g


def _ada_kernel(c_ref, w_ref, b_ref, o_ref):
    c = c_ref[...]
    a = c * jax.nn.sigmoid(c)
    o_ref[0] = jnp.dot(a, w_ref[0], precision=HIGHEST, preferred_element_type=F32) + b_ref[0]


def _ada(cc, w_ada, b_ada):
    depth, d, n = w_ada.shape
    tn = 1536
    return pl.pallas_call(
        _ada_kernel,
        grid=(depth, n // tn),
        in_specs=[
            pl.BlockSpec((MOD_ROWS, d), lambda i, j: (0, 0)),
            pl.BlockSpec((1, d, tn), lambda i, j: (i, 0, j)),
            pl.BlockSpec((1, 1, tn), lambda i, j: (i, 0, j)),
        ],
        out_specs=pl.BlockSpec((1, MOD_ROWS, tn), lambda i, j: (i, 0, j)),
        out_shape=jax.ShapeDtypeStruct((depth, MOD_ROWS, n), F32),
        compiler_params=pltpu.CompilerParams(vmem_limit_bytes=VMEM_LIMIT_BYTES),
        name="ada",
    )(cc, w_ada, b_ada.reshape(depth, 1, n))


def _fold_kernel(c64_ref, s64_ref, wf_ref, wo_ref, wr_ref, wi_ref):
    gd = c64_ref.shape[0]
    for g in range(wf_ref.shape[0]):
        rows = slice(g * gd, (g + 1) * gd)
        t = jnp.dot(wf_ref[g], wo_ref[rows, :], precision=HIGHEST, preferred_element_type=F32)
        wr_ref[rows, :] = jnp.dot(c64_ref[...], t, precision=HIGHEST, preferred_element_type=F32).astype(BF16)
        wi_ref[rows, :] = jnp.dot(s64_ref[...], t, precision=HIGHEST, preferred_element_type=F32).astype(BF16)


def _fold_fourier_weights(w_fourier, w_out):
    depth, groups, gd, _ = w_fourier.shape
    d = w_out.shape[-1]
    width = groups * gd
    assert w_out.shape[1] == ATTN_WIDTH + width and ATTN_WIDTH == width
    j = np.arange(gd)
    ang = 2.0 * np.pi * ((j[:, None] * j[None, :]) % gd) / gd
    c64 = jnp.asarray(np.cos(ang), F32)
    s64 = jnp.asarray(np.sin(ang), F32)
    out = jax.ShapeDtypeStruct((depth, width, d), BF16)
    return pl.pallas_call(
        _fold_kernel,
        grid=(depth,),
        in_specs=[
            pl.BlockSpec((gd, gd), lambda i: (0, 0)),
            pl.BlockSpec((gd, gd), lambda i: (0, 0)),
            pl.BlockSpec((None, groups, gd, gd), lambda i: (i, 0, 0, 0)),
            pl.BlockSpec((None, width, d), lambda i: (i, 1, 0)),
        ],
        out_specs=[pl.BlockSpec((None, width, d), lambda i: (i, 0, 0))] * 2,
        out_shape=[out, out],
        name="fold_fourier",
    )(c64, s64, w_fourier, w_out)


def _proj_kernel(x_ref, sh_ref, sc_ref, g_ref, w_ref, *rest, rope, ctx_row):
    if rope:
        cos_ref, sin_ref, q_ref, k_ref, v_ref, u_ref, u_scratch = rest
    else:
        q_ref, k_ref, v_ref, u_ref = rest
    h = _rms(x_ref[0], g_ref[...]) * (1.0 + _mod_row(sc_ref, ctx_row)) + _mod_row(sh_ref, ctx_row)
    p = jnp.dot(h.astype(BF16), w_ref[...], preferred_element_type=F32)
    k_lo = ATTN_WIDTH
    v_lo = ATTN_WIDTH + KV_WIDTH
    u_lo = ATTN_WIDTH + 2 * KV_WIDTH
    if rope:
        cos = cos_ref[...]
        sin = sin_ref[...]
        even_lane = (lax.broadcasted_iota(jnp.int32, cos.shape, 1) & 1) == 0

        def rot(xs):
            partner = jnp.where(even_lane, pltpu.roll(xs, LANES - 1, 1), pltpu.roll(xs, 1, 1))
            return xs * cos + partner * sin

        for j in range(ATTN_WIDTH // LANES):
            q_ref[0, :, j * LANES:(j + 1) * LANES] = rot(p[:, j * LANES:(j + 1) * LANES]).astype(BF16)
        k_ref[0] = rot(p[:, k_lo:v_lo]).astype(BF16)
    else:
        q_ref[0] = p[:, :k_lo].astype(BF16)
        k_ref[0] = p[:, k_lo:v_lo].astype(BF16)
    vt = p[:, v_lo:u_lo].T.astype(BF16)
    ones = jnp.ones((VT_HEAD_ROWS - HEAD_DIM, vt.shape[1]), BF16)
    for h_kv in range(N_KV_HEADS):
        v_ref[0, h_kv * VT_HEAD_ROWS:h_kv * VT_HEAD_ROWS + HEAD_DIM] = vt[h_kv * HEAD_DIM:(h_kv + 1) * HEAD_DIM]
        v_ref[0, h_kv * VT_HEAD_ROWS + HEAD_DIM:(h_kv + 1) * VT_HEAD_ROWS] = ones
    if rope:
        per = u_scratch.shape[1] // DFT_C
        for c in range(u_scratch.shape[0]):
            u_scratch[c] = p[:, u_lo + c * LANES:u_lo + (c + 1) * LANES]
            for n2 in range(DFT_C):
                u_ref[0, n2, :, c * LANES:(c + 1) * LANES] = (
                    u_scratch[c, pl.ds(n2, per, stride=DFT_C), :].astype(BF16))
    else:
        u_ref[0] = p[:, u_lo:].astype(BF16)


def _project(layer, x, mod, gains, w_in, rope_tables, ctx_row=None):
    bx, t, d = x.shape
    tm = min(TOKEN_TILE, t)
    n_in = w_in.shape[-1]
    rope = rope_tables is not None
    tok = lambda width: pl.BlockSpec((1, tm, width), lambda b, i: (b, i, 0))
    in_specs = [tok(d), _mod_spec(layer, 0, d), _mod_spec(layer, 1, d),
                _layer_resident(layer, (1, d)), _layer_resident(layer, (d, n_in))]
    args = [x, mod, mod, gains, w_in]
    if rope:
        in_specs += [pl.BlockSpec((tm, LANES), lambda b, i: (i, 0))] * 2
        args += list(rope_tables)
    widths = (ATTN_WIDTH, KV_WIDTH, KV_WIDTH, FOURIER_WIDTH)
    out_specs = [tok(w) for w in widths]
    out_shape = [jax.ShapeDtypeStruct((bx, t, w), BF16) for w in widths]
    out_specs[2] = pl.BlockSpec((1, VT_ROWS, tm), lambda b, i: (b, 0, i))
    out_shape[2] = jax.ShapeDtypeStruct((bx, VT_ROWS, t), BF16)
    scratch = []
    if rope:
        assert t == DFT_A * DFT_C and tm % DFT_C == 0
        out_specs[3] = pl.BlockSpec((1, DFT_C, tm // DFT_C, FOURIER_WIDTH), lambda b, i: (b, 0, i, 0))
        out_shape[3] = jax.ShapeDtypeStruct((bx, DFT_C, DFT_A, FOURIER_WIDTH), BF16)
        scratch = [pltpu.VMEM((FOURIER_WIDTH // LANES, tm, LANES), F32)]
    return pl.pallas_call(
        functools.partial(_proj_kernel, rope=rope, ctx_row=ctx_row),
        grid=(bx, t // tm),
        in_specs=in_specs,
        out_specs=out_specs,
        out_shape=out_shape,
        scratch_shapes=scratch,
        compiler_params=pltpu.CompilerParams(vmem_limit_bytes=VMEM_LIMIT_BYTES),
        name="project_rope" if rope else "project",
    )(*args)


def _attn_kernel(sink_ref, q_ref, *rest, local):
    w = WINDOW
    if local:
        kp_ref, kc_ref, kn_ref, vp_ref, vc_ref, vn_ref, kx_ref, vx_ref, o_ref = rest
        k_all = jnp.concatenate([kp_ref[0], kc_ref[0], kn_ref[0]], axis=0)
        vt_all = jnp.concatenate([vp_ref[0], vc_ref[0], vn_ref[0]], axis=1)
        step = pl.program_id(1)
        last = pl.num_programs(1) - 1
    else:
        kx_ref, vx_ref, o_ref = rest
    kx = kx_ref[0]
    vxt = vx_ref[0]
    n_sub = q_ref.shape[1] // w
    cols = Q_PER_KV * w
    group = lax.shift_right_logical(lax.broadcasted_iota(jnp.int32, (1, cols), 1), int(math.log2(w)))
    if local:
        kj = lax.broadcasted_iota(jnp.int32, (w, cols), 0)
        qi = lax.broadcasted_iota(jnp.int32, (w, cols), 1) & (w - 1)
        far = 1 << 20
        bias_prev = jnp.where(kj >= qi, 0.0, NEG_INF)
        bias_next = jnp.where(kj <= qi, 0.0, NEG_INF)
        bias_prev_first = jnp.where(kj >= qi + jnp.where(step > 0, 0, far), 0.0, NEG_INF)
        bias_next_last = jnp.where(kj <= qi - jnp.where(step < last, 0, far), 0.0, NEG_INF)
    contract_last = (((1,), (1,)), ((), ()))

    def sink_row(h):
        sink = jnp.full((1, cols), sink_ref[Q_PER_KV * h] * LOG2_E, F32)
        for g in range(1, Q_PER_KV):
            sink = jnp.where(group == g, sink_ref[Q_PER_KV * h + g] * LOG2_E, sink)
        return sink

    sinks = [sink_row(h) for h in range(N_KV_HEADS)]

    def scores(h, s):
        hs = slice(h * HEAD_DIM, (h + 1) * HEAD_DIM)
        q = q_ref[0, s * w:(s + 1) * w, :]
        qs = jnp.concatenate([q[:, (Q_PER_KV * h + g) * HEAD_DIM:(Q_PER_KV * h + g + 1) * HEAD_DIM]
                              for g in range(Q_PER_KV)], axis=0)
        pieces = []
        if local:
            sl = lax.dot_general(k_all[s * w:(s + 3) * w, hs], qs, contract_last, preferred_element_type=F32)
            pieces += [sl[:w] + (bias_prev_first if s == 0 else bias_prev), sl[w:2 * w],
                       sl[2 * w:] + (bias_next_last if s == n_sub - 1 else bias_next)]
        pieces.append(lax.dot_general(kx[:, hs], qs, contract_last, preferred_element_type=F32))
        m = sinks[h]
        for x in pieces:
            m = jnp.maximum(m, jnp.max(x, axis=0, keepdims=True))
        return pieces, m

    def finish(h, s, pieces, m):
        vs = slice(h * VT_HEAD_ROWS, (h + 1) * VT_HEAD_ROWS)
        probs = [jnp.exp2(x - m).astype(BF16) for x in pieces]
        ot = jnp.dot(vxt[vs, :], probs[-1], preferred_element_type=F32)
        if local:
            ot = ot + jnp.dot(vt_all[vs, s * w:(s + 3) * w], jnp.concatenate(probs[:3], axis=0),
                              preferred_element_type=F32)
        denom = ot[HEAD_DIM:HEAD_DIM + 1] + jnp.exp2(sinks[h] - m)
        ot = ot[:HEAD_DIM] / denom
        o = jnp.concatenate([ot[:, g * w:(g + 1) * w].T for g in range(Q_PER_KV)], axis=1)
        lo = h * Q_PER_KV * HEAD_DIM
        o_ref[0, s * w:(s + 1) * w, lo:lo + Q_PER_KV * HEAD_DIM] = o.astype(BF16)

    work = [(h, s) for h in range(N_KV_HEADS) for s in range(n_sub)]
    ready = scores(*work[0])
    for i, (h, s) in enumerate(work):
        upcoming = scores(*work[i + 1]) if i + 1 < len(work) else None
        finish(h, s, *ready)
        ready = upcoming


def _attention(q, k, v, kx, vx, sink, local):
    bx, t, _ = q.shape
    cx = kx.shape[1]
    smem = pl.BlockSpec(memory_space=pltpu.SMEM)
    kx_spec = pl.BlockSpec((1, cx, KV_WIDTH), lambda b, n: (b, 0, 0))
    vx_spec = pl.BlockSpec((1, VT_ROWS, cx), lambda b, n: (b, 0, 0))
    tq = min(ATTN_Q_TILE, t)
    nb = t // tq
    q_spec = pl.BlockSpec((1, tq, ATTN_WIDTH), lambda b, n: (b, n, 0))
    if local:
        per = tq // WINDOW
        n_win = t // WINDOW
        before = lambda n: jnp.maximum(n * per - 1, 0)
        after = lambda n: jnp.minimum((n + 1) * per, n_win - 1)
        k_specs = [pl.BlockSpec((1, WINDOW, KV_WIDTH), lambda b, n: (b, before(n), 0)),
                   pl.BlockSpec((1, tq, KV_WIDTH), lambda b, n: (b, n, 0)),
                   pl.BlockSpec((1, WINDOW, KV_WIDTH), lambda b, n: (b, after(n), 0))]
        v_specs = [pl.BlockSpec((1, VT_ROWS, WINDOW), lambda b, n: (b, 0, before(n))),
                   pl.BlockSpec((1, VT_ROWS, tq), lambda b, n: (b, 0, n)),
                   pl.BlockSpec((1, VT_ROWS, WINDOW), lambda b, n: (b, 0, after(n)))]
        in_specs = [smem, q_spec] + k_specs + v_specs + [kx_spec, vx_spec]
        args = (sink, q, k, k, k, v, v, v, kx, vx)
    else:
        in_specs = [smem, q_spec, kx_spec, vx_spec]
        args = (sink, q, kx, vx)
    return pl.pallas_call(
        functools.partial(_attn_kernel, local=local),
        grid=(bx, nb),
        in_specs=in_specs,
        out_specs=pl.BlockSpec((1, tq, ATTN_WIDTH), lambda b, n: (b, n, 0)),
        out_shape=jax.ShapeDtypeStruct((bx, t, ATTN_WIDTH), BF16),
        compiler_params=pltpu.CompilerParams(vmem_limit_bytes=VMEM_LIMIT_BYTES),
        name="attn_window" if local else "attn_context",
    )(*args)


def _dft_short_kernel(fa_ref, u_ref, ur_ref, ui_ref):
    y = jnp.dot(fa_ref[...].astype(BF16), u_ref[0], preferred_element_type=F32)
    ur_ref[0] = y[:DFT_A].astype(BF16)
    ui_ref[0] = y[DFT_A:].astype(BF16)


def _dft_long_kernel(fa_ref, g_ref, tr_ref, ti_ref, u_ref, ur_ref, ui_ref, yr_s, yi_s):
    fa = fa_ref[...].astype(BF16)
    for n2 in range(DFT_C):
        y = jnp.dot(fa, u_ref[0, n2], preferred_element_type=F32)
        tr = tr_ref[n2]
        ti = ti_ref[n2]
        for c in range(y.shape[1] // LANES):
            sl = slice(c * LANES, (c + 1) * LANES)
            yr = y[:DFT_A, sl]
            yi = y[DFT_A:, sl]
            yr_s[n2, :, sl] = (yr * tr - yi * ti).astype(BF16)
            yi_s[n2, :, sl] = (yr * ti + yi * tr).astype(BF16)
    g = g_ref[...].astype(BF16)
    half = DFT_C * DFT_C
    for j in range(DFT_A // DFT_C):
        sl = slice(j * DFT_C, (j + 1) * DFT_C)
        y = jnp.concatenate([yr_s[n2, sl, :] for n2 in range(DFT_C)]
                            + [yi_s[n2, sl, :] for n2 in range(DFT_C)], axis=0)
        r = jnp.dot(g, y, preferred_element_type=F32).astype(BF16)
        for k2 in range(DFT_C):
            ur_ref[0, k2, sl, :] = r[k2 * DFT_C:(k2 + 1) * DFT_C]
            ui_ref[0, k2, sl, :] = r[half + k2 * DFT_C:half + (k2 + 1) * DFT_C]


def _dft_tables(n_pos):
    a = np.arange(DFT_A)
    ang = 2.0 * np.pi * ((a[:, None] * a[None, :]) % DFT_A) / DFT_A
    norm = 1.0 / math.sqrt(n_pos * FOURIER_GROUP_DIM)
    fa = jnp.asarray(np.concatenate([np.cos(ang), -np.sin(ang)], axis=0) * norm, F32)
    if n_pos == DFT_A:
        return fa, None, None, None
    assert n_pos == DFT_A * DFT_C
    n2 = np.arange(DFT_C)
    tw = 2.0 * np.pi * (n2[:, None] * a[None, :]) / n_pos
    tr = jnp.asarray(np.repeat(np.cos(tw)[:, :, None], LANES, axis=2), F32)
    ti = jnp.asarray(np.repeat(-np.sin(tw)[:, :, None], LANES, axis=2), F32)
    ang16 = 2.0 * np.pi * ((n2[:, None] * n2[None, :]) % DFT_C) / DFT_C
    c16, s16 = np.cos(ang16), np.sin(ang16)
    eye = np.eye(DFT_C)
    kron = lambda f: np.einsum("kn,ab->kanb", f, eye).reshape(DFT_C * DFT_C, DFT_C * DFT_C)
    gmat = np.block([[kron(c16), kron(s16)], [kron(-s16), kron(c16)]])
    return fa, tr, ti, jnp.asarray(gmat, F32)


def _position_dft(u):
    bx, w = u.shape[0], u.shape[-1]
    n_pos = math.prod(u.shape[1:-1])
    fa, tr, ti, gmat = _dft_tables(n_pos)
    fa_spec = _resident((2 * DFT_A, DFT_A))
    if n_pos == DFT_A:
        blk = pl.BlockSpec((1, DFT_A, w), lambda b: (b, 0, 0))
        out = jax.ShapeDtypeStruct((bx, DFT_A, w), BF16)
        return pl.pallas_call(
            _dft_short_kernel,
            grid=(bx,), in_specs=[fa_spec, blk], out_specs=[blk, blk], out_shape=[out, out],
            name="dft_context",
        )(fa, u)
    rows = DFT_C * DFT_C
    blk = pl.BlockSpec((1, DFT_C, DFT_A, w), lambda b: (b, 0, 0, 0))
    out = jax.ShapeDtypeStruct((bx, DFT_C, DFT_A, w), BF16)
    ur, ui = pl.pallas_call(
        _dft_long_kernel,
        grid=(bx,),
        in_specs=[fa_spec, _resident((2 * rows, 2 * rows)), _resident(tr.shape), _resident(ti.shape), blk],
        out_specs=[blk, blk], out_shape=[out, out],
        scratch_shapes=[pltpu.VMEM((DFT_C, DFT_A, w), BF16)] * 2,
        compiler_params=pltpu.CompilerParams(vmem_limit_bytes=VMEM_LIMIT_BYTES),
        name="dft_long",
    )(fa, gmat, tr, ti, u)
    return ur.reshape(bx, n_pos, w), ui.reshape(bx, n_pos, w)


def _out_kernel(a_ref, ur_ref, ui_ref, x_ref, wa_ref, wr_ref, wi_ref, gpost_ref, gate_ref, o_ref, *, ctx_row):
    mix = jnp.dot(a_ref[0], wa_ref[...], preferred_element_type=F32)
    mix += jnp.dot(ur_ref[0], wr_ref[...], preferred_element_type=F32)
    mix += jnp.dot(ui_ref[0], wi_ref[...], preferred_element_type=F32)
    o_ref[0] = x_ref[0] + _mod_row(gate_ref, ctx_row) * _rms(mix, gpost_ref[...])


def _mix_out(layer, attn, ur, ui, x, w_out, w_four_r, w_four_i, gains, mod, ctx_row=None):
    bx, t, d = x.shape
    tm = min(TOKEN_TILE, t)
    tok = lambda width: pl.BlockSpec((1, tm, width), lambda b, i: (b, i, 0))
    wspec = _layer_resident(layer, (FOURIER_WIDTH, d))
    return pl.pallas_call(
        functools.partial(_out_kernel, ctx_row=ctx_row),
        grid=(bx, t // tm),
        in_specs=[tok(ATTN_WIDTH), tok(FOURIER_WIDTH), tok(FOURIER_WIDTH), tok(d),
                  _layer_resident(layer, (ATTN_WIDTH, d)), wspec, wspec,
                  _layer_resident(layer, (1, d)), _mod_spec(layer, 2, d)],
        out_specs=tok(d),
        out_shape=jax.ShapeDtypeStruct((bx, t, d), F32),
        compiler_params=pltpu.CompilerParams(vmem_limit_bytes=VMEM_LIMIT_BYTES),
        name="mix_out",
    )(attn, ur, ui, x, w_out, w_four_r, w_four_i, gains, mod)


def _ffn_kernel(x_ref, sh_ref, sc_ref, gate_ref, gpre_ref, gpost_ref, wg_ref, wu_ref, wd_ref, o_ref, *, ctx_row):
    tm = x_ref.shape[1]
    hidden = wg_ref.shape[1]
    shift, scale, gate = (_mod_row(r, ctx_row) for r in (sh_ref, sc_ref, gate_ref))

    def mlp(rows):
        x = x_ref[0, rows, :]
        h = (_rms(x, gpre_ref[...]) * (1.0 + scale) + shift).astype(BF16)
        acc = jnp.zeros(x.shape, F32)
        for lo in range(0, hidden, FFN_CHUNK):
            hi = min(lo + FFN_CHUNK, hidden)
            g = jnp.dot(h, wg_ref[:, lo:hi], preferred_element_type=F32)
            u = jnp.dot(h, wu_ref[:, lo:hi], preferred_element_type=F32)
            a = (g * jax.nn.sigmoid(g) * u).astype(BF16)
            acc += jnp.dot(a, wd_ref[lo:hi, :], preferred_element_type=F32)
        return acc

    def write(rows, acc):
        o_ref[0, rows, :] = x_ref[0, rows, :] + gate * _rms(acc, gpost_ref[...])

    per = min(FFN_ROWS, tm)
    slices = [slice(r, r + per) for r in range(0, tm, per)]
    pending = None
    for rows in slices:
        acc = mlp(rows)
        if pending is not None:
            write(*pending)
        pending = (rows, acc)
    write(*pending)


def _ffn(layer, x, mod, gains_pre, gains_post, wg, wu, wd, ctx_row=None):
    bx, t, d = x.shape
    tm = min(TOKEN_TILE, t)
    assert tm % min(FFN_ROWS, tm) == 0
    hidden = wg.shape[-1]
    tok = pl.BlockSpec((1, tm, d), lambda b, i: (b, i, 0))
    return pl.pallas_call(
        functools.partial(_ffn_kernel, ctx_row=ctx_row),
        grid=(bx, t // tm),
        in_specs=[tok, _mod_spec(layer, 3, d), _mod_spec(layer, 4, d), _mod_spec(layer, 5, d),
                  _layer_resident(layer, (1, d)), _layer_resident(layer, (1, d)),
                  _layer_resident(layer, (d, hidden)), _layer_resident(layer, (d, hidden)),
                  _layer_resident(layer, (hidden, d))],
        out_specs=tok,
        out_shape=jax.ShapeDtypeStruct((bx, t, d), F32),
        compiler_params=pltpu.CompilerParams(vmem_limit_bytes=VMEM_LIMIT_BYTES),
        name="ffn",
    )(x, mod, mod, mod, gains_pre, gains_post, wg, wu, wd)


def _rope_tables(seq):
    t = np.arange(seq)
    freqs = ROPE_THETA ** (-np.arange(ROPE_PAIRS_PER_AXIS, dtype=np.float64) / ROPE_PAIRS_PER_AXIS)
    ang = np.concatenate([(t // GRID_W)[:, None] * freqs, (t % GRID_W)[:, None] * freqs], axis=-1)
    cos = np.repeat(np.cos(ang), 2, axis=-1)
    sin = np.repeat(np.sin(ang), 2, axis=-1) * np.tile([-1.0, 1.0], HEAD_DIM // 2)
    heads_per_row = LANES // HEAD_DIM
    return (jnp.asarray(np.tile(cos, (1, heads_per_row)), F32), jnp.asarray(np.tile(sin, (1, heads_per_row)), F32))


def kernel(x, c, ctx, c_ctx, w_ada, b_ada, norm_pre_mix, norm_post_mix, norm_pre_ffn, norm_post_ffn,
           w_in, w_out, w_fourier, sink, w_gate, w_up, w_down):
    batch, seq, d = x.shape
    depth = w_ada.shape[0]
    assert batch + 1 <= MOD_ROWS

    cc = jnp.zeros((MOD_ROWS, d), F32).at[:batch].set(c).at[batch].set(c_ctx)
    mod = _ada(cc, w_ada, b_ada)
    w_four_r, w_four_i = _fold_fourier_weights(w_fourier, w_out)
    rope = _rope_tables(seq)

    q_scale = np.ones((w_in.shape[-1],), np.float32)
    q_scale[:ATTN_WIDTH] = HEAD_DIM ** -0.5 * LOG2_E
    w_in_b = (w_in * q_scale).astype(BF16)
    w_out_b = w_out.astype(BF16)
    wg, wu, wd = w_gate.astype(BF16), w_up.astype(BF16), w_down.astype(BF16)
    gains = [g.reshape(depth, 1, d) for g in (norm_pre_mix, norm_post_mix, norm_pre_ffn, norm_post_ffn)]
    g_pre_mix, g_post_mix, g_pre_ffn, g_post_ffn = gains

    xc = ctx
    for i in range(depth):
        def finish(stream, attn, ur, ui, ctx_row):
            mid = _mix_out(i, attn, ur, ui, stream, w_out_b, w_four_r, w_four_i, g_post_mix, mod, ctx_row)
            return _ffn(i, mid, mod, g_pre_ffn, g_post_ffn, wg, wu, wd, ctx_row)

        q, k, v, u = _project(i, x, mod, g_pre_mix, w_in_b, rope)
        qc, kc, vc, uc = _project(i, xc, mod, g_pre_mix, w_in_b, None, ctx_row=batch)
        attn = _attention(q, k, v, kc, vc, sink[i], local=True)
        ur, ui = _position_dft(u)
        x = finish(x, attn, ur, ui, None)

        if i < depth - 1:
            attn_c = _attention(qc, None, None, kc, vc, sink[i], local=False)
            urc, uic = _position_dft(uc)
            xc = finish(xc, attn_c, urc, uic, batch)
    return x
```

```python
import functools
import math

import numpy as np
import jax
import jax.numpy as jnp
from jax import lax
from jax.experimental import pallas as pl
from jax.experimental.pallas import tpu as pltpu

F32 = jnp.float32
BF16 = jnp.bfloat16

GRID_W = 64
HEAD_DIM = 64
N_Q_HEADS = 8
N_KV_HEADS = 2
Q_PER_KV = N_Q_HEADS // N_KV_HEADS
ATTN_WIDTH = N_Q_HEADS * HEAD_DIM
KV_WIDTH = N_KV_HEADS * HEAD_DIM
N_FOURIER_GROUPS = 8
FOURIER_GROUP_DIM = 64
FOURIER_WIDTH = N_FOURIER_GROUPS * FOURIER_GROUP_DIM
WINDOW = 128
ROPE_THETA = 10000.0
ROPE_PAIRS_PER_AXIS = HEAD_DIM // 4
RMS_EPS = 1e-6
N_MOD = 6
NEG_INF = -1e30
LOG2_E = math.log2(math.e)

LANES = 128
BF16_SUBLANES = 16
VMEM_LIMIT_BYTES = 52 * 1024 * 1024

DFT_A = 256
DFT_C = 16

MOD_ROWS = 8
TOKEN_TILE = 512
FFN_CHUNK = 512
ATTN_Q_TILE = 1024
VT_HEAD_ROWS = HEAD_DIM + BF16_SUBLANES
VT_ROWS = N_KV_HEADS * VT_HEAD_ROWS


def _resident(shape):
    zeros = (0,) * len(shape)
    return pl.BlockSpec(shape, lambda *_: zeros, pipeline_mode=pl.Buffered(1))


def _layer_resident(layer, shape, index=None):
    index = (0,) * len(shape) if index is None else index
    return pl.BlockSpec((None,) + tuple(shape), lambda *_: (layer,) + tuple(index), pipeline_mode=pl.Buffered(1))


def _mod_spec(layer, which, d):
    return pl.BlockSpec((None, MOD_ROWS, d), lambda *_: (layer, 0, which))


def _mod_row(ref, ctx_row):
    row = pl.program_id(0) if ctx_row is None else ctx_row
    return ref[pl.ds(row, 1), :]


def _rms(xf, g):
    ms = jnp.mean(xf * xf, axis=-1, keepdims=True)
    return xf * lax.rsqrt(ms + RMS_EPS) * g


def _split_bf16(x):
    hi = x.astype(BF16)
    return hi, (x - hi.astype(F32)).astype(BF16)


def _dot3(a, b):
    (a_hi, a_lo), (b_hi, b_lo) = a, b
    dot = functools.partial(jnp.dot, preferred_element_type=F32)
    return dot(a_hi, b_hi) + dot(a_lo, b_hi) + dot(a_hi, b_lo)


def _ada_kernel(c_ref, w_ref, b_ref, o_ref):
    c = c_ref[...]
    a = c * jax.nn.sigmoid(c)
    a_hi, a_lo = _split_bf16(a)
    w_hi, w_lo = _split_bf16(w_ref[0])
    rows = a.shape[0]
    r = jnp.dot(jnp.concatenate([a_hi, a_lo], axis=0), w_hi, preferred_element_type=F32)
    o_ref[0] = r[:rows] + r[rows:] + jnp.dot(a_hi, w_lo, preferred_element_type=F32) + b_ref[0]


def _ada(cc, w_ada, b_ada):
    depth, d, n = w_ada.shape
    tn = 1536
    return pl.pallas_call(
        _ada_kernel,
        grid=(depth, n // tn),
        in_specs=[
            pl.BlockSpec((MOD_ROWS, d), lambda i, j: (0, 0)),
            pl.BlockSpec((1, d, tn), lambda i, j: (i, 0, j)),
            pl.BlockSpec((1, 1, tn), lambda i, j: (i, 0, j)),
        ],
        out_specs=pl.BlockSpec((1, MOD_ROWS, tn), lambda i, j: (i, 0, j)),
        out_shape=jax.ShapeDtypeStruct((depth, MOD_ROWS, n), F32),
        compiler_params=pltpu.CompilerParams(vmem_limit_bytes=VMEM_LIMIT_BYTES),
        name="ada",
    )(cc, w_ada, b_ada.reshape(depth, 1, n))


def _fold_kernel(c64_ref, s64_ref, wf_ref, wo_ref, wr_ref, wi_ref):
    gd = c64_ref.shape[0]
    for g in range(wf_ref.shape[0]):
        rows = slice(g * gd, (g + 1) * gd)
        t = _split_bf16(_dot3(_split_bf16(wf_ref[g]), _split_bf16(wo_ref[rows, :])))
        wr_ref[rows, :] = _dot3(_split_bf16(c64_ref[...]), t).astype(BF16)
        wi_ref[rows, :] = _dot3(_split_bf16(s64_ref[...]), t).astype(BF16)


def _fold_fourier_weights(w_fourier, w_out):
    depth, groups, gd, _ = w_fourier.shape
    d = w_out.shape[-1]
    width = groups * gd
    assert w_out.shape[1] == ATTN_WIDTH + width and ATTN_WIDTH == width
    j = np.arange(gd)
    ang = 2.0 * np.pi * ((j[:, None] * j[None, :]) % gd) / gd
    c64 = jnp.asarray(np.cos(ang), F32)
    s64 = jnp.asarray(np.sin(ang), F32)
    out = jax.ShapeDtypeStruct((depth, width, d), BF16)
    return pl.pallas_call(
        _fold_kernel,
        grid=(depth,),
        in_specs=[
            pl.BlockSpec((gd, gd), lambda i: (0, 0)),
            pl.BlockSpec((gd, gd), lambda i: (0, 0)),
            pl.BlockSpec((None, groups, gd, gd), lambda i: (i, 0, 0, 0)),
            pl.BlockSpec((None, width, d), lambda i: (i, 1, 0)),
        ],
        out_specs=[pl.BlockSpec((None, width, d), lambda i: (i, 0, 0))] * 2,
        out_shape=[out, out],
        name="fold_fourier",
    )(c64, s64, w_fourier, w_out)


def _proj_kernel(x_ref, sh_ref, sc_ref, g_ref, w_ref, *rest, rope, ctx_row):
    if rope:
        cos_ref, sin_ref, q_ref, k_ref, v_ref, u_ref, u_scratch = rest
    else:
        q_ref, k_ref, v_ref, u_ref = rest
    h = _rms(x_ref[0], g_ref[...]) * (1.0 + _mod_row(sc_ref, ctx_row)) + _mod_row(sh_ref, ctx_row)
    p = jnp.dot(h.astype(BF16), w_ref[...], preferred_element_type=F32)
    k_lo = ATTN_WIDTH
    v_lo = ATTN_WIDTH + KV_WIDTH
    u_lo = ATTN_WIDTH + 2 * KV_WIDTH
    if rope:
        cos = cos_ref[...]
        sin = sin_ref[...]
        even_lane = (lax.broadcasted_iota(jnp.int32, cos.shape, 1) & 1) == 0

        def rot(xs):
            partner = jnp.where(even_lane, pltpu.roll(xs, LANES - 1, 1), pltpu.roll(xs, 1, 1))
            return xs * cos + partner * sin

        for j in range(ATTN_WIDTH // LANES):
            q_ref[0, :, j * LANES:(j + 1) * LANES] = rot(p[:, j * LANES:(j + 1) * LANES]).astype(BF16)
        k_ref[0] = rot(p[:, k_lo:v_lo]).astype(BF16)
    else:
        q_ref[0] = p[:, :k_lo].astype(BF16)
        k_ref[0] = p[:, k_lo:v_lo].astype(BF16)
    vt = p[:, v_lo:u_lo].T.astype(BF16)
    ones = jnp.ones((VT_HEAD_ROWS - HEAD_DIM, vt.shape[1]), BF16)
    for h_kv in range(N_KV_HEADS):
        v_ref[0, h_kv * VT_HEAD_ROWS:h_kv * VT_HEAD_ROWS + HEAD_DIM] = vt[h_kv * HEAD_DIM:(h_kv + 1) * HEAD_DIM]
        v_ref[0, h_kv * VT_HEAD_ROWS + HEAD_DIM:(h_kv + 1) * VT_HEAD_ROWS] = ones
    if rope:
        per = u_scratch.shape[1] // DFT_C
        for c in range(u_scratch.shape[0]):
            u_scratch[c] = p[:, u_lo + c * LANES:u_lo + (c + 1) * LANES]
            for n2 in range(DFT_C):
                u_ref[0, n2, :, c * LANES:(c + 1) * LANES] = (
                    u_scratch[c, pl.ds(n2, per, stride=DFT_C), :].astype(BF16))
    else:
        u_ref[0] = p[:, u_lo:].astype(BF16)


def _project(layer, x, mod, gains, w_in, rope_tables, ctx_row=None):
    bx, t, d = x.shape
    tm = min(TOKEN_TILE, t)
    n_in = w_in.shape[-1]
    rope = rope_tables is not None
    tok = lambda width: pl.BlockSpec((1, tm, width), lambda b, i: (b, i, 0))
    in_specs = [tok(d), _mod_spec(layer, 0, d), _mod_spec(layer, 1, d),
                _layer_resident(layer, (1, d)), _layer_resident(layer, (d, n_in))]
    args = [x, mod, mod, gains, w_in]
    if rope:
        in_specs += [pl.BlockSpec((tm, LANES), lambda b, i: (i, 0))] * 2
        args += list(rope_tables)
    widths = (ATTN_WIDTH, KV_WIDTH, KV_WIDTH, FOURIER_WIDTH)
    out_specs = [tok(w) for w in widths]
    out_shape = [jax.ShapeDtypeStruct((bx, t, w), BF16) for w in widths]
    out_specs[2] = pl.BlockSpec((1, VT_ROWS, tm), lambda b, i: (b, 0, i))
    out_shape[2] = jax.ShapeDtypeStruct((bx, VT_ROWS, t), BF16)
    scratch = []
    if rope:
        assert t == DFT_A * DFT_C and tm % DFT_C == 0
        out_specs[3] = pl.BlockSpec((1, DFT_C, tm // DFT_C, FOURIER_WIDTH), lambda b, i: (b, 0, i, 0))
        out_shape[3] = jax.ShapeDtypeStruct((bx, DFT_C, DFT_A, FOURIER_WIDTH), BF16)
        scratch = [pltpu.VMEM((FOURIER_WIDTH // LANES, tm, LANES), F32)]
    return pl.pallas_call(
        functools.partial(_proj_kernel, rope=rope, ctx_row=ctx_row),
        grid=(bx, t // tm),
        in_specs=in_specs,
        out_specs=out_specs,
        out_shape=out_shape,
        scratch_shapes=scratch,
        compiler_params=pltpu.CompilerParams(vmem_limit_bytes=VMEM_LIMIT_BYTES),
        name="project_rope" if rope else "project",
    )(*args)


def _attn_kernel(sink_ref, q_ref, *rest, local):
    w = WINDOW
    if local:
        kp_ref, kc_ref, kn_ref, vp_ref, vc_ref, vn_ref, kx_ref, vx_ref, o_ref = rest
        k_all = jnp.concatenate([kp_ref[0], kc_ref[0], kn_ref[0]], axis=0)
        vt_all = jnp.concatenate([vp_ref[0], vc_ref[0], vn_ref[0]], axis=1)
        step = pl.program_id(1)
        last = pl.num_programs(1) - 1
    else:
        kx_ref, vx_ref, o_ref = rest
    kx = kx_ref[0]
    vxt = vx_ref[0]
    n_sub = q_ref.shape[1] // w
    cols = Q_PER_KV * w
    group = lax.shift_right_logical(lax.broadcasted_iota(jnp.int32, (1, cols), 1), int(math.log2(w)))
    if local:
        kj = lax.broadcasted_iota(jnp.int32, (w, cols), 0)
        qi = lax.broadcasted_iota(jnp.int32, (w, cols), 1) & (w - 1)
        far = 1 << 20
        bias_prev = jnp.where(kj >= qi, 0.0, NEG_INF)
        bias_next = jnp.where(kj <= qi, 0.0, NEG_INF)
        bias_prev_first = jnp.where(kj >= qi + jnp.where(step > 0, 0, far), 0.0, NEG_INF)
        bias_next_last = jnp.where(kj <= qi - jnp.where(step < last, 0, far), 0.0, NEG_INF)
    contract_last = (((1,), (1,)), ((), ()))

    def sink_row(h):
        sink = jnp.full((1, cols), sink_ref[Q_PER_KV * h] * LOG2_E, F32)
        for g in range(1, Q_PER_KV):
            sink = jnp.where(group == g, sink_ref[Q_PER_KV * h + g] * LOG2_E, sink)
        return sink

    sinks = [sink_row(h) for h in range(N_KV_HEADS)]

    def scores(h, s):
        hs = slice(h * HEAD_DIM, (h + 1) * HEAD_DIM)
        q = q_ref[0, s * w:(s + 1) * w, :]
        qs = jnp.concatenate([q[:, (Q_PER_KV * h + g) * HEAD_DIM:(Q_PER_KV * h + g + 1) * HEAD_DIM]
                              for g in range(Q_PER_KV)], axis=0)
        pieces = []
        if local:
            sl = lax.dot_general(k_all[s * w:(s + 3) * w, hs], qs, contract_last, preferred_element_type=F32)
            pieces += [sl[:w] + (bias_prev_first if s == 0 else bias_prev), sl[w:2 * w],
                       sl[2 * w:] + (bias_next_last if s == n_sub - 1 else bias_next)]
        pieces.append(lax.dot_general(kx[:, hs], qs, contract_last, preferred_element_type=F32))
        m = sinks[h]
        for x in pieces:
            m = jnp.maximum(m, jnp.max(x, axis=0, keepdims=True))
        return pieces, m

    def finish(h, s, pieces, m):
        vs = slice(h * VT_HEAD_ROWS, (h + 1) * VT_HEAD_ROWS)
        probs = [jnp.exp2(x - m).astype(BF16) for x in pieces]
        ot = jnp.dot(vxt[vs, :], probs[-1], preferred_element_type=F32)
        if local:
            ot = ot + jnp.dot(vt_all[vs, s * w:(s + 3) * w], jnp.concatenate(probs[:3], axis=0),
                              preferred_element_type=F32)
        denom = ot[HEAD_DIM:HEAD_DIM + 1] + jnp.exp2(sinks[h] - m)
        ot = ot[:HEAD_DIM] / denom
        o = jnp.concatenate([ot[:, g * w:(g + 1) * w].T for g in range(Q_PER_KV)], axis=1)
        lo = h * Q_PER_KV * HEAD_DIM
        o_ref[0, s * w:(s + 1) * w, lo:lo + Q_PER_KV * HEAD_DIM] = o.astype(BF16)

    work = [(h, s) for h in range(N_KV_HEADS) for s in range(n_sub)]
    ready = scores(*work[0])
    for i, (h, s) in enumerate(work):
        upcoming = scores(*work[i + 1]) if i + 1 < len(work) else None
        finish(h, s, *ready)
        ready = upcoming


def _attention(q, k, v, kx, vx, sink, local):
    bx, t, _ = q.shape
    cx = kx.shape[1]
    smem = pl.BlockSpec(memory_space=pltpu.SMEM)
    kx_spec = pl.BlockSpec((1, cx, KV_WIDTH), lambda b, n: (b, 0, 0))
    vx_spec = pl.BlockSpec((1, VT_ROWS, cx), lambda b, n: (b, 0, 0))
    tq = min(ATTN_Q_TILE, t)
    nb = t // tq
    q_spec = pl.BlockSpec((1, tq, ATTN_WIDTH), lambda b, n: (b, n, 0))
    if local:
        per = tq // WINDOW
        n_win = t // WINDOW
        before = lambda n: jnp.maximum(n * per - 1, 0)
        after = lambda n: jnp.minimum((n + 1) * per, n_win - 1)
        k_specs = [pl.BlockSpec((1, WINDOW, KV_WIDTH), lambda b, n: (b, before(n), 0)),
                   pl.BlockSpec((1, tq, KV_WIDTH), lambda b, n: (b, n, 0)),
                   pl.BlockSpec((1, WINDOW, KV_WIDTH), lambda b, n: (b, after(n), 0))]
        v_specs = [pl.BlockSpec((1, VT_ROWS, WINDOW), lambda b, n: (b, 0, before(n))),
                   pl.BlockSpec((1, VT_ROWS, tq), lambda b, n: (b, 0, n)),
                   pl.BlockSpec((1, VT_ROWS, WINDOW), lambda b, n: (b, 0, after(n)))]
        in_specs = [smem, q_spec] + k_specs + v_specs + [kx_spec, vx_spec]
        args = (sink, q, k, k, k, v, v, v, kx, vx)
    else:
        in_specs = [smem, q_spec, kx_spec, vx_spec]
        args = (sink, q, kx, vx)
    return pl.pallas_call(
        functools.partial(_attn_kernel, local=local),
        grid=(bx, nb),
        in_specs=in_specs,
        out_specs=pl.BlockSpec((1, tq, ATTN_WIDTH), lambda b, n: (b, n, 0)),
        out_shape=jax.ShapeDtypeStruct((bx, t, ATTN_WIDTH), BF16),
        compiler_params=pltpu.CompilerParams(vmem_limit_bytes=VMEM_LIMIT_BYTES),
        name="attn_window" if local else "attn_context",
    )(*args)


def _dft_short_kernel(fa_ref, u_ref, ur_ref, ui_ref):
    y = jnp.dot(fa_ref[...].astype(BF16), u_ref[0], preferred_element_type=F32)
    ur_ref[0] = y[:DFT_A].astype(BF16)
    ui_ref[0] = y[DFT_A:].astype(BF16)


def _dft_long_kernel(fa_ref, g_ref, tr_ref, ti_ref, u_ref, ur_ref, ui_ref, yr_s, yi_s):
    fa = fa_ref[...].astype(BF16)
    for n2 in range(DFT_C):
        y = jnp.dot(fa, u_ref[0, n2], preferred_element_type=F32)
        tr = tr_ref[n2]
        ti = ti_ref[n2]
        for c in range(y.shape[1] // LANES):
            sl = slice(c * LANES, (c + 1) * LANES)
            yr = y[:DFT_A, sl]
            yi = y[DFT_A:, sl]
            yr_s[n2, :, sl] = (yr * tr - yi * ti).astype(BF16)
            yi_s[n2, :, sl] = (yr * ti + yi * tr).astype(BF16)
    g = g_ref[...].astype(BF16)
    half = DFT_C * DFT_C
    for j in range(DFT_A // DFT_C):
        sl = slice(j * DFT_C, (j + 1) * DFT_C)
        y = jnp.concatenate([yr_s[n2, sl, :] for n2 in range(DFT_C)]
                            + [yi_s[n2, sl, :] for n2 in range(DFT_C)], axis=0)
        r = jnp.dot(g, y, preferred_element_type=F32).astype(BF16)
        for k2 in range(DFT_C):
            ur_ref[0, k2, sl, :] = r[k2 * DFT_C:(k2 + 1) * DFT_C]
            ui_ref[0, k2, sl, :] = r[half + k2 * DFT_C:half + (k2 + 1) * DFT_C]


def _dft_tables(n_pos):
    a = np.arange(DFT_A)
    ang = 2.0 * np.pi * ((a[:, None] * a[None, :]) % DFT_A) / DFT_A
    norm = 1.0 / math.sqrt(n_pos * FOURIER_GROUP_DIM)
    fa = jnp.asarray(np.concatenate([np.cos(ang), -np.sin(ang)], axis=0) * norm, F32)
    if n_pos == DFT_A:
        return fa, None, None, None
    assert n_pos == DFT_A * DFT_C
    n2 = np.arange(DFT_C)
    tw = 2.0 * np.pi * (n2[:, None] * a[None, :]) / n_pos
    tr = jnp.asarray(np.repeat(np.cos(tw)[:, :, None], LANES, axis=2), F32)
    ti = jnp.asarray(np.repeat(-np.sin(tw)[:, :, None], LANES, axis=2), F32)
    ang16 = 2.0 * np.pi * ((n2[:, None] * n2[None, :]) % DFT_C) / DFT_C
    c16, s16 = np.cos(ang16), np.sin(ang16)
    eye = np.eye(DFT_C)
    kron = lambda f: np.einsum("kn,ab->kanb", f, eye).reshape(DFT_C * DFT_C, DFT_C * DFT_C)
    gmat = np.block([[kron(c16), kron(s16)], [kron(-s16), kron(c16)]])
    return fa, tr, ti, jnp.asarray(gmat, F32)


def _position_dft(u):
    bx, w = u.shape[0], u.shape[-1]
    n_pos = math.prod(u.shape[1:-1])
    fa, tr, ti, gmat = _dft_tables(n_pos)
    fa_spec = _resident((2 * DFT_A, DFT_A))
    if n_pos == DFT_A:
        blk = pl.BlockSpec((1, DFT_A, w), lambda b: (b, 0, 0))
        out = jax.ShapeDtypeStruct((bx, DFT_A, w), BF16)
        return pl.pallas_call(
            _dft_short_kernel,
            grid=(bx,), in_specs=[fa_spec, blk], out_specs=[blk, blk], out_shape=[out, out],
            name="dft_context",
        )(fa, u)
    rows = DFT_C * DFT_C
    blk = pl.BlockSpec((1, DFT_C, DFT_A, w), lambda b: (b, 0, 0, 0))
    out = jax.ShapeDtypeStruct((bx, DFT_C, DFT_A, w), BF16)
    ur, ui = pl.pallas_call(
        _dft_long_kernel,
        grid=(bx,),
        in_specs=[fa_spec, _resident((2 * rows, 2 * rows)), _resident(tr.shape), _resident(ti.shape), blk],
        out_specs=[blk, blk], out_shape=[out, out],
        scratch_shapes=[pltpu.VMEM((DFT_C, DFT_A, w), BF16)] * 2,
        compiler_params=pltpu.CompilerParams(vmem_limit_bytes=VMEM_LIMIT_BYTES),
        name="dft_long",
    )(fa, gmat, tr, ti, u)
    return ur.reshape(bx, n_pos, w), ui.reshape(bx, n_pos, w)


def _tail_kernel(a_ref, ur_ref, ui_ref, x_ref, gate1_ref, sh_ref, sc_ref, gate2_ref,
                 gpost_mix_ref, gpre_ref, gpost_ref, wa_ref, wr_ref, wi_ref, wg_ref, wu_ref, wd_ref, o_ref,
                 *, ctx_row):
    gate1, shift, scale, gate2 = (_mod_row(r, ctx_row) for r in (gate1_ref, sh_ref, sc_ref, gate2_ref))
    mix = jnp.dot(a_ref[0], wa_ref[...], preferred_element_type=F32)
    mix += jnp.dot(ur_ref[0], wr_ref[...], preferred_element_type=F32)
    mix += jnp.dot(ui_ref[0], wi_ref[...], preferred_element_type=F32)
    x_mid = x_ref[0] + gate1 * _rms(mix, gpost_mix_ref[...])
    h = (_rms(x_mid, gpre_ref[...]) * (1.0 + scale) + shift).astype(BF16)
    hidden = wg_ref.shape[1]
    acc = jnp.zeros(x_mid.shape, F32)
    for lo in range(0, hidden, FFN_CHUNK):
        hi = min(lo + FFN_CHUNK, hidden)
        g = jnp.dot(h, wg_ref[:, lo:hi], preferred_element_type=F32)
        u = jnp.dot(h, wu_ref[:, lo:hi], preferred_element_type=F32)
        a = (g * jax.nn.sigmoid(g) * u).astype(BF16)
        acc += jnp.dot(a, wd_ref[lo:hi, :], preferred_element_type=F32)
    o_ref[0] = x_mid + gate2 * _rms(acc, gpost_ref[...])


def _tail(layer, attn, ur, ui, x, mod, g_post_mix, g_pre_ffn, g_post_ffn, w_out, w_four_r, w_four_i,
          wg, wu, wd, ctx_row=None):
    bx, t, d = x.shape
    tm = min(TOKEN_TILE, t)
    hidden = wg.shape[-1]
    tok = lambda width: pl.BlockSpec((1, tm, width), lambda b, i: (b, i, 0))
    gain = _layer_resident(layer, (1, d))
    wfour = _layer_resident(layer, (FOURIER_WIDTH, d))
    return pl.pallas_call(
        functools.partial(_tail_kernel, ctx_row=ctx_row),
        grid=(bx, t // tm),
        in_specs=[tok(ATTN_WIDTH), tok(FOURIER_WIDTH), tok(FOURIER_WIDTH), tok(d),
                  _mod_spec(layer, 2, d), _mod_spec(layer, 3, d), _mod_spec(layer, 4, d), _mod_spec(layer, 5, d),
                  gain, gain, gain,
                  _layer_resident(layer, (ATTN_WIDTH, d)), wfour, wfour,
                  _layer_resident(layer, (d, hidden)), _layer_resident(layer, (d, hidden)),
                  _layer_resident(layer, (hidden, d))],
        out_specs=tok(d),
        out_shape=jax.ShapeDtypeStruct((bx, t, d), F32),
        compiler_params=pltpu.CompilerParams(vmem_limit_bytes=VMEM_LIMIT_BYTES),
        name="tail",
    )(attn, ur, ui, x, mod, mod, mod, mod, g_post_mix, g_pre_ffn, g_post_ffn,
      w_out, w_four_r, w_four_i, wg, wu, wd)


def _rope_tables(seq):
    t = np.arange(seq)
    freqs = ROPE_THETA ** (-np.arange(ROPE_PAIRS_PER_AXIS, dtype=np.float64) / ROPE_PAIRS_PER_AXIS)
    ang = np.concatenate([(t // GRID_W)[:, None] * freqs, (t % GRID_W)[:, None] * freqs], axis=-1)
    cos = np.repeat(np.cos(ang), 2, axis=-1)
    sin = np.repeat(np.sin(ang), 2, axis=-1) * np.tile([-1.0, 1.0], HEAD_DIM // 2)
    heads_per_row = LANES // HEAD_DIM
    return (jnp.asarray(np.tile(cos, (1, heads_per_row)), F32), jnp.asarray(np.tile(sin, (1, heads_per_row)), F32))


def kernel(x, c, ctx, c_ctx, w_ada, b_ada, norm_pre_mix, norm_post_mix, norm_pre_ffn, norm_post_ffn,
           w_in, w_out, w_fourier, sink, w_gate, w_up, w_down):
    batch, seq, d = x.shape
    depth = w_ada.shape[0]
    assert batch + 1 <= MOD_ROWS

    cc = jnp.zeros((MOD_ROWS, d), F32).at[:batch].set(c).at[batch].set(c_ctx)
    mod = _ada(cc, w_ada, b_ada)
    w_four_r, w_four_i = _fold_fourier_weights(w_fourier, w_out)
    rope = _rope_tables(seq)

    q_scale = np.ones((w_in.shape[-1],), np.float32)
    q_scale[:ATTN_WIDTH] = HEAD_DIM ** -0.5 * LOG2_E
    w_in_b = (w_in * q_scale).astype(BF16)
    w_out_b = w_out.astype(BF16)
    wg, wu, wd = w_gate.astype(BF16), w_up.astype(BF16), w_down.astype(BF16)
    gains = [g.reshape(depth, 1, d) for g in (norm_pre_mix, norm_post_mix, norm_pre_ffn, norm_post_ffn)]
    g_pre_mix, g_post_mix, g_pre_ffn, g_post_ffn = gains

    xc = ctx
    for i in range(depth):
        def finish(stream, attn, ur, ui, ctx_row):
            return _tail(i, attn, ur, ui, stream, mod, g_post_mix, g_pre_ffn, g_post_ffn,
                         w_out_b, w_four_r, w_four_i, wg, wu, wd, ctx_row)

        q, k, v, u = _project(i, x, mod, g_pre_mix, w_in_b, rope)
        qc, kc, vc, uc = _project(i, xc, mod, g_pre_mix, w_in_b, None, ctx_row=batch)
        attn = _attention(q, k, v, kc, vc, sink[i], local=True)
        ur, ui = _position_dft(u)
        x = finish(x, attn, ur, ui, None)

        if i < depth - 1:
            attn_c = _attention(qc, None, None, kc, vc, sink[i], local=False)
            urc, uic = _position_dft(uc)
            xc = finish(xc, attn_c, urc, uic, batch)
    return x
```

```python
import functools
import math

import numpy as np
import jax
import jax.numpy as jnp
from jax import lax
from jax.experimental import pallas as pl
from jax.experimental.pallas import tpu as pltpu

F32 = jnp.float32
BF16 = jnp.bfloat16

GRID_W = 64
HEAD_DIM = 64
N_Q_HEADS = 8
N_KV_HEADS = 2
Q_PER_KV = N_Q_HEADS // N_KV_HEADS
ATTN_WIDTH = N_Q_HEADS * HEAD_DIM
KV_WIDTH = N_KV_HEADS * HEAD_DIM
N_FOURIER_GROUPS = 8
FOURIER_GROUP_DIM = 64
FOURIER_WIDTH = N_FOURIER_GROUPS * FOURIER_GROUP_DIM
WINDOW = 128
ROPE_THETA = 10000.0
ROPE_PAIRS_PER_AXIS = HEAD_DIM // 4
RMS_EPS = 1e-6
N_MOD = 6
NEG_INF = -1e30
LOG2_E = math.log2(math.e)

LANES = 128
BF16_SUBLANES = 16
VMEM_LIMIT_BYTES = 60 * 1024 * 1024

DFT_A = 256
DFT_C = 16

MOD_ROWS = 8
TOKEN_TILE = 512
FFN_CHUNK = 512
TAIL_TILE = 1024
TAIL_SLICE_ROWS = 512
ATTN_Q_TILE = 1024
VT_HEAD_ROWS = HEAD_DIM + BF16_SUBLANES
VT_ROWS = N_KV_HEADS * VT_HEAD_ROWS


def _resident(shape):
    zeros = (0,) * len(shape)
    return pl.BlockSpec(shape, lambda *_: zeros, pipeline_mode=pl.Buffered(1))


def _layer_resident(layer, shape, index=None):
    index = (0,) * len(shape) if index is None else index
    return pl.BlockSpec((None,) + tuple(shape), lambda *_: (layer,) + tuple(index), pipeline_mode=pl.Buffered(1))


def _mod_spec(layer, which, d):
    return pl.BlockSpec((None, MOD_ROWS, d), lambda *_: (layer, 0, which))


def _mod_row(ref, ctx_row):
    row = pl.program_id(0) if ctx_row is None else ctx_row
    return ref[pl.ds(row, 1), :]


def _rms(xf, g):
    ms = jnp.mean(xf * xf, axis=-1, keepdims=True)
    return xf * lax.rsqrt(ms + RMS_EPS) * g


def _split_bf16(x):
    hi = x.astype(BF16)
    return hi, (x - hi.astype(F32)).astype(BF16)


def _dot3(a, b):
    (a_hi, a_lo), (b_hi, b_lo) = a, b
    dot = functools.partial(jnp.dot, preferred_element_type=F32)
    return dot(a_hi, b_hi) + dot(a_lo, b_hi) + dot(a_hi, b_lo)


def _ada_kernel(c_ref, w_ref, b_ref, o_ref):
    c = c_ref[...]
    a = c * jax.nn.sigmoid(c)
    a_hi, a_lo = _split_bf16(a)
    w_hi, w_lo = _split_bf16(w_ref[0])
    rows = a.shape[0]
    r = jnp.dot(jnp.concatenate([a_hi, a_lo], axis=0), w_hi, preferred_element_type=F32)
    o_ref[0] = r[:rows] + r[rows:] + jnp.dot(a_hi, w_lo, preferred_element_type=F32) + b_ref[0]


def _ada(cc, w_ada, b_ada):
    depth, d, n = w_ada.shape
    tn = 1536
    return pl.pallas_call(
        _ada_kernel,
        grid=(depth, n // tn),
        in_specs=[
            pl.BlockSpec((MOD_ROWS, d), lambda i, j: (0, 0)),
            pl.BlockSpec((1, d, tn), lambda i, j: (i, 0, j)),
            pl.BlockSpec((1, 1, tn), lambda i, j: (i, 0, j)),
        ],
        out_specs=pl.BlockSpec((1, MOD_ROWS, tn), lambda i, j: (i, 0, j)),
        out_shape=jax.ShapeDtypeStruct((depth, MOD_ROWS, n), F32),
        compiler_params=pltpu.CompilerParams(vmem_limit_bytes=VMEM_LIMIT_BYTES),
        name="ada",
    )(cc, w_ada, b_ada.reshape(depth, 1, n))


def _fold_kernel(c64_ref, s64_ref, wf_ref, wo_ref, wr_ref, wi_ref):
    gd = c64_ref.shape[0]
    for g in range(wf_ref.shape[0]):
        rows = slice(g * gd, (g + 1) * gd)
        t = _split_bf16(_dot3(_split_bf16(wf_ref[g]), _split_bf16(wo_ref[rows, :])))
        wr_ref[rows, :] = _dot3(_split_bf16(c64_ref[...]), t).astype(BF16)
        wi_ref[rows, :] = _dot3(_split_bf16(s64_ref[...]), t).astype(BF16)


def _fold_fourier_weights(w_fourier, w_out):
    depth, groups, gd, _ = w_fourier.shape
    d = w_out.shape[-1]
    width = groups * gd
    assert w_out.shape[1] == ATTN_WIDTH + width and ATTN_WIDTH == width
    j = np.arange(gd)
    ang = 2.0 * np.pi * ((j[:, None] * j[None, :]) % gd) / gd
    c64 = jnp.asarray(np.cos(ang), F32)
    s64 = jnp.asarray(np.sin(ang), F32)
    out = jax.ShapeDtypeStruct((depth, width, d), BF16)
    return pl.pallas_call(
        _fold_kernel,
        grid=(depth,),
        in_specs=[
            pl.BlockSpec((gd, gd), lambda i: (0, 0)),
            pl.BlockSpec((gd, gd), lambda i: (0, 0)),
            pl.BlockSpec((None, groups, gd, gd), lambda i: (i, 0, 0, 0)),
            pl.BlockSpec((None, width, d), lambda i: (i, 1, 0)),
        ],
        out_specs=[pl.BlockSpec((None, width, d), lambda i: (i, 0, 0))] * 2,
        out_shape=[out, out],
        name="fold_fourier",
    )(c64, s64, w_fourier, w_out)


def _proj_kernel(x_ref, sh_ref, sc_ref, g_ref, w_ref, *rest, rope, ctx_row):
    if rope:
        cos_ref, sin_ref, q_ref, k_ref, v_ref, u_ref, u_scratch = rest
    else:
        q_ref, k_ref, v_ref, u_ref = rest
    h = _rms(x_ref[0], g_ref[...]) * (1.0 + _mod_row(sc_ref, ctx_row)) + _mod_row(sh_ref, ctx_row)
    p = jnp.dot(h.astype(BF16), w_ref[...], preferred_element_type=F32)
    k_lo = ATTN_WIDTH
    v_lo = ATTN_WIDTH + KV_WIDTH
    u_lo = ATTN_WIDTH + 2 * KV_WIDTH
    if rope:
        cos = cos_ref[...]
        sin = sin_ref[...]
        even_lane = (lax.broadcasted_iota(jnp.int32, cos.shape, 1) & 1) == 0

        def rot(xs):
            partner = jnp.where(even_lane, pltpu.roll(xs, LANES - 1, 1), pltpu.roll(xs, 1, 1))
            return xs * cos + partner * sin

        for j in range(ATTN_WIDTH // LANES):
            q_ref[0, :, j * LANES:(j + 1) * LANES] = rot(p[:, j * LANES:(j + 1) * LANES]).astype(BF16)
        k_ref[0] = rot(p[:, k_lo:v_lo]).astype(BF16)
    else:
        q_ref[0] = p[:, :k_lo].astype(BF16)
        k_ref[0] = p[:, k_lo:v_lo].astype(BF16)
    vt = p[:, v_lo:u_lo].T.astype(BF16)
    ones = jnp.ones((VT_HEAD_ROWS - HEAD_DIM, vt.shape[1]), BF16)
    for h_kv in range(N_KV_HEADS):
        v_ref[0, h_kv * VT_HEAD_ROWS:h_kv * VT_HEAD_ROWS + HEAD_DIM] = vt[h_kv * HEAD_DIM:(h_kv + 1) * HEAD_DIM]
        v_ref[0, h_kv * VT_HEAD_ROWS + HEAD_DIM:(h_kv + 1) * VT_HEAD_ROWS] = ones
    if rope:
        per = u_scratch.shape[1] // DFT_C
        for c in range(u_scratch.shape[0]):
            u_scratch[c] = p[:, u_lo + c * LANES:u_lo + (c + 1) * LANES]
            for n2 in range(DFT_C):
                u_ref[0, n2, :, c * LANES:(c + 1) * LANES] = (
                    u_scratch[c, pl.ds(n2, per, stride=DFT_C), :].astype(BF16))
    else:
        u_ref[0] = p[:, u_lo:].astype(BF16)


def _project(layer, x, mod, gains, w_in, rope_tables, ctx_row=None):
    bx, t, d = x.shape
    tm = min(TOKEN_TILE, t)
    n_in = w_in.shape[-1]
    rope = rope_tables is not None
    tok = lambda width: pl.BlockSpec((1, tm, width), lambda b, i: (b, i, 0))
    in_specs = [tok(d), _mod_spec(layer, 0, d), _mod_spec(layer, 1, d),
                _layer_resident(layer, (1, d)), _layer_resident(layer, (d, n_in))]
    args = [x, mod, mod, gains, w_in]
    if rope:
        in_specs += [pl.BlockSpec((tm, LANES), lambda b, i: (i, 0))] * 2
        args += list(rope_tables)
    widths = (ATTN_WIDTH, KV_WIDTH, KV_WIDTH, FOURIER_WIDTH)
    out_specs = [tok(w) for w in widths]
    out_shape = [jax.ShapeDtypeStruct((bx, t, w), BF16) for w in widths]
    out_specs[2] = pl.BlockSpec((1, VT_ROWS, tm), lambda b, i: (b, 0, i))
    out_shape[2] = jax.ShapeDtypeStruct((bx, VT_ROWS, t), BF16)
    scratch = []
    if rope:
        assert t == DFT_A * DFT_C and tm % DFT_C == 0
        out_specs[3] = pl.BlockSpec((1, DFT_C, tm // DFT_C, FOURIER_WIDTH), lambda b, i: (b, 0, i, 0))
        out_shape[3] = jax.ShapeDtypeStruct((bx, DFT_C, DFT_A, FOURIER_WIDTH), BF16)
        scratch = [pltpu.VMEM((FOURIER_WIDTH // LANES, tm, LANES), F32)]
    return pl.pallas_call(
        functools.partial(_proj_kernel, rope=rope, ctx_row=ctx_row),
        grid=(bx, t // tm),
        in_specs=in_specs,
        out_specs=out_specs,
        out_shape=out_shape,
        scratch_shapes=scratch,
        compiler_params=pltpu.CompilerParams(vmem_limit_bytes=VMEM_LIMIT_BYTES),
        name="project_rope" if rope else "project",
    )(*args)


def _attn_kernel(sink_ref, q_ref, *rest, local):
    w = WINDOW
    if local:
        kp_ref, kc_ref, kn_ref, vp_ref, vc_ref, vn_ref, kx_ref, vx_ref, o_ref = rest
        k_all = jnp.concatenate([kp_ref[0], kc_ref[0], kn_ref[0]], axis=0)
        vt_all = jnp.concatenate([vp_ref[0], vc_ref[0], vn_ref[0]], axis=1)
        step = pl.program_id(1)
        last = pl.num_programs(1) - 1
    else:
        kx_ref, vx_ref, o_ref = rest
    kx = kx_ref[0]
    vxt = vx_ref[0]
    n_sub = q_ref.shape[1] // w
    cols = Q_PER_KV * w
    group = lax.shift_right_logical(lax.broadcasted_iota(jnp.int32, (1, cols), 1), int(math.log2(w)))
    if local:
        kj = lax.broadcasted_iota(jnp.int32, (w, cols), 0)
        qi = lax.broadcasted_iota(jnp.int32, (w, cols), 1) & (w - 1)
        far = 1 << 20
        bias_prev = jnp.where(kj >= qi, 0.0, NEG_INF)
        bias_next = jnp.where(kj <= qi, 0.0, NEG_INF)
        bias_prev_first = jnp.where(kj >= qi + jnp.where(step > 0, 0, far), 0.0, NEG_INF)
        bias_next_last = jnp.where(kj <= qi - jnp.where(step < last, 0, far), 0.0, NEG_INF)
    contract_last = (((1,), (1,)), ((), ()))

    def sink_row(h):
        sink = jnp.full((1, cols), sink_ref[Q_PER_KV * h] * LOG2_E, F32)
        for g in range(1, Q_PER_KV):
            sink = jnp.where(group == g, sink_ref[Q_PER_KV * h + g] * LOG2_E, sink)
        return sink

    sinks = [sink_row(h) for h in range(N_KV_HEADS)]

    def scores(h, s):
        hs = slice(h * HEAD_DIM, (h + 1) * HEAD_DIM)
        q = q_ref[0, s * w:(s + 1) * w, :]
        qs = jnp.concatenate([q[:, (Q_PER_KV * h + g) * HEAD_DIM:(Q_PER_KV * h + g + 1) * HEAD_DIM]
                              for g in range(Q_PER_KV)], axis=0)
        pieces = []
        if local:
            sl = lax.dot_general(k_all[s * w:(s + 3) * w, hs], qs, contract_last, preferred_element_type=F32)
            pieces += [sl[:w] + (bias_prev_first if s == 0 else bias_prev), sl[w:2 * w],
                       sl[2 * w:] + (bias_next_last if s == n_sub - 1 else bias_next)]
        pieces.append(lax.dot_general(kx[:, hs], qs, contract_last, preferred_element_type=F32))
        m = sinks[h]
        for x in pieces:
            m = jnp.maximum(m, jnp.max(x, axis=0, keepdims=True))
        return pieces, m

    def finish(h, s, pieces, m):
        vs = slice(h * VT_HEAD_ROWS, (h + 1) * VT_HEAD_ROWS)
        probs = [jnp.exp2(x - m).astype(BF16) for x in pieces]
        ot = jnp.dot(vxt[vs, :], probs[-1], preferred_element_type=F32)
        if local:
            ot = ot + jnp.dot(vt_all[vs, s * w:(s + 3) * w], jnp.concatenate(probs[:3], axis=0),
                              preferred_element_type=F32)
        denom = ot[HEAD_DIM:HEAD_DIM + 1] + jnp.exp2(sinks[h] - m)
        ot = ot[:HEAD_DIM] / denom
        o = jnp.concatenate([ot[:, g * w:(g + 1) * w].T for g in range(Q_PER_KV)], axis=1)
        lo = h * Q_PER_KV * HEAD_DIM
        o_ref[0, s * w:(s + 1) * w, lo:lo + Q_PER_KV * HEAD_DIM] = o.astype(BF16)

    work = [(h, s) for h in range(N_KV_HEADS) for s in range(n_sub)]
    ready = scores(*work[0])
    for i, (h, s) in enumerate(work):
        upcoming = scores(*work[i + 1]) if i + 1 < len(work) else None
        finish(h, s, *ready)
        ready = upcoming


def _attention(q, k, v, kx, vx, sink, local):
    bx, t, _ = q.shape
    cx = kx.shape[1]
    smem = pl.BlockSpec(memory_space=pltpu.SMEM)
    kx_spec = pl.BlockSpec((1, cx, KV_WIDTH), lambda b, n: (b, 0, 0))
    vx_spec = pl.BlockSpec((1, VT_ROWS, cx), lambda b, n: (b, 0, 0))
    tq = min(ATTN_Q_TILE, t)
    nb = t // tq
    q_spec = pl.BlockSpec((1, tq, ATTN_WIDTH), lambda b, n: (b, n, 0))
    if local:
        per = tq // WINDOW
        n_win = t // WINDOW
        before = lambda n: jnp.maximum(n * per - 1, 0)
        after = lambda n: jnp.minimum((n + 1) * per, n_win - 1)
        k_specs = [pl.BlockSpec((1, WINDOW, KV_WIDTH), lambda b, n: (b, before(n), 0)),
                   pl.BlockSpec((1, tq, KV_WIDTH), lambda b, n: (b, n, 0)),
                   pl.BlockSpec((1, WINDOW, KV_WIDTH), lambda b, n: (b, after(n), 0))]
        v_specs = [pl.BlockSpec((1, VT_ROWS, WINDOW), lambda b, n: (b, 0, before(n))),
                   pl.BlockSpec((1, VT_ROWS, tq), lambda b, n: (b, 0, n)),
                   pl.BlockSpec((1, VT_ROWS, WINDOW), lambda b, n: (b, 0, after(n)))]
        in_specs = [smem, q_spec] + k_specs + v_specs + [kx_spec, vx_spec]
        args = (sink, q, k, k, k, v, v, v, kx, vx)
    else:
        in_specs = [smem, q_spec, kx_spec, vx_spec]
        args = (sink, q, kx, vx)
    return pl.pallas_call(
        functools.partial(_attn_kernel, local=local),
        grid=(bx, nb),
        in_specs=in_specs,
        out_specs=pl.BlockSpec((1, tq, ATTN_WIDTH), lambda b, n: (b, n, 0)),
        out_shape=jax.ShapeDtypeStruct((bx, t, ATTN_WIDTH), BF16),
        compiler_params=pltpu.CompilerParams(vmem_limit_bytes=VMEM_LIMIT_BYTES),
        name="attn_window" if local else "attn_context",
    )(*args)


def _dft_short_kernel(fa_ref, u_ref, ur_ref, ui_ref):
    y = jnp.dot(fa_ref[...].astype(BF16), u_ref[0], preferred_element_type=F32)
    ur_ref[0] = y[:DFT_A].astype(BF16)
    ui_ref[0] = y[DFT_A:].astype(BF16)


def _dft_long_kernel(fa_ref, g_ref, tr_ref, ti_ref, u_ref, ur_ref, ui_ref, yr_s, yi_s):
    fa = fa_ref[...].astype(BF16)
    for n2 in range(DFT_C):
        y = jnp.dot(fa, u_ref[0, n2], preferred_element_type=F32)
        tr = tr_ref[n2]
        ti = ti_ref[n2]
        for c in range(y.shape[1] // LANES):
            sl = slice(c * LANES, (c + 1) * LANES)
            yr = y[:DFT_A, sl]
            yi = y[DFT_A:, sl]
            yr_s[n2, :, sl] = (yr * tr - yi * ti).astype(BF16)
            yi_s[n2, :, sl] = (yr * ti + yi * tr).astype(BF16)
    g = g_ref[...].astype(BF16)
    half = DFT_C * DFT_C
    for j in range(DFT_A // DFT_C):
        sl = slice(j * DFT_C, (j + 1) * DFT_C)
        y = jnp.concatenate([yr_s[n2, sl, :] for n2 in range(DFT_C)]
                            + [yi_s[n2, sl, :] for n2 in range(DFT_C)], axis=0)
        r = jnp.dot(g, y, preferred_element_type=F32).astype(BF16)
        for k2 in range(DFT_C):
            ur_ref[0, k2, sl, :] = r[k2 * DFT_C:(k2 + 1) * DFT_C]
            ui_ref[0, k2, sl, :] = r[half + k2 * DFT_C:half + (k2 + 1) * DFT_C]


def _dft_tables(n_pos):
    a = np.arange(DFT_A)
    ang = 2.0 * np.pi * ((a[:, None] * a[None, :]) % DFT_A) / DFT_A
    norm = 1.0 / math.sqrt(n_pos * FOURIER_GROUP_DIM)
    fa = jnp.asarray(np.concatenate([np.cos(ang), -np.sin(ang)], axis=0) * norm, F32)
    if n_pos == DFT_A:
        return fa, None, None, None
    assert n_pos == DFT_A * DFT_C
    n2 = np.arange(DFT_C)
    tw = 2.0 * np.pi * (n2[:, None] * a[None, :]) / n_pos
    tr = jnp.asarray(np.repeat(np.cos(tw)[:, :, None], LANES, axis=2), F32)
    ti = jnp.asarray(np.repeat(-np.sin(tw)[:, :, None], LANES, axis=2), F32)
    ang16 = 2.0 * np.pi * ((n2[:, None] * n2[None, :]) % DFT_C) / DFT_C
    c16, s16 = np.cos(ang16), np.sin(ang16)
    eye = np.eye(DFT_C)
    kron = lambda f: np.einsum("kn,ab->kanb", f, eye).reshape(DFT_C * DFT_C, DFT_C * DFT_C)
    gmat = np.block([[kron(c16), kron(s16)], [kron(-s16), kron(c16)]])
    return fa, tr, ti, jnp.asarray(gmat, F32)


def _position_dft(u):
    bx, w = u.shape[0], u.shape[-1]
    n_pos = math.prod(u.shape[1:-1])
    fa, tr, ti, gmat = _dft_tables(n_pos)
    fa_spec = _resident((2 * DFT_A, DFT_A))
    if n_pos == DFT_A:
        blk = pl.BlockSpec((1, DFT_A, w), lambda b: (b, 0, 0))
        out = jax.ShapeDtypeStruct((bx, DFT_A, w), BF16)
        return pl.pallas_call(
            _dft_short_kernel,
            grid=(bx,), in_specs=[fa_spec, blk], out_specs=[blk, blk], out_shape=[out, out],
            name="dft_context",
        )(fa, u)
    rows = DFT_C * DFT_C
    blk = pl.BlockSpec((1, DFT_C, DFT_A, w), lambda b: (b, 0, 0, 0))
    out = jax.ShapeDtypeStruct((bx, DFT_C, DFT_A, w), BF16)
    ur, ui = pl.pallas_call(
        _dft_long_kernel,
        grid=(bx,),
        in_specs=[fa_spec, _resident((2 * rows, 2 * rows)), _resident(tr.shape), _resident(ti.shape), blk],
        out_specs=[blk, blk], out_shape=[out, out],
        scratch_shapes=[pltpu.VMEM((DFT_C, DFT_A, w), BF16)] * 2,
        compiler_params=pltpu.CompilerParams(vmem_limit_bytes=VMEM_LIMIT_BYTES),
        name="dft_long",
    )(fa, gmat, tr, ti, u)
    return ur.reshape(bx, n_pos, w), ui.reshape(bx, n_pos, w)


def _tail_kernel(a_ref, ur_ref, ui_ref, x_ref, gate1_ref, sh_ref, sc_ref, gate2_ref,
                 gpost_mix_ref, gpre_ref, gpost_ref, wa_ref, wr_ref, wi_ref, wg_ref, wu_ref, wd_ref, o_ref,
                 *, ctx_row):
    gate1, shift, scale, gate2 = (_mod_row(r, ctx_row) for r in (gate1_ref, sh_ref, sc_ref, gate2_ref))
    gain_mix = gate1 * gpost_mix_ref[...]
    gain_in = (1.0 + scale) * gpre_ref[...]
    gain_out = gate2 * gpost_ref[...]
    hidden = wg_ref.shape[1]
    tm = x_ref.shape[1]
    per = min(TAIL_SLICE_ROWS, tm)
    assert tm % per == 0

    def mix_dots(rows):
        mix = jnp.dot(a_ref[0, rows, :], wa_ref[...], preferred_element_type=F32)
        mix += jnp.dot(ur_ref[0, rows, :], wr_ref[...], preferred_element_type=F32)
        mix += jnp.dot(ui_ref[0, rows, :], wi_ref[...], preferred_element_type=F32)
        return mix

    def norms(rows, mix):
        x_mid = x_ref[0, rows, :] + _rms(mix, gain_mix)
        return x_mid, (_rms(x_mid, gain_in) + shift).astype(BF16)

    def ffn(h):
        acc = jnp.zeros((h.shape[0], o_ref.shape[2]), F32)
        for lo in range(0, hidden, FFN_CHUNK):
            hi = min(lo + FFN_CHUNK, hidden)
            g = jnp.dot(h, wg_ref[:, lo:hi], preferred_element_type=F32)
            u = jnp.dot(h, wu_ref[:, lo:hi], preferred_element_type=F32)
            a = (g * jax.nn.sigmoid(g) * u).astype(BF16)
            acc += jnp.dot(a, wd_ref[lo:hi, :], preferred_element_type=F32)
        return acc

    def epilogue(rows, x_mid, acc):
        o_ref[0, rows, :] = x_mid + _rms(acc, gain_out)

    slices = [slice(r, r + per) for r in range(0, tm, per)]
    mixes = [mix_dots(rows) for rows in slices]
    staged = norms(slices[0], mixes[0])
    pending = None
    for i, rows in enumerate(slices):
        x_mid, h = staged
        acc = ffn(h)
        if i + 1 < len(slices):
            staged = norms(slices[i + 1], mixes[i + 1])
        if pending is not None:
            epilogue(*pending)
        pending = (rows, x_mid, acc)
    epilogue(*pending)


def _tail(layer, attn, ur, ui, x, mod, g_post_mix, g_pre_ffn, g_post_ffn, w_out, w_four_r, w_four_i,
          wg, wu, wd, ctx_row=None):
    bx, t, d = x.shape
    tm = min(TAIL_TILE, t)
    hidden = wg.shape[-1]
    tok = lambda width: pl.BlockSpec((1, tm, width), lambda b, i: (b, i, 0))
    gain = _layer_resident(layer, (1, d))
    wfour = _layer_resident(layer, (FOURIER_WIDTH, d))
    return pl.pallas_call(
        functools.partial(_tail_kernel, ctx_row=ctx_row),
        grid=(bx, t // tm),
        in_specs=[tok(ATTN_WIDTH), tok(FOURIER_WIDTH), tok(FOURIER_WIDTH), tok(d),
                  _mod_spec(layer, 2, d), _mod_spec(layer, 3, d), _mod_spec(layer, 4, d), _mod_spec(layer, 5, d),
                  gain, gain, gain,
                  _layer_resident(layer, (ATTN_WIDTH, d)), wfour, wfour,
                  _layer_resident(layer, (d, hidden)), _layer_resident(layer, (d, hidden)),
                  _layer_resident(layer, (hidden, d))],
        out_specs=tok(d),
        out_shape=jax.ShapeDtypeStruct((bx, t, d), F32),
        compiler_params=pltpu.CompilerParams(vmem_limit_bytes=VMEM_LIMIT_BYTES),
        name="tail",
    )(attn, ur, ui, x, mod, mod, mod, mod, g_post_mix, g_pre_ffn, g_post_ffn,
      w_out, w_four_r, w_four_i, wg, wu, wd)


def _rope_tables(seq):
    t = np.arange(seq)
    freqs = ROPE_THETA ** (-np.arange(ROPE_PAIRS_PER_AXIS, dtype=np.float64) / ROPE_PAIRS_PER_AXIS)
    ang = np.concatenate([(t // GRID_W)[:, None] * freqs, (t % GRID_W)[:, None] * freqs], axis=-1)
    cos = np.repeat(np.cos(ang), 2, axis=-1)
    sin = np.repeat(np.sin(ang), 2, axis=-1) * np.tile([-1.0, 1.0], HEAD_DIM // 2)
    heads_per_row = LANES // HEAD_DIM
    return (jnp.asarray(np.tile(cos, (1, heads_per_row)), F32), jnp.asarray(np.tile(sin, (1, heads_per_row)), F32))


def kernel(x, c, ctx, c_ctx, w_ada, b_ada, norm_pre_mix, norm_post_mix, norm_pre_ffn, norm_post_ffn,
           w_in, w_out, w_fourier, sink, w_gate, w_up, w_down):
    batch, seq, d = x.shape
    depth = w_ada.shape[0]
    assert batch + 1 <= MOD_ROWS

    cc = jnp.zeros((MOD_ROWS, d), F32).at[:batch].set(c).at[batch].set(c_ctx)
    mod = _ada(cc, w_ada, b_ada)
    w_four_r, w_four_i = _fold_fourier_weights(w_fourier, w_out)
    rope = _rope_tables(seq)

    q_scale = np.ones((w_in.shape[-1],), np.float32)
    q_scale[:ATTN_WIDTH] = HEAD_DIM ** -0.5 * LOG2_E
    w_in_b = (w_in * q_scale).astype(BF16)
    w_out_b = w_out.astype(BF16)
    wg, wu, wd = w_gate.astype(BF16), w_up.astype(BF16), w_down.astype(BF16)
    gains = [g.reshape(depth, 1, d) for g in (norm_pre_mix, norm_post_mix, norm_pre_ffn, norm_post_ffn)]
    g_pre_mix, g_post_mix, g_pre_ffn, g_post_ffn = gains

    xc = ctx
    for i in range(depth):
        def finish(stream, attn, ur, ui, ctx_row):
            return _tail(i, attn, ur, ui, stream, mod, g_post_mix, g_pre_ffn, g_post_ffn,
                         w_out_b, w_four_r, w_four_i, wg, wu, wd, ctx_row)

        q, k, v, u = _project(i, x, mod, g_pre_mix, w_in_b, rope)
        qc, kc, vc, uc = _project(i, xc, mod, g_pre_mix, w_in_b, None, ctx_row=batch)
        attn = _attention(q, k, v, kc, vc, sink[i], local=True)
        ur, ui = _position_dft(u)
        x = finish(x, attn, ur, ui, None)

        if i < depth - 1:
            attn_c = _attention(qc, None, None, kc, vc, sink[i], local=False)
            urc, uic = _position_dft(uc)
            xc = finish(xc, attn_c, urc, uic, batch)
    return x
```

```python
import functools
import math

import numpy as np
import jax
import jax.numpy as jnp
from jax import lax
from jax.experimental import pallas as pl
from jax.experimental.pallas import tpu as pltpu

F32 = jnp.float32
BF16 = jnp.bfloat16

GRID_W = 64
HEAD_DIM = 64
N_Q_HEADS = 8
N_KV_HEADS = 2
Q_PER_KV = N_Q_HEADS // N_KV_HEADS
ATTN_WIDTH = N_Q_HEADS * HEAD_DIM
KV_WIDTH = N_KV_HEADS * HEAD_DIM
N_FOURIER_GROUPS = 8
FOURIER_GROUP_DIM = 64
FOURIER_WIDTH = N_FOURIER_GROUPS * FOURIER_GROUP_DIM
WINDOW = 128
ROPE_THETA = 10000.0
ROPE_PAIRS_PER_AXIS = HEAD_DIM // 4
RMS_EPS = 1e-6
N_MOD = 6
NEG_INF = -1e30
LOG2_E = math.log2(math.e)

LANES = 128
BF16_SUBLANES = 16
VMEM_LIMIT_BYTES = 60 * 1024 * 1024

DFT_A = 256
DFT_C = 16

MOD_ROWS = 8
PROJ_TILE = 1024
PROJ_SLICE_ROWS = 256
FFN_CHUNK = 512
TAIL_TILE = 1024
TAIL_SLICE_ROWS = 512
ATTN_Q_TILE = 1024
VT_HEAD_ROWS = HEAD_DIM + BF16_SUBLANES
VT_ROWS = N_KV_HEADS * VT_HEAD_ROWS


def _resident(shape):
    zeros = (0,) * len(shape)
    return pl.BlockSpec(shape, lambda *_: zeros, pipeline_mode=pl.Buffered(1))


def _layer_resident(layer, shape, index=None):
    index = (0,) * len(shape) if index is None else index
    return pl.BlockSpec((None,) + tuple(shape), lambda *_: (layer,) + tuple(index), pipeline_mode=pl.Buffered(1))


def _mod_spec(layer, which, d):
    return pl.BlockSpec((None, MOD_ROWS, d), lambda *_: (layer, 0, which))


def _mod_row(ref, ctx_row):
    row = pl.program_id(0) if ctx_row is None else ctx_row
    return ref[pl.ds(row, 1), :]


def _rms(xf, g):
    ms = jnp.mean(xf * xf, axis=-1, keepdims=True)
    return xf * lax.rsqrt(ms + RMS_EPS) * g


def _split_bf16(x):
    hi = x.astype(BF16)
    return hi, (x - hi.astype(F32)).astype(BF16)


def _dot3(a, b):
    (a_hi, a_lo), (b_hi, b_lo) = a, b
    dot = functools.partial(jnp.dot, preferred_element_type=F32)
    return dot(a_hi, b_hi) + dot(a_lo, b_hi) + dot(a_hi, b_lo)


def _ada_kernel(c_ref, w_ref, b_ref, o_ref):
    c = c_ref[...]
    a = c * jax.nn.sigmoid(c)
    a_hi, a_lo = _split_bf16(a)
    w_hi, w_lo = _split_bf16(w_ref[0])
    rows = a.shape[0]
    r = jnp.dot(jnp.concatenate([a_hi, a_lo], axis=0), w_hi, preferred_element_type=F32)
    o_ref[0] = r[:rows] + r[rows:] + jnp.dot(a_hi, w_lo, preferred_element_type=F32) + b_ref[0]


def _ada(cc, w_ada, b_ada):
    depth, d, n = w_ada.shape
    tn = 1536
    return pl.pallas_call(
        _ada_kernel,
        grid=(depth, n // tn),
        in_specs=[
            pl.BlockSpec((MOD_ROWS, d), lambda i, j: (0, 0)),
            pl.BlockSpec((1, d, tn), lambda i, j: (i, 0, j)),
            pl.BlockSpec((1, 1, tn), lambda i, j: (i, 0, j)),
        ],
        out_specs=pl.BlockSpec((1, MOD_ROWS, tn), lambda i, j: (i, 0, j)),
        out_shape=jax.ShapeDtypeStruct((depth, MOD_ROWS, n), F32),
        compiler_params=pltpu.CompilerParams(vmem_limit_bytes=VMEM_LIMIT_BYTES),
        name="ada",
    )(cc, w_ada, b_ada.reshape(depth, 1, n))


def _fold_kernel(c64_ref, s64_ref, wf_ref, wo_ref, wr_ref, wi_ref):
    gd = c64_ref.shape[0]
    for g in range(wf_ref.shape[0]):
        rows = slice(g * gd, (g + 1) * gd)
        t = _split_bf16(_dot3(_split_bf16(wf_ref[g]), _split_bf16(wo_ref[rows, :])))
        wr_ref[rows, :] = _dot3(_split_bf16(c64_ref[...]), t).astype(BF16)
        wi_ref[rows, :] = _dot3(_split_bf16(s64_ref[...]), t).astype(BF16)


def _fold_fourier_weights(w_fourier, w_out):
    depth, groups, gd, _ = w_fourier.shape
    d = w_out.shape[-1]
    width = groups * gd
    assert w_out.shape[1] == ATTN_WIDTH + width and ATTN_WIDTH == width
    j = np.arange(gd)
    ang = 2.0 * np.pi * ((j[:, None] * j[None, :]) % gd) / gd
    c64 = jnp.asarray(np.cos(ang), F32)
    s64 = jnp.asarray(np.sin(ang), F32)
    out = jax.ShapeDtypeStruct((depth, width, d), BF16)
    return pl.pallas_call(
        _fold_kernel,
        grid=(depth,),
        in_specs=[
            pl.BlockSpec((gd, gd), lambda i: (0, 0)),
            pl.BlockSpec((gd, gd), lambda i: (0, 0)),
            pl.BlockSpec((None, groups, gd, gd), lambda i: (i, 0, 0, 0)),
            pl.BlockSpec((None, width, d), lambda i: (i, 1, 0)),
        ],
        out_specs=[pl.BlockSpec((None, width, d), lambda i: (i, 0, 0))] * 2,
        out_shape=[out, out],
        name="fold_fourier",
    )(c64, s64, w_fourier, w_out)


def _proj_kernel(x_ref, sh_ref, sc_ref, g_ref, w_ref, *rest, rope, ctx_row):
    if rope:
        cos_ref, sin_ref, q_ref, k_ref, v_ref, u_ref, u_scratch = rest
    else:
        q_ref, k_ref, v_ref, u_ref = rest
    gain = (1.0 + _mod_row(sc_ref, ctx_row)) * g_ref[...]
    shift = _mod_row(sh_ref, ctx_row)
    k_lo = ATTN_WIDTH
    v_lo = ATTN_WIDTH + KV_WIDTH
    u_lo = ATTN_WIDTH + 2 * KV_WIDTH
    tm = x_ref.shape[1]
    per = min(PROJ_SLICE_ROWS, tm)
    assert tm % per == 0 and per % DFT_C == 0
    if rope:
        even_lane = (lax.broadcasted_iota(jnp.int32, (per, LANES), 1) & 1) == 0

    def norm(rows):
        return (_rms(x_ref[0, rows, :], gain) + shift).astype(BF16)

    def emit(rows, p):
        if rope:
            cos = cos_ref[rows, :]
            sin = sin_ref[rows, :]

            def rot(xs):
                partner = jnp.where(even_lane, pltpu.roll(xs, LANES - 1, 1), pltpu.roll(xs, 1, 1))
                return xs * cos + partner * sin

            for j in range(ATTN_WIDTH // LANES):
                q_ref[0, rows, j * LANES:(j + 1) * LANES] = rot(p[:, j * LANES:(j + 1) * LANES]).astype(BF16)
            k_ref[0, rows, :] = rot(p[:, k_lo:v_lo]).astype(BF16)
        else:
            q_ref[0, rows, :] = p[:, :k_lo].astype(BF16)
            k_ref[0, rows, :] = p[:, k_lo:v_lo].astype(BF16)
        vt = p[:, v_lo:u_lo].T.astype(BF16)
        ones = jnp.ones((VT_HEAD_ROWS - HEAD_DIM, per), BF16)
        for h_kv in range(N_KV_HEADS):
            v_ref[0, h_kv * VT_HEAD_ROWS:h_kv * VT_HEAD_ROWS + HEAD_DIM, rows] = (
                vt[h_kv * HEAD_DIM:(h_kv + 1) * HEAD_DIM])
            v_ref[0, h_kv * VT_HEAD_ROWS + HEAD_DIM:(h_kv + 1) * VT_HEAD_ROWS, rows] = ones
        if rope:
            n1 = slice(rows.start // DFT_C, rows.stop // DFT_C)
            for c in range(u_scratch.shape[0]):
                u_scratch[c, rows, :] = p[:, u_lo + c * LANES:u_lo + (c + 1) * LANES]
                for n2 in range(DFT_C):
                    u_ref[0, n2, n1, c * LANES:(c + 1) * LANES] = (
                        u_scratch[c, pl.ds(rows.start + n2, per // DFT_C, stride=DFT_C), :].astype(BF16))
        else:
            u_ref[0, rows, :] = p[:, u_lo:].astype(BF16)

    slices = [slice(r, r + per) for r in range(0, tm, per)]
    h = norm(slices[0])
    pending = None
    for i, rows in enumerate(slices):
        p = jnp.dot(h, w_ref[...], preferred_element_type=F32)
        if i + 1 < len(slices):
            h = norm(slices[i + 1])
        if pending is not None:
            emit(*pending)
        pending = (rows, p)
    emit(*pending)


def _project(layer, x, mod, gains, w_in, rope_tables, ctx_row=None):
    bx, t, d = x.shape
    tm = min(PROJ_TILE, t)
    n_in = w_in.shape[-1]
    rope = rope_tables is not None
    tok = lambda width: pl.BlockSpec((1, tm, width), lambda b, i: (b, i, 0))
    in_specs = [tok(d), _mod_spec(layer, 0, d), _mod_spec(layer, 1, d),
                _layer_resident(layer, (1, d)), _layer_resident(layer, (d, n_in))]
    args = [x, mod, mod, gains, w_in]
    if rope:
        in_specs += [pl.BlockSpec((tm, LANES), lambda b, i: (i, 0))] * 2
        args += list(rope_tables)
    widths = (ATTN_WIDTH, KV_WIDTH, KV_WIDTH, FOURIER_WIDTH)
    out_specs = [tok(w) for w in widths]
    out_shape = [jax.ShapeDtypeStruct((bx, t, w), BF16) for w in widths]
    out_specs[2] = pl.BlockSpec((1, VT_ROWS, tm), lambda b, i: (b, 0, i))
    out_shape[2] = jax.ShapeDtypeStruct((bx, VT_ROWS, t), BF16)
    scratch = []
    if rope:
        assert t == DFT_A * DFT_C and tm % DFT_C == 0
        out_specs[3] = pl.BlockSpec((1, DFT_C, tm // DFT_C, FOURIER_WIDTH), lambda b, i: (b, 0, i, 0))
        out_shape[3] = jax.ShapeDtypeStruct((bx, DFT_C, DFT_A, FOURIER_WIDTH), BF16)
        scratch = [pltpu.VMEM((FOURIER_WIDTH // LANES, tm, LANES), F32)]
    return pl.pallas_call(
        functools.partial(_proj_kernel, rope=rope, ctx_row=ctx_row),
        grid=(bx, t // tm),
        in_specs=in_specs,
        out_specs=out_specs,
        out_shape=out_shape,
        scratch_shapes=scratch,
        compiler_params=pltpu.CompilerParams(vmem_limit_bytes=VMEM_LIMIT_BYTES),
        name="project_rope" if rope else "project",
    )(*args)


def _attn_kernel(sink_ref, q_ref, *rest, local):
    w = WINDOW
    if local:
        kp_ref, kc_ref, kn_ref, vp_ref, vc_ref, vn_ref, kx_ref, vx_ref, o_ref = rest
        k_all = jnp.concatenate([kp_ref[0], kc_ref[0], kn_ref[0]], axis=0)
        vt_all = jnp.concatenate([vp_ref[0], vc_ref[0], vn_ref[0]], axis=1)
        step = pl.program_id(1)
        last = pl.num_programs(1) - 1
    else:
        kx_ref, vx_ref, o_ref = rest
    kx = kx_ref[0]
    vxt = vx_ref[0]
    n_sub = q_ref.shape[1] // w
    cols = Q_PER_KV * w
    group = lax.shift_right_logical(lax.broadcasted_iota(jnp.int32, (1, cols), 1), int(math.log2(w)))
    if local:
        kj = lax.broadcasted_iota(jnp.int32, (w, cols), 0)
        qi = lax.broadcasted_iota(jnp.int32, (w, cols), 1) & (w - 1)
        far = 1 << 20
        bias_prev = jnp.where(kj >= qi, 0.0, NEG_INF)
        bias_next = jnp.where(kj <= qi, 0.0, NEG_INF)
        bias_prev_first = jnp.where(kj >= qi + jnp.where(step > 0, 0, far), 0.0, NEG_INF)
        bias_next_last = jnp.where(kj <= qi - jnp.where(step < last, 0, far), 0.0, NEG_INF)
    contract_last = (((1,), (1,)), ((), ()))

    def sink_row(h):
        sink = jnp.full((1, cols), sink_ref[Q_PER_KV * h] * LOG2_E, F32)
        for g in range(1, Q_PER_KV):
            sink = jnp.where(group == g, sink_ref[Q_PER_KV * h + g] * LOG2_E, sink)
        return sink

    sinks = [sink_row(h) for h in range(N_KV_HEADS)]

    def scores(h, s):
        hs = slice(h * HEAD_DIM, (h + 1) * HEAD_DIM)
        q = q_ref[0, s * w:(s + 1) * w, :]
        qs = jnp.concatenate([q[:, (Q_PER_KV * h + g) * HEAD_DIM:(Q_PER_KV * h + g + 1) * HEAD_DIM]
                              for g in range(Q_PER_KV)], axis=0)
        pieces = []
        if local:
            sl = lax.dot_general(k_all[s * w:(s + 3) * w, hs], qs, contract_last, preferred_element_type=F32)
            pieces += [sl[:w] + (bias_prev_first if s == 0 else bias_prev), sl[w:2 * w],
                       sl[2 * w:] + (bias_next_last if s == n_sub - 1 else bias_next)]
        pieces.append(lax.dot_general(kx[:, hs], qs, contract_last, preferred_element_type=F32))
        m = sinks[h]
        for x in pieces:
            m = jnp.maximum(m, jnp.max(x, axis=0, keepdims=True))
        return pieces, m

    def finish(h, s, pieces, m):
        vs = slice(h * VT_HEAD_ROWS, (h + 1) * VT_HEAD_ROWS)
        probs = [jnp.exp2(x - m).astype(BF16) for x in pieces]
        ot = jnp.dot(vxt[vs, :], probs[-1], preferred_element_type=F32)
        if local:
            ot = ot + jnp.dot(vt_all[vs, s * w:(s + 3) * w], jnp.concatenate(probs[:3], axis=0),
                              preferred_element_type=F32)
        denom = ot[HEAD_DIM:HEAD_DIM + 1] + jnp.exp2(sinks[h] - m)
        ot = ot[:HEAD_DIM] / denom
        o = jnp.concatenate([ot[:, g * w:(g + 1) * w].T for g in range(Q_PER_KV)], axis=1)
        lo = h * Q_PER_KV * HEAD_DIM
        o_ref[0, s * w:(s + 1) * w, lo:lo + Q_PER_KV * HEAD_DIM] = o.astype(BF16)

    work = [(h, s) for h in range(N_KV_HEADS) for s in range(n_sub)]
    ready = scores(*work[0])
    for i, (h, s) in enumerate(work):
        upcoming = scores(*work[i + 1]) if i + 1 < len(work) else None
        finish(h, s, *ready)
        ready = upcoming


def _attention(q, k, v, kx, vx, sink, local):
    bx, t, _ = q.shape
    cx = kx.shape[1]
    smem = pl.BlockSpec(memory_space=pltpu.SMEM)
    kx_spec = pl.BlockSpec((1, cx, KV_WIDTH), lambda b, n: (b, 0, 0))
    vx_spec = pl.BlockSpec((1, VT_ROWS, cx), lambda b, n: (b, 0, 0))
    tq = min(ATTN_Q_TILE, t)
    nb = t // tq
    q_spec = pl.BlockSpec((1, tq, ATTN_WIDTH), lambda b, n: (b, n, 0))
    if local:
        per = tq // WINDOW
        n_win = t // WINDOW
        before = lambda n: jnp.maximum(n * per - 1, 0)
        after = lambda n: jnp.minimum((n + 1) * per, n_win - 1)
        k_specs = [pl.BlockSpec((1, WINDOW, KV_WIDTH), lambda b, n: (b, before(n), 0)),
                   pl.BlockSpec((1, tq, KV_WIDTH), lambda b, n: (b, n, 0)),
                   pl.BlockSpec((1, WINDOW, KV_WIDTH), lambda b, n: (b, after(n), 0))]
        v_specs = [pl.BlockSpec((1, VT_ROWS, WINDOW), lambda b, n: (b, 0, before(n))),
                   pl.BlockSpec((1, VT_ROWS, tq), lambda b, n: (b, 0, n)),
                   pl.BlockSpec((1, VT_ROWS, WINDOW), lambda b, n: (b, 0, after(n)))]
        in_specs = [smem, q_spec] + k_specs + v_specs + [kx_spec, vx_spec]
        args = (sink, q, k, k, k, v, v, v, kx, vx)
    else:
        in_specs = [smem, q_spec, kx_spec, vx_spec]
        args = (sink, q, kx, vx)
    return pl.pallas_call(
        functools.partial(_attn_kernel, local=local),
        grid=(bx, nb),
        in_specs=in_specs,
        out_specs=pl.BlockSpec((1, tq, ATTN_WIDTH), lambda b, n: (b, n, 0)),
        out_shape=jax.ShapeDtypeStruct((bx, t, ATTN_WIDTH), BF16),
        compiler_params=pltpu.CompilerParams(vmem_limit_bytes=VMEM_LIMIT_BYTES),
        name="attn_window" if local else "attn_context",
    )(*args)


def _dft_short_kernel(fa_ref, u_ref, ur_ref, ui_ref):
    y = jnp.dot(fa_ref[...].astype(BF16), u_ref[0], preferred_element_type=F32)
    ur_ref[0] = y[:DFT_A].astype(BF16)
    ui_ref[0] = y[DFT_A:].astype(BF16)


def _dft_long_kernel(fa_ref, g_ref, tr_ref, ti_ref, u_ref, ur_ref, ui_ref, yr_s, yi_s):
    fa = fa_ref[...].astype(BF16)
    for n2 in range(DFT_C):
        y = jnp.dot(fa, u_ref[0, n2], preferred_element_type=F32)
        tr = tr_ref[n2]
        ti = ti_ref[n2]
        for c in range(y.shape[1] // LANES):
            sl = slice(c * LANES, (c + 1) * LANES)
            yr = y[:DFT_A, sl]
            yi = y[DFT_A:, sl]
            yr_s[n2, :, sl] = (yr * tr - yi * ti).astype(BF16)
            yi_s[n2, :, sl] = (yr * ti + yi * tr).astype(BF16)
    g = g_ref[...].astype(BF16)
    half = DFT_C * DFT_C
    for j in range(DFT_A // DFT_C):
        sl = slice(j * DFT_C, (j + 1) * DFT_C)
        y = jnp.concatenate([yr_s[n2, sl, :] for n2 in range(DFT_C)]
                            + [yi_s[n2, sl, :] for n2 in range(DFT_C)], axis=0)
        r = jnp.dot(g, y, preferred_element_type=F32).astype(BF16)
        for k2 in range(DFT_C):
            ur_ref[0, k2, sl, :] = r[k2 * DFT_C:(k2 + 1) * DFT_C]
            ui_ref[0, k2, sl, :] = r[half + k2 * DFT_C:half + (k2 + 1) * DFT_C]


def _dft_tables(n_pos):
    a = np.arange(DFT_A)
    ang = 2.0 * np.pi * ((a[:, None] * a[None, :]) % DFT_A) / DFT_A
    norm = 1.0 / math.sqrt(n_pos * FOURIER_GROUP_DIM)
    fa = jnp.asarray(np.concatenate([np.cos(ang), -np.sin(ang)], axis=0) * norm, F32)
    if n_pos == DFT_A:
        return fa, None, None, None
    assert n_pos == DFT_A * DFT_C
    n2 = np.arange(DFT_C)
    tw = 2.0 * np.pi * (n2[:, None] * a[None, :]) / n_pos
    tr = jnp.asarray(np.repeat(np.cos(tw)[:, :, None], LANES, axis=2), F32)
    ti = jnp.asarray(np.repeat(-np.sin(tw)[:, :, None], LANES, axis=2), F32)
    ang16 = 2.0 * np.pi * ((n2[:, None] * n2[None, :]) % DFT_C) / DFT_C
    c16, s16 = np.cos(ang16), np.sin(ang16)
    eye = np.eye(DFT_C)
    kron = lambda f: np.einsum("kn,ab->kanb", f, eye).reshape(DFT_C * DFT_C, DFT_C * DFT_C)
    gmat = np.block([[kron(c16), kron(s16)], [kron(-s16), kron(c16)]])
    return fa, tr, ti, jnp.asarray(gmat, F32)


def _position_dft(u):
    bx, w = u.shape[0], u.shape[-1]
    n_pos = math.prod(u.shape[1:-1])
    fa, tr, ti, gmat = _dft_tables(n_pos)
    fa_spec = _resident((2 * DFT_A, DFT_A))
    if n_pos == DFT_A:
        blk = pl.BlockSpec((1, DFT_A, w), lambda b: (b, 0, 0))
        out = jax.ShapeDtypeStruct((bx, DFT_A, w), BF16)
        return pl.pallas_call(
            _dft_short_kernel,
            grid=(bx,), in_specs=[fa_spec, blk], out_specs=[blk, blk], out_shape=[out, out],
            name="dft_context",
        )(fa, u)
    rows = DFT_C * DFT_C
    blk = pl.BlockSpec((1, DFT_C, DFT_A, w), lambda b: (b, 0, 0, 0))
    out = jax.ShapeDtypeStruct((bx, DFT_C, DFT_A, w), BF16)
    ur, ui = pl.pallas_call(
        _dft_long_kernel,
        grid=(bx,),
        in_specs=[fa_spec, _resident((2 * rows, 2 * rows)), _resident(tr.shape), _resident(ti.shape), blk],
        out_specs=[blk, blk], out_shape=[out, out],
        scratch_shapes=[pltpu.VMEM((DFT_C, DFT_A, w), BF16)] * 2,
        compiler_params=pltpu.CompilerParams(vmem_limit_bytes=VMEM_LIMIT_BYTES),
        name="dft_long",
    )(fa, gmat, tr, ti, u)
    return ur.reshape(bx, n_pos, w), ui.reshape(bx, n_pos, w)


def _tail_kernel(a_ref, ur_ref, ui_ref, x_ref, gate1_ref, sh_ref, sc_ref, gate2_ref,
                 gpost_mix_ref, gpre_ref, gpost_ref, wa_ref, wr_ref, wi_ref, wg_ref, wu_ref, wd_ref, o_ref,
                 *, ctx_row):
    gate1, shift, scale, gate2 = (_mod_row(r, ctx_row) for r in (gate1_ref, sh_ref, sc_ref, gate2_ref))
    gain_mix = gate1 * gpost_mix_ref[...]
    gain_in = (1.0 + scale) * gpre_ref[...]
    gain_out = gate2 * gpost_ref[...]
    hidden = wg_ref.shape[1]
    tm = x_ref.shape[1]
    per = min(TAIL_SLICE_ROWS, tm)
    assert tm % per == 0

    def mix_dots(rows):
        mix = jnp.dot(a_ref[0, rows, :], wa_ref[...], preferred_element_type=F32)
        mix += jnp.dot(ur_ref[0, rows, :], wr_ref[...], preferred_element_type=F32)
        mix += jnp.dot(ui_ref[0, rows, :], wi_ref[...], preferred_element_type=F32)
        return mix

    def norms(rows, mix):
        x_mid = x_ref[0, rows, :] + _rms(mix, gain_mix)
        return x_mid, (_rms(x_mid, gain_in) + shift).astype(BF16)

    def ffn(h):
        acc = jnp.zeros((h.shape[0], o_ref.shape[2]), F32)
        for lo in range(0, hidden, FFN_CHUNK):
            hi = min(lo + FFN_CHUNK, hidden)
            g = jnp.dot(h, wg_ref[:, lo:hi], preferred_element_type=F32)
            u = jnp.dot(h, wu_ref[:, lo:hi], preferred_element_type=F32)
            a = (g * jax.nn.sigmoid(g) * u).astype(BF16)
            acc += jnp.dot(a, wd_ref[lo:hi, :], preferred_element_type=F32)
        return acc

    def epilogue(rows, x_mid, acc):
        o_ref[0, rows, :] = x_mid + _rms(acc, gain_out)

    slices = [slice(r, r + per) for r in range(0, tm, per)]
    mixes = [mix_dots(rows) for rows in slices]
    staged = norms(slices[0], mixes[0])
    pending = None
    for i, rows in enumerate(slices):
        x_mid, h = staged
        acc = ffn(h)
        if i + 1 < len(slices):
            staged = norms(slices[i + 1], mixes[i + 1])
        if pending is not None:
            epilogue(*pending)
        pending = (rows, x_mid, acc)
    epilogue(*pending)


def _tail(layer, attn, ur, ui, x, mod, g_post_mix, g_pre_ffn, g_post_ffn, w_out, w_four_r, w_four_i,
          wg, wu, wd, ctx_row=None):
    bx, t, d = x.shape
    tm = min(TAIL_TILE, t)
    hidden = wg.shape[-1]
    tok = lambda width: pl.BlockSpec((1, tm, width), lambda b, i: (b, i, 0))
    gain = _layer_resident(layer, (1, d))
    wfour = _layer_resident(layer, (FOURIER_WIDTH, d))
    return pl.pallas_call(
        functools.partial(_tail_kernel, ctx_row=ctx_row),
        grid=(bx, t // tm),
        in_specs=[tok(ATTN_WIDTH), tok(FOURIER_WIDTH), tok(FOURIER_WIDTH), tok(d),
                  _mod_spec(layer, 2, d), _mod_spec(layer, 3, d), _mod_spec(layer, 4, d), _mod_spec(layer, 5, d),
                  gain, gain, gain,
                  _layer_resident(layer, (ATTN_WIDTH, d)), wfour, wfour,
                  _layer_resident(layer, (d, hidden)), _layer_resident(layer, (d, hidden)),
                  _layer_resident(layer, (hidden, d))],
        out_specs=tok(d),
        out_shape=jax.ShapeDtypeStruct((bx, t, d), F32),
        compiler_params=pltpu.CompilerParams(vmem_limit_bytes=VMEM_LIMIT_BYTES),
        name="tail",
    )(attn, ur, ui, x, mod, mod, mod, mod, g_post_mix, g_pre_ffn, g_post_ffn,
      w_out, w_four_r, w_four_i, wg, wu, wd)


def _rope_tables(seq):
    t = np.arange(seq)
    freqs = ROPE_THETA ** (-np.arange(ROPE_PAIRS_PER_AXIS, dtype=np.float64) / ROPE_PAIRS_PER_AXIS)
    ang = np.concatenate([(t // GRID_W)[:, None] * freqs, (t % GRID_W)[:, None] * freqs], axis=-1)
    cos = np.repeat(np.cos(ang), 2, axis=-1)
    sin = np.repeat(np.sin(ang), 2, axis=-1) * np.tile([-1.0, 1.0], HEAD_DIM // 2)
    heads_per_row = LANES // HEAD_DIM
    return (jnp.asarray(np.tile(cos, (1, heads_per_row)), F32), jnp.asarray(np.tile(sin, (1, heads_per_row)), F32))


def kernel(x, c, ctx, c_ctx, w_ada, b_ada, norm_pre_mix, norm_post_mix, norm_pre_ffn, norm_post_ffn,
           w_in, w_out, w_fourier, sink, w_gate, w_up, w_down):
    batch, seq, d = x.shape
    depth = w_ada.shape[0]
    assert batch + 1 <= MOD_ROWS

    cc = jnp.zeros((MOD_ROWS, d), F32).at[:batch].set(c).at[batch].set(c_ctx)
    mod = _ada(cc, w_ada, b_ada)
    w_four_r, w_four_i = _fold_fourier_weights(w_fourier, w_out)
    rope = _rope_tables(seq)

    q_scale = np.ones((w_in.shape[-1],), np.float32)
    q_scale[:ATTN_WIDTH] = HEAD_DIM ** -0.5 * LOG2_E
    w_in_b = (w_in * q_scale).astype(BF16)
    w_out_b = w_out.astype(BF16)
    wg, wu, wd = w_gate.astype(BF16), w_up.astype(BF16), w_down.astype(BF16)
    gains = [g.reshape(depth, 1, d) for g in (norm_pre_mix, norm_post_mix, norm_pre_ffn, norm_post_ffn)]
    g_pre_mix, g_post_mix, g_pre_ffn, g_post_ffn = gains

    xc = ctx
    for i in range(depth):
        def finish(stream, attn, ur, ui, ctx_row):
            return _tail(i, attn, ur, ui, stream, mod, g_post_mix, g_pre_ffn, g_post_ffn,
                         w_out_b, w_four_r, w_four_i, wg, wu, wd, ctx_row)

        q, k, v, u = _project(i, x, mod, g_pre_mix, w_in_b, rope)
        qc, kc, vc, uc = _project(i, xc, mod, g_pre_mix, w_in_b, None, ctx_row=batch)
        attn = _attention(q, k, v, kc, vc, sink[i], local=True)
        ur, ui = _position_dft(u)
        x = finish(x, attn, ur, ui, None)

        if i < depth - 1:
            attn_c = _attention(qc, None, None, kc, vc, sink[i], local=False)
            urc, uic = _position_dft(uc)
            xc = finish(xc, attn_c, urc, uic, batch)
    return x
```

```python
import functools
import math

import numpy as np
import jax
import jax.numpy as jnp
from jax import lax
from jax.experimental import pallas as pl
from jax.experimental.pallas import tpu as pltpu

F32 = jnp.float32
BF16 = jnp.bfloat16

GRID_W = 64
HEAD_DIM = 64
N_Q_HEADS = 8
N_KV_HEADS = 2
Q_PER_KV = N_Q_HEADS // N_KV_HEADS
ATTN_WIDTH = N_Q_HEADS * HEAD_DIM
KV_WIDTH = N_KV_HEADS * HEAD_DIM
N_FOURIER_GROUPS = 8
FOURIER_GROUP_DIM = 64
FOURIER_WIDTH = N_FOURIER_GROUPS * FOURIER_GROUP_DIM
WINDOW = 128
ROPE_THETA = 10000.0
ROPE_PAIRS_PER_AXIS = HEAD_DIM // 4
RMS_EPS = 1e-6
N_MOD = 6
NEG_INF = -1e30
LOG2_E = math.log2(math.e)

LANES = 128
BF16_SUBLANES = 16
VMEM_LIMIT_BYTES = 60 * 1024 * 1024

DFT_A = 256
DFT_C = 16

MOD_ROWS = 8
PROJ_TILE = 1024
PROJ_SLICE_ROWS = 256
FFN_CHUNK = 512
TAIL_TILE = 1024
TAIL_SLICE_ROWS = 512
ATTN_Q_TILE = 1024
VT_HEAD_ROWS = HEAD_DIM + BF16_SUBLANES
VT_ROWS = N_KV_HEADS * VT_HEAD_ROWS


def _resident(shape):
    zeros = (0,) * len(shape)
    return pl.BlockSpec(shape, lambda *_: zeros, pipeline_mode=pl.Buffered(1))


def _layer_resident(layer, shape, index=None):
    index = (0,) * len(shape) if index is None else index
    return pl.BlockSpec((None,) + tuple(shape), lambda *_: (layer,) + tuple(index), pipeline_mode=pl.Buffered(1))


def _mod_spec(layer, which, d):
    return pl.BlockSpec((None, MOD_ROWS, d), lambda *_: (layer, 0, which))


def _mod_row(ref, ctx_row):
    row = pl.program_id(0) if ctx_row is None else ctx_row
    return ref[pl.ds(row, 1), :]


def _rms(xf, g):
    ms = jnp.mean(xf * xf, axis=-1, keepdims=True)
    return xf * lax.rsqrt(ms + RMS_EPS) * g


def _split_bf16(x):
    hi = x.astype(BF16)
    return hi, (x - hi.astype(F32)).astype(BF16)


def _dot3(a, b):
    (a_hi, a_lo), (b_hi, b_lo) = a, b
    dot = functools.partial(jnp.dot, preferred_element_type=F32)
    return dot(a_hi, b_hi) + dot(a_lo, b_hi) + dot(a_hi, b_lo)


def _ada_kernel(c_ref, w_ref, b_ref, o_ref):
    c = c_ref[...]
    a = c * jax.nn.sigmoid(c)
    a_hi, a_lo = _split_bf16(a)
    w_hi, w_lo = _split_bf16(w_ref[0])
    rows = a.shape[0]
    r = jnp.dot(jnp.concatenate([a_hi, a_lo], axis=0), w_hi, preferred_element_type=F32)
    o_ref[0] = r[:rows] + r[rows:] + jnp.dot(a_hi, w_lo, preferred_element_type=F32) + b_ref[0]


def _ada(cc, w_ada, b_ada):
    depth, d, n = w_ada.shape
    tn = 1536
    return pl.pallas_call(
        _ada_kernel,
        grid=(depth, n // tn),
        in_specs=[
            pl.BlockSpec((MOD_ROWS, d), lambda i, j: (0, 0)),
            pl.BlockSpec((1, d, tn), lambda i, j: (i, 0, j)),
            pl.BlockSpec((1, 1, tn), lambda i, j: (i, 0, j)),
        ],
        out_specs=pl.BlockSpec((1, MOD_ROWS, tn), lambda i, j: (i, 0, j)),
        out_shape=jax.ShapeDtypeStruct((depth, MOD_ROWS, n), F32),
        compiler_params=pltpu.CompilerParams(vmem_limit_bytes=VMEM_LIMIT_BYTES),
        name="ada",
    )(cc, w_ada, b_ada.reshape(depth, 1, n))


def _fold_kernel(c64_ref, s64_ref, wf_ref, wo_ref, wr_ref, wi_ref):
    gd = c64_ref.shape[0]
    for g in range(wf_ref.shape[0]):
        rows = slice(g * gd, (g + 1) * gd)
        t = _split_bf16(_dot3(_split_bf16(wf_ref[g]), _split_bf16(wo_ref[rows, :])))
        wr_ref[rows, :] = _dot3(_split_bf16(c64_ref[...]), t).astype(BF16)
        wi_ref[rows, :] = _dot3(_split_bf16(s64_ref[...]), t).astype(BF16)


def _fold_fourier_weights(w_fourier, w_out):
    depth, groups, gd, _ = w_fourier.shape
    d = w_out.shape[-1]
    width = groups * gd
    assert w_out.shape[1] == ATTN_WIDTH + width and ATTN_WIDTH == width
    j = np.arange(gd)
    ang = 2.0 * np.pi * ((j[:, None] * j[None, :]) % gd) / gd
    c64 = jnp.asarray(np.cos(ang), F32)
    s64 = jnp.asarray(np.sin(ang), F32)
    out = jax.ShapeDtypeStruct((depth, width, d), BF16)
    return pl.pallas_call(
        _fold_kernel,
        grid=(depth,),
        in_specs=[
            pl.BlockSpec((gd, gd), lambda i: (0, 0)),
            pl.BlockSpec((gd, gd), lambda i: (0, 0)),
            pl.BlockSpec((None, groups, gd, gd), lambda i: (i, 0, 0, 0)),
            pl.BlockSpec((None, width, d), lambda i: (i, 1, 0)),
        ],
        out_specs=[pl.BlockSpec((None, width, d), lambda i: (i, 0, 0))] * 2,
        out_shape=[out, out],
        name="fold_fourier",
    )(c64, s64, w_fourier, w_out)


def _proj_kernel(x_ref, sh_ref, sc_ref, g_ref, w_ref, *rest, rope, ctx_row):
    if rope:
        cos_ref, sin_ref, q_ref, k_ref, v_ref, u_ref, u_scratch = rest
    else:
        q_ref, k_ref, v_ref, u_ref = rest
    gain = (1.0 + _mod_row(sc_ref, ctx_row)) * g_ref[...]
    shift = _mod_row(sh_ref, ctx_row)
    k_lo = ATTN_WIDTH
    v_lo = ATTN_WIDTH + KV_WIDTH
    u_lo = ATTN_WIDTH + 2 * KV_WIDTH
    tm = x_ref.shape[1]
    per = min(PROJ_SLICE_ROWS, tm)
    assert tm % per == 0 and per % DFT_C == 0
    if rope:
        even_lane = (lax.broadcasted_iota(jnp.int32, (per, LANES), 1) & 1) == 0

    def norm(rows):
        return (_rms(x_ref[0, rows, :], gain) + shift).astype(BF16)

    def emit(rows, p):
        if rope:
            cos = cos_ref[rows, :]
            sin = sin_ref[rows, :]

            def rot(xs):
                partner = jnp.where(even_lane, pltpu.roll(xs, LANES - 1, 1), pltpu.roll(xs, 1, 1))
                return xs * cos + partner * sin

            for j in range(ATTN_WIDTH // LANES):
                q_ref[0, rows, j * LANES:(j + 1) * LANES] = rot(p[:, j * LANES:(j + 1) * LANES]).astype(BF16)
            k_ref[0, rows, :] = rot(p[:, k_lo:v_lo]).astype(BF16)
        else:
            q_ref[0, rows, :] = p[:, :k_lo].astype(BF16)
            k_ref[0, rows, :] = p[:, k_lo:v_lo].astype(BF16)
        vt = p[:, v_lo:u_lo].T.astype(BF16)
        ones = jnp.ones((VT_HEAD_ROWS - HEAD_DIM, per), BF16)
        for h_kv in range(N_KV_HEADS):
            v_ref[0, h_kv * VT_HEAD_ROWS:h_kv * VT_HEAD_ROWS + HEAD_DIM, rows] = (
                vt[h_kv * HEAD_DIM:(h_kv + 1) * HEAD_DIM])
            v_ref[0, h_kv * VT_HEAD_ROWS + HEAD_DIM:(h_kv + 1) * VT_HEAD_ROWS, rows] = ones
        if rope:
            n1 = slice(rows.start // DFT_C, rows.stop // DFT_C)
            for c in range(u_scratch.shape[0]):
                u_scratch[c, rows, :] = p[:, u_lo + c * LANES:u_lo + (c + 1) * LANES]
                for n2 in range(DFT_C):
                    u_ref[0, n2, n1, c * LANES:(c + 1) * LANES] = (
                        u_scratch[c, pl.ds(rows.start + n2, per // DFT_C, stride=DFT_C), :].astype(BF16))
        else:
            u_ref[0, rows, :] = p[:, u_lo:].astype(BF16)

    slices = [slice(r, r + per) for r in range(0, tm, per)]
    h = norm(slices[0])
    pending = None
    for i, rows in enumerate(slices):
        p = jnp.dot(h, w_ref[...], preferred_element_type=F32)
        if i + 1 < len(slices):
            h = norm(slices[i + 1])
        if pending is not None:
            emit(*pending)
        pending = (rows, p)
    emit(*pending)


def _project(layer, x, mod, gains, w_in, rope_tables, ctx_row=None):
    bx, t, d = x.shape
    tm = min(PROJ_TILE, t)
    n_in = w_in.shape[-1]
    rope = rope_tables is not None
    tok = lambda width: pl.BlockSpec((1, tm, width), lambda b, i: (b, i, 0))
    in_specs = [tok(d), _mod_spec(layer, 0, d), _mod_spec(layer, 1, d),
                _layer_resident(layer, (1, d)), _layer_resident(layer, (d, n_in))]
    args = [x, mod, mod, gains, w_in]
    if rope:
        in_specs += [pl.BlockSpec((tm, LANES), lambda b, i: (i, 0))] * 2
        args += list(rope_tables)
    widths = (ATTN_WIDTH, KV_WIDTH, KV_WIDTH, FOURIER_WIDTH)
    out_specs = [tok(w) for w in widths]
    out_shape = [jax.ShapeDtypeStruct((bx, t, w), BF16) for w in widths]
    out_specs[2] = pl.BlockSpec((1, VT_ROWS, tm), lambda b, i: (b, 0, i))
    out_shape[2] = jax.ShapeDtypeStruct((bx, VT_ROWS, t), BF16)
    scratch = []
    if rope:
        assert t == DFT_A * DFT_C and tm % DFT_C == 0
        out_specs[3] = pl.BlockSpec((1, DFT_C, tm // DFT_C, FOURIER_WIDTH), lambda b, i: (b, 0, i, 0))
        out_shape[3] = jax.ShapeDtypeStruct((bx, DFT_C, DFT_A, FOURIER_WIDTH), BF16)
        scratch = [pltpu.VMEM((FOURIER_WIDTH // LANES, tm, LANES), F32)]
    return pl.pallas_call(
        functools.partial(_proj_kernel, rope=rope, ctx_row=ctx_row),
        grid=(bx, t // tm),
        in_specs=in_specs,
        out_specs=out_specs,
        out_shape=out_shape,
        scratch_shapes=scratch,
        compiler_params=pltpu.CompilerParams(vmem_limit_bytes=VMEM_LIMIT_BYTES),
        name="project_rope" if rope else "project",
    )(*args)


def _attn_kernel(sink_ref, q_ref, *rest, local, n_cast):
    w = WINDOW
    n_in = len(rest) - 1 - 2 * n_cast
    cast_src = rest[n_in:n_in + n_cast]
    cast_dst = rest[n_in + n_cast + 1:]
    rest = rest[:n_in] + rest[n_in + n_cast:n_in + n_cast + 1]
    for src, dst in zip(cast_src, cast_dst):
        dst[...] = src[...].astype(BF16)
    if local:
        kp_ref, kc_ref, kn_ref, vp_ref, vc_ref, vn_ref, kx_ref, vx_ref, o_ref = rest
        k_all = jnp.concatenate([kp_ref[0], kc_ref[0], kn_ref[0]], axis=0)
        vt_all = jnp.concatenate([vp_ref[0], vc_ref[0], vn_ref[0]], axis=1)
        step = pl.program_id(1)
        last = pl.num_programs(1) - 1
    else:
        kx_ref, vx_ref, o_ref = rest
    kx = kx_ref[0]
    vxt = vx_ref[0]
    n_sub = q_ref.shape[1] // w
    cols = Q_PER_KV * w
    group = lax.shift_right_logical(lax.broadcasted_iota(jnp.int32, (1, cols), 1), int(math.log2(w)))
    if local:
        kj = lax.broadcasted_iota(jnp.int32, (w, cols), 0)
        qi = lax.broadcasted_iota(jnp.int32, (w, cols), 1) & (w - 1)
        far = 1 << 20
        bias_prev = jnp.where(kj >= qi, 0.0, NEG_INF)
        bias_next = jnp.where(kj <= qi, 0.0, NEG_INF)
        bias_prev_first = jnp.where(kj >= qi + jnp.where(step > 0, 0, far), 0.0, NEG_INF)
        bias_next_last = jnp.where(kj <= qi - jnp.where(step < last, 0, far), 0.0, NEG_INF)
    contract_last = (((1,), (1,)), ((), ()))

    def sink_row(h):
        sink = jnp.full((1, cols), sink_ref[Q_PER_KV * h] * LOG2_E, F32)
        for g in range(1, Q_PER_KV):
            sink = jnp.where(group == g, sink_ref[Q_PER_KV * h + g] * LOG2_E, sink)
        return sink

    sinks = [sink_row(h) for h in range(N_KV_HEADS)]

    def scores(h, s):
        hs = slice(h * HEAD_DIM, (h + 1) * HEAD_DIM)
        q = q_ref[0, s * w:(s + 1) * w, :]
        qs = jnp.concatenate([q[:, (Q_PER_KV * h + g) * HEAD_DIM:(Q_PER_KV * h + g + 1) * HEAD_DIM]
                              for g in range(Q_PER_KV)], axis=0)
        pieces = []
        if local:
            sl = lax.dot_general(k_all[s * w:(s + 3) * w, hs], qs, contract_last, preferred_element_type=F32)
            pieces += [sl[:w] + (bias_prev_first if s == 0 else bias_prev), sl[w:2 * w],
                       sl[2 * w:] + (bias_next_last if s == n_sub - 1 else bias_next)]
        pieces.append(lax.dot_general(kx[:, hs], qs, contract_last, preferred_element_type=F32))
        m = sinks[h]
        for x in pieces:
            m = jnp.maximum(m, jnp.max(x, axis=0, keepdims=True))
        return pieces, m

    def finish(h, s, pieces, m):
        vs = slice(h * VT_HEAD_ROWS, (h + 1) * VT_HEAD_ROWS)
        probs = [jnp.exp2(x - m).astype(BF16) for x in pieces]
        ot = jnp.dot(vxt[vs, :], probs[-1], preferred_element_type=F32)
        if local:
            ot = ot + jnp.dot(vt_all[vs, s * w:(s + 3) * w], jnp.concatenate(probs[:3], axis=0),
                              preferred_element_type=F32)
        denom = ot[HEAD_DIM:HEAD_DIM + 1] + jnp.exp2(sinks[h] - m)
        ot = ot[:HEAD_DIM] / denom
        o = jnp.concatenate([ot[:, g * w:(g + 1) * w].T for g in range(Q_PER_KV)], axis=1)
        lo = h * Q_PER_KV * HEAD_DIM
        o_ref[0, s * w:(s + 1) * w, lo:lo + Q_PER_KV * HEAD_DIM] = o.astype(BF16)

    work = [(h, s) for h in range(N_KV_HEADS) for s in range(n_sub)]
    ready = scores(*work[0])
    for i, (h, s) in enumerate(work):
        upcoming = scores(*work[i + 1]) if i + 1 < len(work) else None
        finish(h, s, *ready)
        ready = upcoming


def _attention(q, k, v, kx, vx, sink, local, casts=()):
    bx, t, _ = q.shape
    cx = kx.shape[1]
    smem = pl.BlockSpec(memory_space=pltpu.SMEM)
    kx_spec = pl.BlockSpec((1, cx, KV_WIDTH), lambda b, n: (b, 0, 0))
    vx_spec = pl.BlockSpec((1, VT_ROWS, cx), lambda b, n: (b, 0, 0))
    tq = min(ATTN_Q_TILE, t)
    nb = t // tq
    q_spec = pl.BlockSpec((1, tq, ATTN_WIDTH), lambda b, n: (b, n, 0))
    if local:
        per = tq // WINDOW
        n_win = t // WINDOW
        before = lambda n: jnp.maximum(n * per - 1, 0)
        after = lambda n: jnp.minimum((n + 1) * per, n_win - 1)
        k_specs = [pl.BlockSpec((1, WINDOW, KV_WIDTH), lambda b, n: (b, before(n), 0)),
                   pl.BlockSpec((1, tq, KV_WIDTH), lambda b, n: (b, n, 0)),
                   pl.BlockSpec((1, WINDOW, KV_WIDTH), lambda b, n: (b, after(n), 0))]
        v_specs = [pl.BlockSpec((1, VT_ROWS, WINDOW), lambda b, n: (b, 0, before(n))),
                   pl.BlockSpec((1, VT_ROWS, tq), lambda b, n: (b, 0, n)),
                   pl.BlockSpec((1, VT_ROWS, WINDOW), lambda b, n: (b, 0, after(n)))]
        in_specs = [smem, q_spec] + k_specs + v_specs + [kx_spec, vx_spec]
        args = [sink, q, k, k, k, v, v, v, kx, vx]
    else:
        in_specs = [smem, q_spec, kx_spec, vx_spec]
        args = [sink, q, kx, vx]
    out_specs = [pl.BlockSpec((1, tq, ATTN_WIDTH), lambda b, n: (b, n, 0))]
    out_shape = [jax.ShapeDtypeStruct((bx, t, ATTN_WIDTH), BF16)]
    steps = bx * nb
    for w, layer, rows in casts:
        slab = rows // steps
        assert rows % steps == 0 and slab % BF16_SUBLANES == 0
        in_specs.append(pl.BlockSpec((None, slab, w.shape[-1]), lambda b, n, layer=layer: (layer, b * nb + n, 0)))
        out_specs.append(pl.BlockSpec((slab, w.shape[-1]), lambda b, n: (b * nb + n, 0)))
        out_shape.append(jax.ShapeDtypeStruct((rows, w.shape[-1]), BF16))
        args.append(w)
    outs = pl.pallas_call(
        functools.partial(_attn_kernel, local=local, n_cast=len(casts)),
        grid=(bx, nb),
        in_specs=in_specs,
        out_specs=out_specs,
        out_shape=out_shape,
        compiler_params=pltpu.CompilerParams(vmem_limit_bytes=VMEM_LIMIT_BYTES),
        name="attn_window" if local else "attn_context",
    )(*args)
    return outs[0], outs[1:]


def _dft_short_kernel(fa_ref, u_ref, ur_ref, ui_ref):
    y = jnp.dot(fa_ref[...].astype(BF16), u_ref[0], preferred_element_type=F32)
    ur_ref[0] = y[:DFT_A].astype(BF16)
    ui_ref[0] = y[DFT_A:].astype(BF16)


def _dft_long_kernel(fa_ref, g_ref, tr_ref, ti_ref, u_ref, ur_ref, ui_ref, yr_s, yi_s):
    fa = fa_ref[...].astype(BF16)
    for n2 in range(DFT_C):
        y = jnp.dot(fa, u_ref[0, n2], preferred_element_type=F32)
        tr = tr_ref[n2]
        ti = ti_ref[n2]
        for c in range(y.shape[1] // LANES):
            sl = slice(c * LANES, (c + 1) * LANES)
            yr = y[:DFT_A, sl]
            yi = y[DFT_A:, sl]
            yr_s[n2, :, sl] = (yr * tr - yi * ti).astype(BF16)
            yi_s[n2, :, sl] = (yr * ti + yi * tr).astype(BF16)
    g = g_ref[...].astype(BF16)
    half = DFT_C * DFT_C
    for j in range(DFT_A // DFT_C):
        sl = slice(j * DFT_C, (j + 1) * DFT_C)
        y = jnp.concatenate([yr_s[n2, sl, :] for n2 in range(DFT_C)]
                            + [yi_s[n2, sl, :] for n2 in range(DFT_C)], axis=0)
        r = jnp.dot(g, y, preferred_element_type=F32).astype(BF16)
        for k2 in range(DFT_C):
            ur_ref[0, k2, sl, :] = r[k2 * DFT_C:(k2 + 1) * DFT_C]
            ui_ref[0, k2, sl, :] = r[half + k2 * DFT_C:half + (k2 + 1) * DFT_C]


def _dft_tables(n_pos):
    a = np.arange(DFT_A)
    ang = 2.0 * np.pi * ((a[:, None] * a[None, :]) % DFT_A) / DFT_A
    norm = 1.0 / math.sqrt(n_pos * FOURIER_GROUP_DIM)
    fa = jnp.asarray(np.concatenate([np.cos(ang), -np.sin(ang)], axis=0) * norm, F32)
    if n_pos == DFT_A:
        return fa, None, None, None
    assert n_pos == DFT_A * DFT_C
    n2 = np.arange(DFT_C)
    tw = 2.0 * np.pi * (n2[:, None] * a[None, :]) / n_pos
    tr = jnp.asarray(np.repeat(np.cos(tw)[:, :, None], LANES, axis=2), F32)
    ti = jnp.asarray(np.repeat(-np.sin(tw)[:, :, None], LANES, axis=2), F32)
    ang16 = 2.0 * np.pi * ((n2[:, None] * n2[None, :]) % DFT_C) / DFT_C
    c16, s16 = np.cos(ang16), np.sin(ang16)
    eye = np.eye(DFT_C)
    kron = lambda f: np.einsum("kn,ab->kanb", f, eye).reshape(DFT_C * DFT_C, DFT_C * DFT_C)
    gmat = np.block([[kron(c16), kron(s16)], [kron(-s16), kron(c16)]])
    return fa, tr, ti, jnp.asarray(gmat, F32)


def _position_dft(u):
    bx, w = u.shape[0], u.shape[-1]
    n_pos = math.prod(u.shape[1:-1])
    fa, tr, ti, gmat = _dft_tables(n_pos)
    fa_spec = _resident((2 * DFT_A, DFT_A))
    if n_pos == DFT_A:
        blk = pl.BlockSpec((1, DFT_A, w), lambda b: (b, 0, 0))
        out = jax.ShapeDtypeStruct((bx, DFT_A, w), BF16)
        return pl.pallas_call(
            _dft_short_kernel,
            grid=(bx,), in_specs=[fa_spec, blk], out_specs=[blk, blk], out_shape=[out, out],
            name="dft_context",
        )(fa, u)
    rows = DFT_C * DFT_C
    blk = pl.BlockSpec((1, DFT_C, DFT_A, w), lambda b: (b, 0, 0, 0))
    out = jax.ShapeDtypeStruct((bx, DFT_C, DFT_A, w), BF16)
    ur, ui = pl.pallas_call(
        _dft_long_kernel,
        grid=(bx,),
        in_specs=[fa_spec, _resident((2 * rows, 2 * rows)), _resident(tr.shape), _resident(ti.shape), blk],
        out_specs=[blk, blk], out_shape=[out, out],
        scratch_shapes=[pltpu.VMEM((DFT_C, DFT_A, w), BF16)] * 2,
        compiler_params=pltpu.CompilerParams(vmem_limit_bytes=VMEM_LIMIT_BYTES),
        name="dft_long",
    )(fa, gmat, tr, ti, u)
    return ur.reshape(bx, n_pos, w), ui.reshape(bx, n_pos, w)


def _tail_kernel(a_ref, ur_ref, ui_ref, x_ref, gate1_ref, sh_ref, sc_ref, gate2_ref,
                 gpost_mix_ref, gpre_ref, gpost_ref, wa_ref, wr_ref, wi_ref, wg_ref, wu_ref, wd_ref, o_ref,
                 *, ctx_row):
    gate1, shift, scale, gate2 = (_mod_row(r, ctx_row) for r in (gate1_ref, sh_ref, sc_ref, gate2_ref))
    gain_mix = gate1 * gpost_mix_ref[...]
    gain_in = (1.0 + scale) * gpre_ref[...]
    gain_out = gate2 * gpost_ref[...]
    hidden = wg_ref.shape[1]
    tm = x_ref.shape[1]
    per = min(TAIL_SLICE_ROWS, tm)
    assert tm % per == 0

    def mix_dots(rows):
        mix = jnp.dot(a_ref[0, rows, :], wa_ref[...], preferred_element_type=F32)
        mix += jnp.dot(ur_ref[0, rows, :], wr_ref[...], preferred_element_type=F32)
        mix += jnp.dot(ui_ref[0, rows, :], wi_ref[...], preferred_element_type=F32)
        return mix

    def norms(rows, mix):
        x_mid = x_ref[0, rows, :] + _rms(mix, gain_mix)
        return x_mid, (_rms(x_mid, gain_in) + shift).astype(BF16)

    def ffn(h):
        acc = jnp.zeros((h.shape[0], o_ref.shape[2]), F32)
        for lo in range(0, hidden, FFN_CHUNK):
            hi = min(lo + FFN_CHUNK, hidden)
            g = jnp.dot(h, wg_ref[:, lo:hi], preferred_element_type=F32)
            u = jnp.dot(h, wu_ref[:, lo:hi], preferred_element_type=F32)
            a = (g * jax.nn.sigmoid(g) * u).astype(BF16)
            acc += jnp.dot(a, wd_ref[lo:hi, :], preferred_element_type=F32)
        return acc

    def epilogue(rows, x_mid, acc):
        o_ref[0, rows, :] = x_mid + _rms(acc, gain_out)

    slices = [slice(r, r + per) for r in range(0, tm, per)]
    mixes = [mix_dots(rows) for rows in slices]
    staged = norms(slices[0], mixes[0])
    pending = None
    for i, rows in enumerate(slices):
        x_mid, h = staged
        acc = ffn(h)
        if i + 1 < len(slices):
            staged = norms(slices[i + 1], mixes[i + 1])
        if pending is not None:
            epilogue(*pending)
        pending = (rows, x_mid, acc)
    epilogue(*pending)


def _tail(layer, attn, ur, ui, x, mod, g_post_mix, g_pre_ffn, g_post_ffn, w_out, w_four_r, w_four_i,
          wg, wu, wd, ctx_row=None):
    bx, t, d = x.shape
    tm = min(TAIL_TILE, t)
    hidden = wg.shape[-1]
    tok = lambda width: pl.BlockSpec((1, tm, width), lambda b, i: (b, i, 0))
    gain = _layer_resident(layer, (1, d))
    wfour = _layer_resident(layer, (FOURIER_WIDTH, d))
    return pl.pallas_call(
        functools.partial(_tail_kernel, ctx_row=ctx_row),
        grid=(bx, t // tm),
        in_specs=[tok(ATTN_WIDTH), tok(FOURIER_WIDTH), tok(FOURIER_WIDTH), tok(d),
                  _mod_spec(layer, 2, d), _mod_spec(layer, 3, d), _mod_spec(layer, 4, d), _mod_spec(layer, 5, d),
                  gain, gain, gain,
                  _resident((ATTN_WIDTH, d)), wfour, wfour,
                  _resident((d, hidden)), _resident((d, hidden)), _resident((hidden, d))],
        out_specs=tok(d),
        out_shape=jax.ShapeDtypeStruct((bx, t, d), F32),
        compiler_params=pltpu.CompilerParams(vmem_limit_bytes=VMEM_LIMIT_BYTES),
        name="tail",
    )(attn, ur, ui, x, mod, mod, mod, mod, g_post_mix, g_pre_ffn, g_post_ffn,
      w_out, w_four_r, w_four_i, wg, wu, wd)


def _rope_tables(seq):
    t = np.arange(seq)
    freqs = ROPE_THETA ** (-np.arange(ROPE_PAIRS_PER_AXIS, dtype=np.float64) / ROPE_PAIRS_PER_AXIS)
    ang = np.concatenate([(t // GRID_W)[:, None] * freqs, (t % GRID_W)[:, None] * freqs], axis=-1)
    cos = np.repeat(np.cos(ang), 2, axis=-1)
    sin = np.repeat(np.sin(ang), 2, axis=-1) * np.tile([-1.0, 1.0], HEAD_DIM // 2)
    heads_per_row = LANES // HEAD_DIM
    return (jnp.asarray(np.tile(cos, (1, heads_per_row)), F32), jnp.asarray(np.tile(sin, (1, heads_per_row)), F32))


def kernel(x, c, ctx, c_ctx, w_ada, b_ada, norm_pre_mix, norm_post_mix, norm_pre_ffn, norm_post_ffn,
           w_in, w_out, w_fourier, sink, w_gate, w_up, w_down):
    batch, seq, d = x.shape
    depth = w_ada.shape[0]
    assert batch + 1 <= MOD_ROWS

    cc = jnp.zeros((MOD_ROWS, d), F32).at[:batch].set(c).at[batch].set(c_ctx)
    mod = _ada(cc, w_ada, b_ada)
    w_four_r, w_four_i = _fold_fourier_weights(w_fourier, w_out)
    rope = _rope_tables(seq)

    q_scale = np.ones((w_in.shape[-1],), np.float32)
    q_scale[:ATTN_WIDTH] = HEAD_DIM ** -0.5 * LOG2_E
    w_in_b = (w_in * q_scale).astype(BF16)
    gains = [g.reshape(depth, 1, d) for g in (norm_pre_mix, norm_post_mix, norm_pre_ffn, norm_post_ffn)]
    g_pre_mix, g_post_mix, g_pre_ffn, g_post_ffn = gains

    xc = ctx
    for i in range(depth):
        q, k, v, u = _project(i, x, mod, g_pre_mix, w_in_b, rope)
        qc, kc, vc, uc = _project(i, xc, mod, g_pre_mix, w_in_b, None, ctx_row=batch)
        casts = [(w_out, i, ATTN_WIDTH), (w_gate, i, d), (w_up, i, d), (w_down, i, w_down.shape[1])]
        attn, tail_weights = _attention(q, k, v, kc, vc, sink[i], local=True, casts=casts)
        w_attn_out, wg, wu, wd = tail_weights

        def finish(stream, attn, ur, ui, ctx_row):
            return _tail(i, attn, ur, ui, stream, mod, g_post_mix, g_pre_ffn, g_post_ffn,
                         w_attn_out, w_four_r, w_four_i, wg, wu, wd, ctx_row)

        ur, ui = _position_dft(u)
        x = finish(x, attn, ur, ui, None)

        if i < depth - 1:
            attn_c, _ = _attention(qc, None, None, kc, vc, sink[i], local=False)
            urc, uic = _position_dft(uc)
            xc = finish(xc, attn_c, urc, uic, batch)
    return x
```

```python
import functools
import math

import numpy as np
import jax
import jax.numpy as jnp
from jax import lax
from jax.experimental import pallas as pl
from jax.experimental.pallas import tpu as pltpu

F32 = jnp.float32
BF16 = jnp.bfloat16

GRID_W = 64
HEAD_DIM = 64
N_Q_HEADS = 8
N_KV_HEADS = 2
Q_PER_KV = N_Q_HEADS // N_KV_HEADS
ATTN_WIDTH = N_Q_HEADS * HEAD_DIM
KV_WIDTH = N_KV_HEADS * HEAD_DIM
N_FOURIER_GROUPS = 8
FOURIER_GROUP_DIM = 64
FOURIER_WIDTH = N_FOURIER_GROUPS * FOURIER_GROUP_DIM
WINDOW = 128
ROPE_THETA = 10000.0
ROPE_PAIRS_PER_AXIS = HEAD_DIM // 4
RMS_EPS = 1e-6
N_MOD = 6
NEG_INF = -1e30
LOG2_E = math.log2(math.e)

LANES = 128
BF16_SUBLANES = 16
VMEM_LIMIT_BYTES = 60 * 1024 * 1024

DFT_A = 256
DFT_C = 16

MOD_ROWS = 8
PROJ_TILE = 1024
PROJ_SLICE_ROWS = 256
FFN_CHUNK = 512
TAIL_TILE = 1024
TAIL_SLICE_ROWS = 512
ATTN_Q_TILE = 1024
VT_HEAD_ROWS = HEAD_DIM + BF16_SUBLANES
VT_ROWS = N_KV_HEADS * VT_HEAD_ROWS


def _resident(shape):
    zeros = (0,) * len(shape)
    return pl.BlockSpec(shape, lambda *_: zeros, pipeline_mode=pl.Buffered(1))


def _layer_resident(layer, shape, index=None):
    index = (0,) * len(shape) if index is None else index
    return pl.BlockSpec((None,) + tuple(shape), lambda *_: (layer,) + tuple(index), pipeline_mode=pl.Buffered(1))


def _mod_spec(layer, which, d):
    return pl.BlockSpec((None, MOD_ROWS, d), lambda *_: (layer, 0, which))


def _mod_row(ref, ctx_row):
    row = pl.program_id(0) if ctx_row is None else ctx_row
    return ref[pl.ds(row, 1), :]


def _rms(xf, g):
    ms = jnp.mean(xf * xf, axis=-1, keepdims=True)
    return xf * lax.rsqrt(ms + RMS_EPS) * g


def _split_bf16(x):
    hi = x.astype(BF16)
    return hi, (x - hi.astype(F32)).astype(BF16)


def _dot3(a, b):
    (a_hi, a_lo), (b_hi, b_lo) = a, b
    dot = functools.partial(jnp.dot, preferred_element_type=F32)
    return dot(a_hi, b_hi) + dot(a_lo, b_hi) + dot(a_hi, b_lo)


def _ada_kernel(c_ref, w_ref, b_ref, o_ref):
    c = c_ref[...]
    a = c * jax.nn.sigmoid(c)
    a_hi, a_lo = _split_bf16(a)
    w_hi, w_lo = _split_bf16(w_ref[0])
    rows = a.shape[0]
    r = jnp.dot(jnp.concatenate([a_hi, a_lo], axis=0), w_hi, preferred_element_type=F32)
    o_ref[0] = r[:rows] + r[rows:] + jnp.dot(a_hi, w_lo, preferred_element_type=F32) + b_ref[0]


def _ada(cc, w_ada, b_ada):
    depth, d, n = w_ada.shape
    tn = 1536
    return pl.pallas_call(
        _ada_kernel,
        grid=(depth, n // tn),
        in_specs=[
            pl.BlockSpec((MOD_ROWS, d), lambda i, j: (0, 0)),
            pl.BlockSpec((1, d, tn), lambda i, j: (i, 0, j)),
            pl.BlockSpec((1, 1, tn), lambda i, j: (i, 0, j)),
        ],
        out_specs=pl.BlockSpec((1, MOD_ROWS, tn), lambda i, j: (i, 0, j)),
        out_shape=jax.ShapeDtypeStruct((depth, MOD_ROWS, n), F32),
        compiler_params=pltpu.CompilerParams(vmem_limit_bytes=VMEM_LIMIT_BYTES),
        name="ada",
    )(cc, w_ada, b_ada.reshape(depth, 1, n))


def _fold_kernel(c64_ref, s64_ref, wf_ref, wo_ref, wr_ref, wi_ref):
    gd = c64_ref.shape[0]
    for g in range(wf_ref.shape[0]):
        rows = slice(g * gd, (g + 1) * gd)
        t = _split_bf16(_dot3(_split_bf16(wf_ref[g]), _split_bf16(wo_ref[rows, :])))
        wr_ref[rows, :] = _dot3(_split_bf16(c64_ref[...]), t).astype(BF16)
        wi_ref[rows, :] = _dot3(_split_bf16(s64_ref[...]), t).astype(BF16)


def _fold_fourier_weights(w_fourier, w_out):
    depth, groups, gd, _ = w_fourier.shape
    d = w_out.shape[-1]
    width = groups * gd
    assert w_out.shape[1] == ATTN_WIDTH + width and ATTN_WIDTH == width
    j = np.arange(gd)
    ang = 2.0 * np.pi * ((j[:, None] * j[None, :]) % gd) / gd
    c64 = jnp.asarray(np.cos(ang), F32)
    s64 = jnp.asarray(np.sin(ang), F32)
    out = jax.ShapeDtypeStruct((depth, width, d), BF16)
    return pl.pallas_call(
        _fold_kernel,
        grid=(depth,),
        in_specs=[
            pl.BlockSpec((gd, gd), lambda i: (0, 0)),
            pl.BlockSpec((gd, gd), lambda i: (0, 0)),
            pl.BlockSpec((None, groups, gd, gd), lambda i: (i, 0, 0, 0)),
            pl.BlockSpec((None, width, d), lambda i: (i, 1, 0)),
        ],
        out_specs=[pl.BlockSpec((None, width, d), lambda i: (i, 0, 0))] * 2,
        out_shape=[out, out],
        name="fold_fourier",
    )(c64, s64, w_fourier, w_out)


def _proj_kernel(x_ref, sh_ref, sc_ref, g_ref, w_ref, *rest, rope, ctx_row):
    if rope:
        cos_ref, sin_ref, q_ref, k_ref, v_ref, u_ref, u_scratch = rest
    else:
        q_ref, k_ref, v_ref, u_ref = rest
    gain = (1.0 + _mod_row(sc_ref, ctx_row)) * g_ref[...]
    shift = _mod_row(sh_ref, ctx_row)
    k_lo = ATTN_WIDTH
    v_lo = ATTN_WIDTH + KV_WIDTH
    u_lo = ATTN_WIDTH + 2 * KV_WIDTH
    tm = x_ref.shape[1]
    per = min(PROJ_SLICE_ROWS, tm)
    assert tm % per == 0 and per % DFT_C == 0
    if rope:
        even_lane = (lax.broadcasted_iota(jnp.int32, (per, LANES), 1) & 1) == 0

    def norm(rows):
        return (_rms(x_ref[0, rows, :], gain) + shift).astype(BF16)

    def emit(rows, p):
        if rope:
            cos = cos_ref[rows, :]
            sin = sin_ref[rows, :]

            def rot(xs):
                partner = jnp.where(even_lane, pltpu.roll(xs, LANES - 1, 1), pltpu.roll(xs, 1, 1))
                return xs * cos + partner * sin

            for j in range(ATTN_WIDTH // LANES):
                q_ref[0, rows, j * LANES:(j + 1) * LANES] = rot(p[:, j * LANES:(j + 1) * LANES]).astype(BF16)
            k_ref[0, rows, :] = rot(p[:, k_lo:v_lo]).astype(BF16)
        else:
            q_ref[0, rows, :] = p[:, :k_lo].astype(BF16)
            k_ref[0, rows, :] = p[:, k_lo:v_lo].astype(BF16)
        vt = p[:, v_lo:u_lo].T.astype(BF16)
        ones = jnp.ones((VT_HEAD_ROWS - HEAD_DIM, per), BF16)
        for h_kv in range(N_KV_HEADS):
            v_ref[0, h_kv * VT_HEAD_ROWS:h_kv * VT_HEAD_ROWS + HEAD_DIM, rows] = (
                vt[h_kv * HEAD_DIM:(h_kv + 1) * HEAD_DIM])
            v_ref[0, h_kv * VT_HEAD_ROWS + HEAD_DIM:(h_kv + 1) * VT_HEAD_ROWS, rows] = ones
        if rope:
            n1 = slice(rows.start // DFT_C, rows.stop // DFT_C)
            for c in range(u_scratch.shape[0]):
                u_scratch[c, rows, :] = p[:, u_lo + c * LANES:u_lo + (c + 1) * LANES]
                for n2 in range(DFT_C):
                    u_ref[0, n2, n1, c * LANES:(c + 1) * LANES] = (
                        u_scratch[c, pl.ds(rows.start + n2, per // DFT_C, stride=DFT_C), :].astype(BF16))
        else:
            u_ref[0, rows, :] = p[:, u_lo:].astype(BF16)

    slices = [slice(r, r + per) for r in range(0, tm, per)]
    h = norm(slices[0])
    pending = None
    for i, rows in enumerate(slices):
        p = jnp.dot(h, w_ref[...], preferred_element_type=F32)
        if i + 1 < len(slices):
            h = norm(slices[i + 1])
        if pending is not None:
            emit(*pending)
        pending = (rows, p)
    emit(*pending)


def _project(layer, x, mod, gains, w_in, rope_tables, ctx_row=None):
    bx, t, d = x.shape
    tm = min(PROJ_TILE, t)
    n_in = w_in.shape[-1]
    rope = rope_tables is not None
    tok = lambda width: pl.BlockSpec((1, tm, width), lambda b, i: (b, i, 0))
    in_specs = [tok(d), _mod_spec(layer, 0, d), _mod_spec(layer, 1, d),
                _layer_resident(layer, (1, d)), _layer_resident(layer, (d, n_in))]
    args = [x, mod, mod, gains, w_in]
    if rope:
        in_specs += [pl.BlockSpec((tm, LANES), lambda b, i: (i, 0))] * 2
        args += list(rope_tables)
    widths = (ATTN_WIDTH, KV_WIDTH, KV_WIDTH, FOURIER_WIDTH)
    out_specs = [tok(w) for w in widths]
    out_shape = [jax.ShapeDtypeStruct((bx, t, w), BF16) for w in widths]
    out_specs[2] = pl.BlockSpec((1, VT_ROWS, tm), lambda b, i: (b, 0, i))
    out_shape[2] = jax.ShapeDtypeStruct((bx, VT_ROWS, t), BF16)
    scratch = []
    if rope:
        assert t == DFT_A * DFT_C and tm % DFT_C == 0
        out_specs[3] = pl.BlockSpec((1, DFT_C, tm // DFT_C, FOURIER_WIDTH), lambda b, i: (b, 0, i, 0))
        out_shape[3] = jax.ShapeDtypeStruct((bx, DFT_C, DFT_A, FOURIER_WIDTH), BF16)
        scratch = [pltpu.VMEM((FOURIER_WIDTH // LANES, tm, LANES), F32)]
    return pl.pallas_call(
        functools.partial(_proj_kernel, rope=rope, ctx_row=ctx_row),
        grid=(bx, t // tm),
        in_specs=in_specs,
        out_specs=out_specs,
        out_shape=out_shape,
        scratch_shapes=scratch,
        compiler_params=pltpu.CompilerParams(vmem_limit_bytes=VMEM_LIMIT_BYTES),
        name="project_rope" if rope else "project",
    )(*args)


def _attn_kernel(sink_ref, q_ref, *rest, local, n_cast):
    w = WINDOW
    n_in = len(rest) - 1 - 2 * n_cast
    cast_src = rest[n_in:n_in + n_cast]
    cast_dst = rest[n_in + n_cast + 1:]
    rest = rest[:n_in] + rest[n_in + n_cast:n_in + n_cast + 1]
    for src, dst in zip(cast_src, cast_dst):
        dst[...] = src[...].astype(BF16)
    if local:
        kp_ref, kc_ref, kn_ref, vp_ref, vc_ref, vn_ref, kx_ref, vx_ref, o_ref = rest
        k_all = jnp.concatenate([kp_ref[0], kc_ref[0], kn_ref[0]], axis=0)
        vt_all = jnp.concatenate([vp_ref[0], vc_ref[0], vn_ref[0]], axis=1)
        step = pl.program_id(1)
        last = pl.num_programs(1) - 1
    else:
        kx_ref, vx_ref, o_ref = rest
    kx = kx_ref[0]
    vxt = vx_ref[0]
    n_sub = q_ref.shape[1] // w
    cols = Q_PER_KV * w
    group = lax.shift_right_logical(lax.broadcasted_iota(jnp.int32, (1, cols), 1), int(math.log2(w)))
    if local:
        kj = lax.broadcasted_iota(jnp.int32, (w, cols), 0)
        qi = lax.broadcasted_iota(jnp.int32, (w, cols), 1) & (w - 1)
        far = 1 << 20
        bias_prev = jnp.where(kj >= qi, 0.0, NEG_INF)
        bias_next = jnp.where(kj <= qi, 0.0, NEG_INF)
        bias_prev_first = jnp.where(kj >= qi + jnp.where(step > 0, 0, far), 0.0, NEG_INF)
        bias_next_last = jnp.where(kj <= qi - jnp.where(step < last, 0, far), 0.0, NEG_INF)
    contract_last = (((1,), (1,)), ((), ()))

    def sink_row(h):
        sink = jnp.full((1, cols), sink_ref[Q_PER_KV * h] * LOG2_E, F32)
        for g in range(1, Q_PER_KV):
            sink = jnp.where(group == g, sink_ref[Q_PER_KV * h + g] * LOG2_E, sink)
        return sink

    sinks = [sink_row(h) for h in range(N_KV_HEADS)]

    def scores(h, s):
        hs = slice(h * HEAD_DIM, (h + 1) * HEAD_DIM)
        q = q_ref[0, s * w:(s + 1) * w, :]
        qs = jnp.concatenate([q[:, (Q_PER_KV * h + g) * HEAD_DIM:(Q_PER_KV * h + g + 1) * HEAD_DIM]
                              for g in range(Q_PER_KV)], axis=0)
        pieces = []
        if local:
            sl = lax.dot_general(k_all[s * w:(s + 3) * w, hs], qs, contract_last, preferred_element_type=F32)
            pieces += [sl[:w] + (bias_prev_first if s == 0 else bias_prev), sl[w:2 * w],
                       sl[2 * w:] + (bias_next_last if s == n_sub - 1 else bias_next)]
        pieces.append(lax.dot_general(kx[:, hs], qs, contract_last, preferred_element_type=F32))
        m = sinks[h]
        for x in pieces:
            m = jnp.maximum(m, jnp.max(x, axis=0, keepdims=True))
        return pieces, m

    def finish(h, s, pieces, m):
        vs = slice(h * VT_HEAD_ROWS, (h + 1) * VT_HEAD_ROWS)
        probs = [jnp.exp2(x - m).astype(BF16) for x in pieces]
        ot = jnp.dot(vxt[vs, :], probs[-1], preferred_element_type=F32)
        if local:
            ot = ot + jnp.dot(vt_all[vs, s * w:(s + 3) * w], jnp.concatenate(probs[:3], axis=0),
                              preferred_element_type=F32)
        denom = ot[HEAD_DIM:HEAD_DIM + 1] + jnp.exp2(sinks[h] - m)
        ot = ot[:HEAD_DIM] / denom
        o = jnp.concatenate([ot[:, g * w:(g + 1) * w].T for g in range(Q_PER_KV)], axis=1)
        lo = h * Q_PER_KV * HEAD_DIM
        o_ref[0, s * w:(s + 1) * w, lo:lo + Q_PER_KV * HEAD_DIM] = o.astype(BF16)

    work = [(h, s) for h in range(N_KV_HEADS) for s in range(n_sub)]
    ready = scores(*work[0])
    for i, (h, s) in enumerate(work):
        upcoming = scores(*work[i + 1]) if i + 1 < len(work) else None
        finish(h, s, *ready)
        ready = upcoming


def _attention(q, k, v, kx, vx, sink, cx, local, casts=()):
    smem = pl.BlockSpec(memory_space=pltpu.SMEM)
    kx_spec = pl.BlockSpec((1, cx, KV_WIDTH), lambda b, n: (0, b, 0))
    vx_spec = pl.BlockSpec((1, VT_ROWS, cx), lambda b, n: (0, 0, b))
    if local:
        bx, t, _ = q.shape
        tq = min(ATTN_Q_TILE, t)
        q_spec = pl.BlockSpec((1, tq, ATTN_WIDTH), lambda b, n: (b, n, 0))
    else:
        bx, t, tq = q.shape[1] // cx, cx, cx
        q_spec = pl.BlockSpec((1, tq, ATTN_WIDTH), lambda b, n: (0, b, 0))
    nb = t // tq
    if local:
        per = tq // WINDOW
        n_win = t // WINDOW
        before = lambda n: jnp.maximum(n * per - 1, 0)
        after = lambda n: jnp.minimum((n + 1) * per, n_win - 1)
        k_specs = [pl.BlockSpec((1, WINDOW, KV_WIDTH), lambda b, n: (b, before(n), 0)),
                   pl.BlockSpec((1, tq, KV_WIDTH), lambda b, n: (b, n, 0)),
                   pl.BlockSpec((1, WINDOW, KV_WIDTH), lambda b, n: (b, after(n), 0))]
        v_specs = [pl.BlockSpec((1, VT_ROWS, WINDOW), lambda b, n: (b, 0, before(n))),
                   pl.BlockSpec((1, VT_ROWS, tq), lambda b, n: (b, 0, n)),
                   pl.BlockSpec((1, VT_ROWS, WINDOW), lambda b, n: (b, 0, after(n)))]
        in_specs = [smem, q_spec] + k_specs + v_specs + [kx_spec, vx_spec]
        args = [sink, q, k, k, k, v, v, v, kx, vx]
    else:
        in_specs = [smem, q_spec, kx_spec, vx_spec]
        args = [sink, q, kx, vx]
    out_specs = [q_spec]
    out_shape = [jax.ShapeDtypeStruct(q.shape, BF16)]
    steps = bx * nb
    for w, layer, rows in casts:
        slab = rows // steps
        assert rows % steps == 0 and slab % BF16_SUBLANES == 0
        in_specs.append(pl.BlockSpec((None, slab, w.shape[-1]), lambda b, n, layer=layer: (layer, b * nb + n, 0)))
        out_specs.append(pl.BlockSpec((slab, w.shape[-1]), lambda b, n: (b * nb + n, 0)))
        out_shape.append(jax.ShapeDtypeStruct((rows, w.shape[-1]), BF16))
        args.append(w)
    outs = pl.pallas_call(
        functools.partial(_attn_kernel, local=local, n_cast=len(casts)),
        grid=(bx, nb),
        in_specs=in_specs,
        out_specs=out_specs,
        out_shape=out_shape,
        compiler_params=pltpu.CompilerParams(vmem_limit_bytes=VMEM_LIMIT_BYTES),
        name="attn_window" if local else "attn_context",
    )(*args)
    return outs[0], outs[1:]


def _dft_short_kernel(fa_ref, u_ref, ur_ref, ui_ref):
    y = jnp.dot(fa_ref[...].astype(BF16), u_ref[0], preferred_element_type=F32)
    ur_ref[0] = y[:DFT_A].astype(BF16)
    ui_ref[0] = y[DFT_A:].astype(BF16)


def _dft_long_kernel(fa_ref, g_ref, tr_ref, ti_ref, u_ref, ur_ref, ui_ref, yr_s, yi_s):
    fa = fa_ref[...].astype(BF16)
    for n2 in range(DFT_C):
        y = jnp.dot(fa, u_ref[0, n2], preferred_element_type=F32)
        tr = tr_ref[n2]
        ti = ti_ref[n2]
        for c in range(y.shape[1] // LANES):
            sl = slice(c * LANES, (c + 1) * LANES)
            yr = y[:DFT_A, sl]
            yi = y[DFT_A:, sl]
            yr_s[n2, :, sl] = (yr * tr - yi * ti).astype(BF16)
            yi_s[n2, :, sl] = (yr * ti + yi * tr).astype(BF16)
    g = g_ref[...].astype(BF16)
    half = DFT_C * DFT_C
    for j in range(DFT_A // DFT_C):
        sl = slice(j * DFT_C, (j + 1) * DFT_C)
        y = jnp.concatenate([yr_s[n2, sl, :] for n2 in range(DFT_C)]
                            + [yi_s[n2, sl, :] for n2 in range(DFT_C)], axis=0)
        r = jnp.dot(g, y, preferred_element_type=F32).astype(BF16)
        for k2 in range(DFT_C):
            ur_ref[0, k2, sl, :] = r[k2 * DFT_C:(k2 + 1) * DFT_C]
            ui_ref[0, k2, sl, :] = r[half + k2 * DFT_C:half + (k2 + 1) * DFT_C]


def _dft_tables(n_pos):
    a = np.arange(DFT_A)
    ang = 2.0 * np.pi * ((a[:, None] * a[None, :]) % DFT_A) / DFT_A
    norm = 1.0 / math.sqrt(n_pos * FOURIER_GROUP_DIM)
    fa = jnp.asarray(np.concatenate([np.cos(ang), -np.sin(ang)], axis=0) * norm, F32)
    if n_pos == DFT_A:
        return fa, None, None, None
    assert n_pos == DFT_A * DFT_C
    n2 = np.arange(DFT_C)
    tw = 2.0 * np.pi * (n2[:, None] * a[None, :]) / n_pos
    tr = jnp.asarray(np.repeat(np.cos(tw)[:, :, None], LANES, axis=2), F32)
    ti = jnp.asarray(np.repeat(-np.sin(tw)[:, :, None], LANES, axis=2), F32)
    ang16 = 2.0 * np.pi * ((n2[:, None] * n2[None, :]) % DFT_C) / DFT_C
    c16, s16 = np.cos(ang16), np.sin(ang16)
    eye = np.eye(DFT_C)
    kron = lambda f: np.einsum("kn,ab->kanb", f, eye).reshape(DFT_C * DFT_C, DFT_C * DFT_C)
    gmat = np.block([[kron(c16), kron(s16)], [kron(-s16), kron(c16)]])
    return fa, tr, ti, jnp.asarray(gmat, F32)


def _position_dft(u):
    bx, w = u.shape[0], u.shape[-1]
    n_pos = DFT_A if u.ndim == 3 else math.prod(u.shape[1:-1])
    fa, tr, ti, gmat = _dft_tables(n_pos)
    fa_spec = _resident((2 * DFT_A, DFT_A))
    if u.ndim == 3:
        blk = pl.BlockSpec((1, DFT_A, w), lambda b: (0, b, 0))
        out = jax.ShapeDtypeStruct(u.shape, BF16)
        return pl.pallas_call(
            _dft_short_kernel,
            grid=(u.shape[1] // DFT_A,), in_specs=[fa_spec, blk], out_specs=[blk, blk], out_shape=[out, out],
            name="dft_context",
        )(fa, u)
    rows = DFT_C * DFT_C
    blk = pl.BlockSpec((1, DFT_C, DFT_A, w), lambda b: (b, 0, 0, 0))
    out = jax.ShapeDtypeStruct((bx, DFT_C, DFT_A, w), BF16)
    ur, ui = pl.pallas_call(
        _dft_long_kernel,
        grid=(bx,),
        in_specs=[fa_spec, _resident((2 * rows, 2 * rows)), _resident(tr.shape), _resident(ti.shape), blk],
        out_specs=[blk, blk], out_shape=[out, out],
        scratch_shapes=[pltpu.VMEM((DFT_C, DFT_A, w), BF16)] * 2,
        compiler_params=pltpu.CompilerParams(vmem_limit_bytes=VMEM_LIMIT_BYTES),
        name="dft_long",
    )(fa, gmat, tr, ti, u)
    return ur.reshape(bx, n_pos, w), ui.reshape(bx, n_pos, w)


def _tail_kernel(a_ref, ur_ref, ui_ref, x_ref, gate1_ref, sh_ref, sc_ref, gate2_ref,
                 gpost_mix_ref, gpre_ref, gpost_ref, wa_ref, wr_ref, wi_ref, wg_ref, wu_ref, wd_ref, o_ref,
                 *, ctx_row):
    gate1, shift, scale, gate2 = (_mod_row(r, ctx_row) for r in (gate1_ref, sh_ref, sc_ref, gate2_ref))
    gain_mix = gate1 * gpost_mix_ref[...]
    gain_in = (1.0 + scale) * gpre_ref[...]
    gain_out = gate2 * gpost_ref[...]
    hidden = wg_ref.shape[1]
    tm = x_ref.shape[1]
    per = min(TAIL_SLICE_ROWS, tm)
    assert tm % per == 0

    def mix_dots(rows):
        mix = jnp.dot(a_ref[0, rows, :], wa_ref[...], preferred_element_type=F32)
        mix += jnp.dot(ur_ref[0, rows, :], wr_ref[...], preferred_element_type=F32)
        mix += jnp.dot(ui_ref[0, rows, :], wi_ref[...], preferred_element_type=F32)
        return mix

    def norms(rows, mix):
        x_mid = x_ref[0, rows, :] + _rms(mix, gain_mix)
        return x_mid, (_rms(x_mid, gain_in) + shift).astype(BF16)

    def ffn(h):
        acc = jnp.zeros((h.shape[0], o_ref.shape[2]), F32)
        for lo in range(0, hidden, FFN_CHUNK):
            hi = min(lo + FFN_CHUNK, hidden)
            g = jnp.dot(h, wg_ref[:, lo:hi], preferred_element_type=F32)
            u = jnp.dot(h, wu_ref[:, lo:hi], preferred_element_type=F32)
            a = (g * jax.nn.sigmoid(g) * u).astype(BF16)
            acc += jnp.dot(a, wd_ref[lo:hi, :], preferred_element_type=F32)
        return acc

    def epilogue(rows, x_mid, acc):
        o_ref[0, rows, :] = x_mid + _rms(acc, gain_out)

    slices = [slice(r, r + per) for r in range(0, tm, per)]
    mixes = [mix_dots(rows) for rows in slices]
    staged = norms(slices[0], mixes[0])
    pending = None
    for i, rows in enumerate(slices):
        x_mid, h = staged
        acc = ffn(h)
        if i + 1 < len(slices):
            staged = norms(slices[i + 1], mixes[i + 1])
        if pending is not None:
            epilogue(*pending)
        pending = (rows, x_mid, acc)
    epilogue(*pending)


def _tail(layer, attn, ur, ui, x, mod, g_post_mix, g_pre_ffn, g_post_ffn, w_out, w_four_r, w_four_i,
          wg, wu, wd, ctx_row=None):
    bx, t, d = x.shape
    tm = min(TAIL_TILE, t)
    hidden = wg.shape[-1]
    tok = lambda width: pl.BlockSpec((1, tm, width), lambda b, i: (b, i, 0))
    gain = _layer_resident(layer, (1, d))
    wfour = _layer_resident(layer, (FOURIER_WIDTH, d))
    return pl.pallas_call(
        functools.partial(_tail_kernel, ctx_row=ctx_row),
        grid=(bx, t // tm),
        in_specs=[tok(ATTN_WIDTH), tok(FOURIER_WIDTH), tok(FOURIER_WIDTH), tok(d),
                  _mod_spec(layer, 2, d), _mod_spec(layer, 3, d), _mod_spec(layer, 4, d), _mod_spec(layer, 5, d),
                  gain, gain, gain,
                  _resident((ATTN_WIDTH, d)), wfour, wfour,
                  _resident((d, hidden)), _resident((d, hidden)), _resident((hidden, d))],
        out_specs=tok(d),
        out_shape=jax.ShapeDtypeStruct((bx, t, d), F32),
        compiler_params=pltpu.CompilerParams(vmem_limit_bytes=VMEM_LIMIT_BYTES),
        name="tail",
    )(attn, ur, ui, x, mod, mod, mod, mod, g_post_mix, g_pre_ffn, g_post_ffn,
      w_out, w_four_r, w_four_i, wg, wu, wd)


def _rope_tables(seq):
    t = np.arange(seq)
    freqs = ROPE_THETA ** (-np.arange(ROPE_PAIRS_PER_AXIS, dtype=np.float64) / ROPE_PAIRS_PER_AXIS)
    ang = np.concatenate([(t // GRID_W)[:, None] * freqs, (t % GRID_W)[:, None] * freqs], axis=-1)
    cos = np.repeat(np.cos(ang), 2, axis=-1)
    sin = np.repeat(np.sin(ang), 2, axis=-1) * np.tile([-1.0, 1.0], HEAD_DIM // 2)
    heads_per_row = LANES // HEAD_DIM
    return (jnp.asarray(np.tile(cos, (1, heads_per_row)), F32), jnp.asarray(np.tile(sin, (1, heads_per_row)), F32))


def kernel(x, c, ctx, c_ctx, w_ada, b_ada, norm_pre_mix, norm_post_mix, norm_pre_ffn, norm_post_ffn,
           w_in, w_out, w_fourier, sink, w_gate, w_up, w_down):
    batch, seq, d = x.shape
    depth = w_ada.shape[0]
    assert batch + 1 <= MOD_ROWS

    cc = jnp.zeros((MOD_ROWS, d), F32).at[:batch].set(c).at[batch].set(c_ctx)
    mod = _ada(cc, w_ada, b_ada)
    w_four_r, w_four_i = _fold_fourier_weights(w_fourier, w_out)
    rope = _rope_tables(seq)

    q_scale = np.ones((w_in.shape[-1],), np.float32)
    q_scale[:ATTN_WIDTH] = HEAD_DIM ** -0.5 * LOG2_E
    w_in_b = (w_in * q_scale).astype(BF16)
    gains = [g.reshape(depth, 1, d) for g in (norm_pre_mix, norm_post_mix, norm_pre_ffn, norm_post_ffn)]
    g_pre_mix, g_post_mix, g_pre_ffn, g_post_ffn = gains

    ctx_len = ctx.shape[1]
    assert ctx_len == DFT_A
    xc = ctx.reshape(1, batch * ctx_len, d)
    for i in range(depth):
        q, k, v, u = _project(i, x, mod, g_pre_mix, w_in_b, rope)
        qc, kc, vc, uc = _project(i, xc, mod, g_pre_mix, w_in_b, None, ctx_row=batch)
        casts = [(w_out, i, ATTN_WIDTH), (w_gate, i, d), (w_up, i, d), (w_down, i, w_down.shape[1])]
        attn, tail_weights = _attention(q, k, v, kc, vc, sink[i], ctx_len, local=True, casts=casts)
        w_attn_out, wg, wu, wd = tail_weights

        def finish(stream, attn, ur, ui, ctx_row):
            return _tail(i, attn, ur, ui, stream, mod, g_post_mix, g_pre_ffn, g_post_ffn,
                         w_attn_out, w_four_r, w_four_i, wg, wu, wd, ctx_row)

        ur, ui = _position_dft(u)
        x = finish(x, attn, ur, ui, None)

        if i < depth - 1:
            attn_c, _ = _attention(qc, None, None, kc, vc, sink[i], ctx_len, local=False)
            urc, uic = _position_dft(uc)
            xc = finish(xc, attn_c, urc, uic, batch)
    return x
```

```python
import functools
import math

import numpy as np
import jax
import jax.numpy as jnp
from jax import lax
from jax.experimental import pallas as pl
from jax.experimental.pallas import tpu as pltpu

F32 = jnp.float32
BF16 = jnp.bfloat16

GRID_W = 64
HEAD_DIM = 64
N_Q_HEADS = 8
N_KV_HEADS = 2
Q_PER_KV = N_Q_HEADS // N_KV_HEADS
ATTN_WIDTH = N_Q_HEADS * HEAD_DIM
KV_WIDTH = N_KV_HEADS * HEAD_DIM
N_FOURIER_GROUPS = 8
FOURIER_GROUP_DIM = 64
FOURIER_WIDTH = N_FOURIER_GROUPS * FOURIER_GROUP_DIM
WINDOW = 128
ROPE_THETA = 10000.0
ROPE_PAIRS_PER_AXIS = HEAD_DIM // 4
RMS_EPS = 1e-6
N_MOD = 6
NEG_INF = -1e30
LOG2_E = math.log2(math.e)

LANES = 128
BF16_SUBLANES = 16
VMEM_LIMIT_BYTES = 60 * 1024 * 1024

DFT_A = 256
DFT_C = 16

MOD_ROWS = 8
PROJ_TILE = 1024
PROJ_SLICE_ROWS = 256
FFN_CHUNK = 512
TAIL_TILE = 1024
TAIL_SLICE_ROWS = 512
ATTN_Q_TILE = 1024
ATTN_LOOKAHEAD = 2
VT_HEAD_ROWS = HEAD_DIM + BF16_SUBLANES
VT_ROWS = N_KV_HEADS * VT_HEAD_ROWS


def _resident(shape):
    zeros = (0,) * len(shape)
    return pl.BlockSpec(shape, lambda *_: zeros, pipeline_mode=pl.Buffered(1))


def _layer_resident(layer, shape, index=None):
    index = (0,) * len(shape) if index is None else index
    return pl.BlockSpec((None,) + tuple(shape), lambda *_: (layer,) + tuple(index), pipeline_mode=pl.Buffered(1))


def _mod_spec(layer, which, d):
    return pl.BlockSpec((None, MOD_ROWS, d), lambda *_: (layer, 0, which))


def _mod_row(ref, ctx_row):
    row = pl.program_id(0) if ctx_row is None else ctx_row
    return ref[pl.ds(row, 1), :]


def _rms(xf, g):
    ms = jnp.mean(xf * xf, axis=-1, keepdims=True)
    return xf * lax.rsqrt(ms + RMS_EPS) * g


def _split_bf16(x):
    hi = x.astype(BF16)
    return hi, (x - hi.astype(F32)).astype(BF16)


def _dot3(a, b):
    (a_hi, a_lo), (b_hi, b_lo) = a, b
    dot = functools.partial(jnp.dot, preferred_element_type=F32)
    return dot(a_hi, b_hi) + dot(a_lo, b_hi) + dot(a_hi, b_lo)


def _ada_kernel(c_ref, w_ref, b_ref, o_ref):
    c = c_ref[...]
    a = c * jax.nn.sigmoid(c)
    a_hi, a_lo = _split_bf16(a)
    w_hi, w_lo = _split_bf16(w_ref[0])
    rows = a.shape[0]
    r = jnp.dot(jnp.concatenate([a_hi, a_lo], axis=0), w_hi, preferred_element_type=F32)
    o_ref[0] = r[:rows] + r[rows:] + jnp.dot(a_hi, w_lo, preferred_element_type=F32) + b_ref[0]


def _ada(cc, w_ada, b_ada):
    depth, d, n = w_ada.shape
    tn = 1536
    return pl.pallas_call(
        _ada_kernel,
        grid=(depth, n // tn),
        in_specs=[
            pl.BlockSpec((MOD_ROWS, d), lambda i, j: (0, 0)),
            pl.BlockSpec((1, d, tn), lambda i, j: (i, 0, j)),
            pl.BlockSpec((1, 1, tn), lambda i, j: (i, 0, j)),
        ],
        out_specs=pl.BlockSpec((1, MOD_ROWS, tn), lambda i, j: (i, 0, j)),
        out_shape=jax.ShapeDtypeStruct((depth, MOD_ROWS, n), F32),
        compiler_params=pltpu.CompilerParams(vmem_limit_bytes=VMEM_LIMIT_BYTES),
        name="ada",
    )(cc, w_ada, b_ada.reshape(depth, 1, n))


def _fold_kernel(c64_ref, s64_ref, wf_ref, wo_ref, wr_ref, wi_ref):
    gd = c64_ref.shape[0]
    for g in range(wf_ref.shape[0]):
        rows = slice(g * gd, (g + 1) * gd)
        t = _split_bf16(_dot3(_split_bf16(wf_ref[g]), _split_bf16(wo_ref[rows, :])))
        wr_ref[rows, :] = _dot3(_split_bf16(c64_ref[...]), t).astype(BF16)
        wi_ref[rows, :] = _dot3(_split_bf16(s64_ref[...]), t).astype(BF16)


def _fold_fourier_weights(w_fourier, w_out):
    depth, groups, gd, _ = w_fourier.shape
    d = w_out.shape[-1]
    width = groups * gd
    assert w_out.shape[1] == ATTN_WIDTH + width and ATTN_WIDTH == width
    j = np.arange(gd)
    ang = 2.0 * np.pi * ((j[:, None] * j[None, :]) % gd) / gd
    c64 = jnp.asarray(np.cos(ang), F32)
    s64 = jnp.asarray(np.sin(ang), F32)
    out = jax.ShapeDtypeStruct((depth, width, d), BF16)
    return pl.pallas_call(
        _fold_kernel,
        grid=(depth,),
        in_specs=[
            pl.BlockSpec((gd, gd), lambda i: (0, 0)),
            pl.BlockSpec((gd, gd), lambda i: (0, 0)),
            pl.BlockSpec((None, groups, gd, gd), lambda i: (i, 0, 0, 0)),
            pl.BlockSpec((None, width, d), lambda i: (i, 1, 0)),
        ],
        out_specs=[pl.BlockSpec((None, width, d), lambda i: (i, 0, 0))] * 2,
        out_shape=[out, out],
        name="fold_fourier",
    )(c64, s64, w_fourier, w_out)


def _proj_kernel(x_ref, sh_ref, sc_ref, g_ref, w_ref, *rest, rope, ctx_row):
    if rope:
        cos_ref, sin_ref, q_ref, k_ref, v_ref, u_ref, u_scratch = rest
    else:
        q_ref, k_ref, v_ref, u_ref = rest
    gain = (1.0 + _mod_row(sc_ref, ctx_row)) * g_ref[...]
    shift = _mod_row(sh_ref, ctx_row)
    k_lo = ATTN_WIDTH
    v_lo = ATTN_WIDTH + KV_WIDTH
    u_lo = ATTN_WIDTH + 2 * KV_WIDTH
    tm = x_ref.shape[1]
    per = min(PROJ_SLICE_ROWS, tm)
    assert tm % per == 0 and per % DFT_C == 0
    if rope:
        even_lane = (lax.broadcasted_iota(jnp.int32, (per, LANES), 1) & 1) == 0

    def norm(rows):
        return (_rms(x_ref[0, rows, :], gain) + shift).astype(BF16)

    def emit(rows, p):
        if rope:
            cos = cos_ref[rows, :]
            sin = sin_ref[rows, :]

            def rot(xs):
                partner = jnp.where(even_lane, pltpu.roll(xs, LANES - 1, 1), pltpu.roll(xs, 1, 1))
                return xs * cos + partner * sin

            for j in range(ATTN_WIDTH // LANES):
                q_ref[0, rows, j * LANES:(j + 1) * LANES] = rot(p[:, j * LANES:(j + 1) * LANES]).astype(BF16)
            k_ref[0, rows, :] = rot(p[:, k_lo:v_lo]).astype(BF16)
        else:
            q_ref[0, rows, :] = p[:, :k_lo].astype(BF16)
            k_ref[0, rows, :] = p[:, k_lo:v_lo].astype(BF16)
        vt = p[:, v_lo:u_lo].T.astype(BF16)
        ones = jnp.ones((VT_HEAD_ROWS - HEAD_DIM, per), BF16)
        for h_kv in range(N_KV_HEADS):
            v_ref[0, h_kv * VT_HEAD_ROWS:h_kv * VT_HEAD_ROWS + HEAD_DIM, rows] = (
                vt[h_kv * HEAD_DIM:(h_kv + 1) * HEAD_DIM])
            v_ref[0, h_kv * VT_HEAD_ROWS + HEAD_DIM:(h_kv + 1) * VT_HEAD_ROWS, rows] = ones
        if rope:
            n1 = slice(rows.start // DFT_C, rows.stop // DFT_C)
            for c in range(u_scratch.shape[0]):
                u_scratch[c, rows, :] = p[:, u_lo + c * LANES:u_lo + (c + 1) * LANES]
                for n2 in range(DFT_C):
                    u_ref[0, n2, n1, c * LANES:(c + 1) * LANES] = (
                        u_scratch[c, pl.ds(rows.start + n2, per // DFT_C, stride=DFT_C), :].astype(BF16))
        else:
            u_ref[0, rows, :] = p[:, u_lo:].astype(BF16)

    slices = [slice(r, r + per) for r in range(0, tm, per)]
    h = norm(slices[0])
    pending = None
    for i, rows in enumerate(slices):
        p = jnp.dot(h, w_ref[...], preferred_element_type=F32)
        if i + 1 < len(slices):
            h = norm(slices[i + 1])
        if pending is not None:
            emit(*pending)
        pending = (rows, p)
    emit(*pending)


def _project(layer, x, mod, gains, w_in, rope_tables, ctx_row=None):
    bx, t, d = x.shape
    tm = min(PROJ_TILE, t)
    n_in = w_in.shape[-1]
    rope = rope_tables is not None
    tok = lambda width: pl.BlockSpec((1, tm, width), lambda b, i: (b, i, 0))
    in_specs = [tok(d), _mod_spec(layer, 0, d), _mod_spec(layer, 1, d),
                _layer_resident(layer, (1, d)), _layer_resident(layer, (d, n_in))]
    args = [x, mod, mod, gains, w_in]
    if rope:
        in_specs += [pl.BlockSpec((tm, LANES), lambda b, i: (i, 0))] * 2
        args += list(rope_tables)
    widths = (ATTN_WIDTH, KV_WIDTH, KV_WIDTH, FOURIER_WIDTH)
    out_specs = [tok(w) for w in widths]
    out_shape = [jax.ShapeDtypeStruct((bx, t, w), BF16) for w in widths]
    out_specs[2] = pl.BlockSpec((1, VT_ROWS, tm), lambda b, i: (b, 0, i))
    out_shape[2] = jax.ShapeDtypeStruct((bx, VT_ROWS, t), BF16)
    scratch = []
    if rope:
        assert t == DFT_A * DFT_C and tm % DFT_C == 0
        out_specs[3] = pl.BlockSpec((1, DFT_C, tm // DFT_C, FOURIER_WIDTH), lambda b, i: (b, 0, i, 0))
        out_shape[3] = jax.ShapeDtypeStruct((bx, DFT_C, DFT_A, FOURIER_WIDTH), BF16)
        scratch = [pltpu.VMEM((FOURIER_WIDTH // LANES, tm, LANES), F32)]
    return pl.pallas_call(
        functools.partial(_proj_kernel, rope=rope, ctx_row=ctx_row),
        grid=(bx, t // tm),
        in_specs=in_specs,
        out_specs=out_specs,
        out_shape=out_shape,
        scratch_shapes=scratch,
        compiler_params=pltpu.CompilerParams(vmem_limit_bytes=VMEM_LIMIT_BYTES),
        name="project_rope" if rope else "project",
    )(*args)


def _attn_kernel(sink_ref, q_ref, *rest, local, n_cast):
    w = WINDOW
    n_in = len(rest) - 1 - 2 * n_cast
    cast_src = rest[n_in:n_in + n_cast]
    cast_dst = rest[n_in + n_cast + 1:]
    rest = rest[:n_in] + rest[n_in + n_cast:n_in + n_cast + 1]
    for src, dst in zip(cast_src, cast_dst):
        dst[...] = src[...].astype(BF16)
    if local:
        kp_ref, kc_ref, kn_ref, vp_ref, vc_ref, vn_ref, kx_ref, vx_ref, o_ref = rest
        k_all = jnp.concatenate([kp_ref[0], kc_ref[0], kn_ref[0]], axis=0)
        vt_all = jnp.concatenate([vp_ref[0], vc_ref[0], vn_ref[0]], axis=1)
        step = pl.program_id(1)
        last = pl.num_programs(1) - 1
    else:
        kx_ref, vx_ref, o_ref = rest
    kx = kx_ref[0]
    vxt = vx_ref[0]
    n_sub = q_ref.shape[1] // w
    cols = Q_PER_KV * w
    group = lax.shift_right_logical(lax.broadcasted_iota(jnp.int32, (1, cols), 1), int(math.log2(w)))
    if local:
        kj = lax.broadcasted_iota(jnp.int32, (w, cols), 0)
        qi = lax.broadcasted_iota(jnp.int32, (w, cols), 1) & (w - 1)
        far = 1 << 20
        bias_prev = jnp.where(kj >= qi, 0.0, NEG_INF)
        bias_next = jnp.where(kj <= qi, 0.0, NEG_INF)
        bias_prev_first = jnp.where(kj >= qi + jnp.where(step > 0, 0, far), 0.0, NEG_INF)
        bias_next_last = jnp.where(kj <= qi - jnp.where(step < last, 0, far), 0.0, NEG_INF)
    contract_last = (((1,), (1,)), ((), ()))

    def sink_row(h):
        sink = jnp.full((1, cols), sink_ref[Q_PER_KV * h] * LOG2_E, F32)
        for g in range(1, Q_PER_KV):
            sink = jnp.where(group == g, sink_ref[Q_PER_KV * h + g] * LOG2_E, sink)
        return sink

    sinks = [sink_row(h) for h in range(N_KV_HEADS)]

    def scores(h, s):
        hs = slice(h * HEAD_DIM, (h + 1) * HEAD_DIM)
        q = q_ref[0, s * w:(s + 1) * w, :]
        qs = jnp.concatenate([q[:, (Q_PER_KV * h + g) * HEAD_DIM:(Q_PER_KV * h + g + 1) * HEAD_DIM]
                              for g in range(Q_PER_KV)], axis=0)
        pieces = []
        if local:
            sl = lax.dot_general(k_all[s * w:(s + 3) * w, hs], qs, contract_last, preferred_element_type=F32)
            pieces += [sl[:w] + (bias_prev_first if s == 0 else bias_prev), sl[w:2 * w],
                       sl[2 * w:] + (bias_next_last if s == n_sub - 1 else bias_next)]
        pieces.append(lax.dot_general(kx[:, hs], qs, contract_last, preferred_element_type=F32))
        m = sinks[h]
        for x in pieces:
            m = jnp.maximum(m, jnp.max(x, axis=0, keepdims=True))
        return pieces, m

    def finish(h, s, pieces, m):
        vs = slice(h * VT_HEAD_ROWS, (h + 1) * VT_HEAD_ROWS)
        probs = [jnp.exp2(x - m).astype(BF16) for x in pieces]
        ot = jnp.dot(vxt[vs, :], probs[-1], preferred_element_type=F32)
        if local:
            ot = ot + jnp.dot(vt_all[vs, s * w:(s + 3) * w], jnp.concatenate(probs[:3], axis=0),
                              preferred_element_type=F32)
        denom = ot[HEAD_DIM:HEAD_DIM + 1] + jnp.exp2(sinks[h] - m)
        ot = ot[:HEAD_DIM] / denom
        o = jnp.concatenate([ot[:, g * w:(g + 1) * w].T for g in range(Q_PER_KV)], axis=1)
        lo = h * Q_PER_KV * HEAD_DIM
        o_ref[0, s * w:(s + 1) * w, lo:lo + Q_PER_KV * HEAD_DIM] = o.astype(BF16)

    work = [(h, s) for h in range(N_KV_HEADS) for s in range(n_sub)]
    queue = [scores(*item) for item in work[:ATTN_LOOKAHEAD]]
    for i, (h, s) in enumerate(work):
        if i + ATTN_LOOKAHEAD < len(work):
            queue.append(scores(*work[i + ATTN_LOOKAHEAD]))
        finish(h, s, *queue.pop(0))


def _attention(q, k, v, kx, vx, sink, cx, local, casts=()):
    smem = pl.BlockSpec(memory_space=pltpu.SMEM)
    kx_spec = pl.BlockSpec((1, cx, KV_WIDTH), lambda b, n: (0, b, 0))
    vx_spec = pl.BlockSpec((1, VT_ROWS, cx), lambda b, n: (0, 0, b))
    if local:
        bx, t, _ = q.shape
        tq = min(ATTN_Q_TILE, t)
        q_spec = pl.BlockSpec((1, tq, ATTN_WIDTH), lambda b, n: (b, n, 0))
    else:
        bx, t, tq = q.shape[1] // cx, cx, cx
        q_spec = pl.BlockSpec((1, tq, ATTN_WIDTH), lambda b, n: (0, b, 0))
    nb = t // tq
    if local:
        per = tq // WINDOW
        n_win = t // WINDOW
        before = lambda n: jnp.maximum(n * per - 1, 0)
        after = lambda n: jnp.minimum((n + 1) * per, n_win - 1)
        k_specs = [pl.BlockSpec((1, WINDOW, KV_WIDTH), lambda b, n: (b, before(n), 0)),
                   pl.BlockSpec((1, tq, KV_WIDTH), lambda b, n: (b, n, 0)),
                   pl.BlockSpec((1, WINDOW, KV_WIDTH), lambda b, n: (b, after(n), 0))]
        v_specs = [pl.BlockSpec((1, VT_ROWS, WINDOW), lambda b, n: (b, 0, before(n))),
                   pl.BlockSpec((1, VT_ROWS, tq), lambda b, n: (b, 0, n)),
                   pl.BlockSpec((1, VT_ROWS, WINDOW), lambda b, n: (b, 0, after(n)))]
        in_specs = [smem, q_spec] + k_specs + v_specs + [kx_spec, vx_spec]
        args = [sink, q, k, k, k, v, v, v, kx, vx]
    else:
        in_specs = [smem, q_spec, kx_spec, vx_spec]
        args = [sink, q, kx, vx]
    out_specs = [q_spec]
    out_shape = [jax.ShapeDtypeStruct(q.shape, BF16)]
    steps = bx * nb
    for w, layer, rows in casts:
        slab = rows // steps
        assert rows % steps == 0 and slab % BF16_SUBLANES == 0
        in_specs.append(pl.BlockSpec((None, slab, w.shape[-1]), lambda b, n, layer=layer: (layer, b * nb + n, 0)))
        out_specs.append(pl.BlockSpec((slab, w.shape[-1]), lambda b, n: (b * nb + n, 0)))
        out_shape.append(jax.ShapeDtypeStruct((rows, w.shape[-1]), BF16))
        args.append(w)
    outs = pl.pallas_call(
        functools.partial(_attn_kernel, local=local, n_cast=len(casts)),
        grid=(bx, nb),
        in_specs=in_specs,
        out_specs=out_specs,
        out_shape=out_shape,
        compiler_params=pltpu.CompilerParams(vmem_limit_bytes=VMEM_LIMIT_BYTES),
        name="attn_window" if local else "attn_context",
    )(*args)
    return outs[0], outs[1:]


def _dft_short_kernel(fa_ref, u_ref, ur_ref, ui_ref):
    y = jnp.dot(fa_ref[...].astype(BF16), u_ref[0], preferred_element_type=F32)
    ur_ref[0] = y[:DFT_A].astype(BF16)
    ui_ref[0] = y[DFT_A:].astype(BF16)


def _dft_long_kernel(fa_ref, g_ref, tr_ref, ti_ref, u_ref, ur_ref, ui_ref, yr_s, yi_s):
    fa = fa_ref[...].astype(BF16)
    for n2 in range(DFT_C):
        y = jnp.dot(fa, u_ref[0, n2], preferred_element_type=F32)
        tr = tr_ref[n2]
        ti = ti_ref[n2]
        for c in range(y.shape[1] // LANES):
            sl = slice(c * LANES, (c + 1) * LANES)
            yr = y[:DFT_A, sl]
            yi = y[DFT_A:, sl]
            yr_s[n2, :, sl] = (yr * tr - yi * ti).astype(BF16)
            yi_s[n2, :, sl] = (yr * ti + yi * tr).astype(BF16)
    g = g_ref[...].astype(BF16)
    half = DFT_C * DFT_C
    for j in range(DFT_A // DFT_C):
        sl = slice(j * DFT_C, (j + 1) * DFT_C)
        y = jnp.concatenate([yr_s[n2, sl, :] for n2 in range(DFT_C)]
                            + [yi_s[n2, sl, :] for n2 in range(DFT_C)], axis=0)
        r = jnp.dot(g, y, preferred_element_type=F32).astype(BF16)
        for k2 in range(DFT_C):
            ur_ref[0, k2, sl, :] = r[k2 * DFT_C:(k2 + 1) * DFT_C]
            ui_ref[0, k2, sl, :] = r[half + k2 * DFT_C:half + (k2 + 1) * DFT_C]


def _dft_tables(n_pos):
    a = np.arange(DFT_A)
    ang = 2.0 * np.pi * ((a[:, None] * a[None, :]) % DFT_A) / DFT_A
    norm = 1.0 / math.sqrt(n_pos * FOURIER_GROUP_DIM)
    fa = jnp.asarray(np.concatenate([np.cos(ang), -np.sin(ang)], axis=0) * norm, F32)
    if n_pos == DFT_A:
        return fa, None, None, None
    assert n_pos == DFT_A * DFT_C
    n2 = np.arange(DFT_C)
    tw = 2.0 * np.pi * (n2[:, None] * a[None, :]) / n_pos
    tr = jnp.asarray(np.repeat(np.cos(tw)[:, :, None], LANES, axis=2), F32)
    ti = jnp.asarray(np.repeat(-np.sin(tw)[:, :, None], LANES, axis=2), F32)
    ang16 = 2.0 * np.pi * ((n2[:, None] * n2[None, :]) % DFT_C) / DFT_C
    c16, s16 = np.cos(ang16), np.sin(ang16)
    eye = np.eye(DFT_C)
    kron = lambda f: np.einsum("kn,ab->kanb", f, eye).reshape(DFT_C * DFT_C, DFT_C * DFT_C)
    gmat = np.block([[kron(c16), kron(s16)], [kron(-s16), kron(c16)]])
    return fa, tr, ti, jnp.asarray(gmat, F32)


def _position_dft(u):
    bx, w = u.shape[0], u.shape[-1]
    n_pos = DFT_A if u.ndim == 3 else math.prod(u.shape[1:-1])
    fa, tr, ti, gmat = _dft_tables(n_pos)
    fa_spec = _resident((2 * DFT_A, DFT_A))
    if u.ndim == 3:
        blk = pl.BlockSpec((1, DFT_A, w), lambda b: (0, b, 0))
        out = jax.ShapeDtypeStruct(u.shape, BF16)
        return pl.pallas_call(
            _dft_short_kernel,
            grid=(u.shape[1] // DFT_A,), in_specs=[fa_spec, blk], out_specs=[blk, blk], out_shape=[out, out],
            name="dft_context",
        )(fa, u)
    rows = DFT_C * DFT_C
    blk = pl.BlockSpec((1, DFT_C, DFT_A, w), lambda b: (b, 0, 0, 0))
    out = jax.ShapeDtypeStruct((bx, DFT_C, DFT_A, w), BF16)
    ur, ui = pl.pallas_call(
        _dft_long_kernel,
        grid=(bx,),
        in_specs=[fa_spec, _resident((2 * rows, 2 * rows)), _resident(tr.shape), _resident(ti.shape), blk],
        out_specs=[blk, blk], out_shape=[out, out],
        scratch_shapes=[pltpu.VMEM((DFT_C, DFT_A, w), BF16)] * 2,
        compiler_params=pltpu.CompilerParams(vmem_limit_bytes=VMEM_LIMIT_BYTES),
        name="dft_long",
    )(fa, gmat, tr, ti, u)
    return ur.reshape(bx, n_pos, w), ui.reshape(bx, n_pos, w)


def _tail_kernel(a_ref, ur_ref, ui_ref, x_ref, gate1_ref, sh_ref, sc_ref, gate2_ref,
                 gpost_mix_ref, gpre_ref, gpost_ref, wa_ref, wr_ref, wi_ref, wg_ref, wu_ref, wd_ref, o_ref,
                 *, ctx_row):
    gate1, shift, scale, gate2 = (_mod_row(r, ctx_row) for r in (gate1_ref, sh_ref, sc_ref, gate2_ref))
    gain_mix = gate1 * gpost_mix_ref[...]
    gain_in = (1.0 + scale) * gpre_ref[...]
    gain_out = gate2 * gpost_ref[...]
    hidden = wg_ref.shape[1]
    tm = x_ref.shape[1]
    per = min(TAIL_SLICE_ROWS, tm)
    assert tm % per == 0

    def mix_dots(rows):
        mix = jnp.dot(a_ref[0, rows, :], wa_ref[...], preferred_element_type=F32)
        mix += jnp.dot(ur_ref[0, rows, :], wr_ref[...], preferred_element_type=F32)
        mix += jnp.dot(ui_ref[0, rows, :], wi_ref[...], preferred_element_type=F32)
        return mix

    def norms(rows, mix):
        x_mid = x_ref[0, rows, :] + _rms(mix, gain_mix)
        return x_mid, (_rms(x_mid, gain_in) + shift).astype(BF16)

    def ffn(h):
        acc = jnp.zeros((h.shape[0], o_ref.shape[2]), F32)
        for lo in range(0, hidden, FFN_CHUNK):
            hi = min(lo + FFN_CHUNK, hidden)
            g = jnp.dot(h, wg_ref[:, lo:hi], preferred_element_type=F32)
            u = jnp.dot(h, wu_ref[:, lo:hi], preferred_element_type=F32)
            a = (g * jax.nn.sigmoid(g) * u).astype(BF16)
            acc += jnp.dot(a, wd_ref[lo:hi, :], preferred_element_type=F32)
        return acc

    def epilogue(rows, x_mid, acc):
        o_ref[0, rows, :] = x_mid + _rms(acc, gain_out)

    slices = [slice(r, r + per) for r in range(0, tm, per)]
    mixes = [mix_dots(rows) for rows in slices]
    staged = norms(slices[0], mixes[0])
    pending = None
    for i, rows in enumerate(slices):
        x_mid, h = staged
        acc = ffn(h)
        if i + 1 < len(slices):
            staged = norms(slices[i + 1], mixes[i + 1])
        if pending is not None:
            epilogue(*pending)
        pending = (rows, x_mid, acc)
    epilogue(*pending)


def _tail(layer, attn, ur, ui, x, mod, g_post_mix, g_pre_ffn, g_post_ffn, w_out, w_four_r, w_four_i,
          wg, wu, wd, ctx_row=None):
    bx, t, d = x.shape
    tm = min(TAIL_TILE, max(t // 2, TAIL_SLICE_ROWS) if bx == 1 else t)
    hidden = wg.shape[-1]
    tok = lambda width: pl.BlockSpec((1, tm, width), lambda b, i: (b, i, 0))
    gain = _layer_resident(layer, (1, d))
    wfour = _layer_resident(layer, (FOURIER_WIDTH, d))
    return pl.pallas_call(
        functools.partial(_tail_kernel, ctx_row=ctx_row),
        grid=(bx, t // tm),
        in_specs=[tok(ATTN_WIDTH), tok(FOURIER_WIDTH), tok(FOURIER_WIDTH), tok(d),
                  _mod_spec(layer, 2, d), _mod_spec(layer, 3, d), _mod_spec(layer, 4, d), _mod_spec(layer, 5, d),
                  gain, gain, gain,
                  _resident((ATTN_WIDTH, d)), wfour, wfour,
                  _resident((d, hidden)), _resident((d, hidden)), _resident((hidden, d))],
        out_specs=tok(d),
        out_shape=jax.ShapeDtypeStruct((bx, t, d), F32),
        compiler_params=pltpu.CompilerParams(vmem_limit_bytes=VMEM_LIMIT_BYTES),
        name="tail",
    )(attn, ur, ui, x, mod, mod, mod, mod, g_post_mix, g_pre_ffn, g_post_ffn,
      w_out, w_four_r, w_four_i, wg, wu, wd)


def _rope_tables(seq):
    t = np.arange(seq)
    freqs = ROPE_THETA ** (-np.arange(ROPE_PAIRS_PER_AXIS, dtype=np.float64) / ROPE_PAIRS_PER_AXIS)
    ang = np.concatenate([(t // GRID_W)[:, None] * freqs, (t % GRID_W)[:, None] * freqs], axis=-1)
    cos = np.repeat(np.cos(ang), 2, axis=-1)
    sin = np.repeat(np.sin(ang), 2, axis=-1) * np.tile([-1.0, 1.0], HEAD_DIM // 2)
    heads_per_row = LANES // HEAD_DIM
    return (jnp.asarray(np.tile(cos, (1, heads_per_row)), F32), jnp.asarray(np.tile(sin, (1, heads_per_row)), F32))


def kernel(x, c, ctx, c_ctx, w_ada, b_ada, norm_pre_mix, norm_post_mix, norm_pre_ffn, norm_post_ffn,
           w_in, w_out, w_fourier, sink, w_gate, w_up, w_down):
    batch, seq, d = x.shape
    depth = w_ada.shape[0]
    assert batch + 1 <= MOD_ROWS

    cc = jnp.zeros((MOD_ROWS, d), F32).at[:batch].set(c).at[batch].set(c_ctx)
    mod = _ada(cc, w_ada, b_ada)
    w_four_r, w_four_i = _fold_fourier_weights(w_fourier, w_out)
    rope = _rope_tables(seq)

    q_scale = np.ones((w_in.shape[-1],), np.float32)
    q_scale[:ATTN_WIDTH] = HEAD_DIM ** -0.5 * LOG2_E
    w_in_b = (w_in * q_scale).astype(BF16)
    gains = [g.reshape(depth, 1, d) for g in (norm_pre_mix, norm_post_mix, norm_pre_ffn, norm_post_ffn)]
    g_pre_mix, g_post_mix, g_pre_ffn, g_post_ffn = gains

    ctx_len = ctx.shape[1]
    assert ctx_len == DFT_A
    xc = ctx.reshape(1, batch * ctx_len, d)
    for i in range(depth):
        q, k, v, u = _project(i, x, mod, g_pre_mix, w_in_b, rope)
        qc, kc, vc, uc = _project(i, xc, mod, g_pre_mix, w_in_b, None, ctx_row=batch)
        casts = [(w_out, i, ATTN_WIDTH), (w_gate, i, d), (w_up, i, d), (w_down, i, w_down.shape[1])]
        attn, tail_weights = _attention(q, k, v, kc, vc, sink[i], ctx_len, local=True, casts=casts)
        w_attn_out, wg, wu, wd = tail_weights

        def finish(stream, attn, ur, ui, ctx_row):
            return _tail(i, attn, ur, ui, stream, mod, g_post_mix, g_pre_ffn, g_post_ffn,
                         w_attn_out, w_four_r, w_four_i, wg, wu, wd, ctx_row)

        ur, ui = _position_dft(u)
        x = finish(x, attn, ur, ui, None)

        if i < depth - 1:
            attn_c, _ = _attention(qc, None, None, kc, vc, sink[i], ctx_len, local=False)
            urc, uic = _position_dft(uc)
            xc = finish(xc, attn_c, urc, uic, batch)
    return x
```

```python
import functools
import math

import numpy as np
import jax
import jax.numpy as jnp
from jax import lax
from jax.experimental import pallas as pl
from jax.experimental.pallas import tpu as pltpu

F32 = jnp.float32
BF16 = jnp.bfloat16

GRID_W = 64
HEAD_DIM = 64
N_Q_HEADS = 8
N_KV_HEADS = 2
Q_PER_KV = N_Q_HEADS // N_KV_HEADS
ATTN_WIDTH = N_Q_HEADS * HEAD_DIM
KV_WIDTH = N_KV_HEADS * HEAD_DIM
N_FOURIER_GROUPS = 8
FOURIER_GROUP_DIM = 64
FOURIER_WIDTH = N_FOURIER_GROUPS * FOURIER_GROUP_DIM
WINDOW = 128
ROPE_THETA = 10000.0
ROPE_PAIRS_PER_AXIS = HEAD_DIM // 4
RMS_EPS = 1e-6
N_MOD = 6
NEG_INF = -1e30
LOG2_E = math.log2(math.e)

LANES = 128
BF16_SUBLANES = 16
VMEM_LIMIT_BYTES = 60 * 1024 * 1024

DFT_A = 256
DFT_C = 16
DFT_CK = 8

MOD_ROWS = 8
ADA_COLS = 768
PROJ_TILE = 1024
PROJ_SLICE_ROWS = 256
FFN_CHUNK = 512
TAIL_TILE = 1024
TAIL_SLICE_ROWS = 512
ATTN_Q_TILE = 1024
ATTN_LOOKAHEAD = 2
VT_HEAD_ROWS = HEAD_DIM + BF16_SUBLANES
VT_ROWS = N_KV_HEADS * VT_HEAD_ROWS


def _resident(shape):
    zeros = (0,) * len(shape)
    return pl.BlockSpec(shape, lambda *_: zeros, pipeline_mode=pl.Buffered(1))


def _layer_resident(layer, shape, index=None):
    index = (0,) * len(shape) if index is None else index
    return pl.BlockSpec((None,) + tuple(shape), lambda *_: (layer,) + tuple(index), pipeline_mode=pl.Buffered(1))


def _mod_spec(layer, which, d):
    return pl.BlockSpec((None, MOD_ROWS, d), lambda *_: (layer, 0, which))


def _mod_row(ref, ctx_row):
    row = pl.program_id(0) if ctx_row is None else ctx_row
    return ref[pl.ds(row, 1), :]


def _rms(xf, g):
    ms = jnp.mean(xf * xf, axis=-1, keepdims=True)
    return xf * lax.rsqrt(ms + RMS_EPS) * g


def _split_bf16(x):
    hi = x.astype(BF16)
    return hi, (x - hi.astype(F32)).astype(BF16)


def _dot3(a, b):
    (a_hi, a_lo), (b_hi, b_lo) = a, b
    dot = functools.partial(jnp.dot, preferred_element_type=F32)
    return dot(a_hi, b_hi) + dot(a_lo, b_hi) + dot(a_hi, b_lo)


def _ada_kernel(c_ref, w_ref, b_ref, o_ref):
    c = c_ref[...]
    a = c * jax.nn.sigmoid(c)
    a_hi, a_lo = _split_bf16(a)
    w_hi, w_lo = _split_bf16(w_ref[0])
    rows = a.shape[0]
    r = jnp.dot(jnp.concatenate([a_hi, a_lo], axis=0), w_hi, preferred_element_type=F32)
    o_ref[0] = r[:rows] + r[rows:] + jnp.dot(a_hi, w_lo, preferred_element_type=F32) + b_ref[0]


def _ada(cc, w_ada, b_ada):
    depth, d, n = w_ada.shape
    tn = ADA_COLS
    assert n % tn == 0
    return pl.pallas_call(
        _ada_kernel,
        grid=(depth, n // tn),
        in_specs=[
            pl.BlockSpec((MOD_ROWS, d), lambda i, j: (0, 0)),
            pl.BlockSpec((1, d, tn), lambda i, j: (i, 0, j)),
            pl.BlockSpec((1, 1, tn), lambda i, j: (i, 0, j)),
        ],
        out_specs=pl.BlockSpec((1, MOD_ROWS, tn), lambda i, j: (i, 0, j)),
        out_shape=jax.ShapeDtypeStruct((depth, MOD_ROWS, n), F32),
        compiler_params=pltpu.CompilerParams(vmem_limit_bytes=VMEM_LIMIT_BYTES),
        name="ada",
    )(cc, w_ada, b_ada.reshape(depth, 1, n))


def _fold_kernel(c64_ref, s64_ref, wf_ref, wo_ref, wr_ref, wi_ref):
    gd = c64_ref.shape[0]
    for g in range(wf_ref.shape[0]):
        rows = slice(g * gd, (g + 1) * gd)
        t = _split_bf16(_dot3(_split_bf16(wf_ref[g]), _split_bf16(wo_ref[rows, :])))
        wr_ref[rows, :] = _dot3(_split_bf16(c64_ref[...]), t).astype(BF16)
        wi_ref[rows, :] = _dot3(_split_bf16(s64_ref[...]), t).astype(BF16)


def _fold_fourier_weights(w_fourier, w_out):
    depth, groups, gd, _ = w_fourier.shape
    d = w_out.shape[-1]
    width = groups * gd
    assert w_out.shape[1] == ATTN_WIDTH + width and ATTN_WIDTH == width
    j = np.arange(gd)
    ang = 2.0 * np.pi * ((j[:, None] * j[None, :]) % gd) / gd
    c64 = jnp.asarray(np.cos(ang), F32)
    s64 = jnp.asarray(np.sin(ang), F32)
    out = jax.ShapeDtypeStruct((depth, width, d), BF16)
    return pl.pallas_call(
        _fold_kernel,
        grid=(depth,),
        in_specs=[
            pl.BlockSpec((gd, gd), lambda i: (0, 0)),
            pl.BlockSpec((gd, gd), lambda i: (0, 0)),
            pl.BlockSpec((None, groups, gd, gd), lambda i: (i, 0, 0, 0)),
            pl.BlockSpec((None, width, d), lambda i: (i, 1, 0)),
        ],
        out_specs=[pl.BlockSpec((None, width, d), lambda i: (i, 0, 0))] * 2,
        out_shape=[out, out],
        name="fold_fourier",
    )(c64, s64, w_fourier, w_out)


def _proj_kernel(x_ref, sh_ref, sc_ref, g_ref, w_ref, *rest, rope, ctx_row):
    if rope:
        cos_ref, sin_ref, q_ref, k_ref, v_ref, u_ref, u_scratch = rest
    else:
        q_ref, k_ref, v_ref, u_ref = rest
    gain = (1.0 + _mod_row(sc_ref, ctx_row)) * g_ref[...]
    shift = _mod_row(sh_ref, ctx_row)
    k_lo = ATTN_WIDTH
    v_lo = ATTN_WIDTH + KV_WIDTH
    u_lo = ATTN_WIDTH + 2 * KV_WIDTH
    tm = x_ref.shape[1]
    per = min(PROJ_SLICE_ROWS, tm)
    assert tm % per == 0 and per % DFT_C == 0
    if rope:
        even_lane = (lax.broadcasted_iota(jnp.int32, (per, LANES), 1) & 1) == 0

    def norm(rows):
        return (_rms(x_ref[0, rows, :], gain) + shift).astype(BF16)

    def emit(rows, p):
        if rope:
            cos = cos_ref[rows, :]
            sin = sin_ref[rows, :]

            def rot(xs):
                partner = jnp.where(even_lane, pltpu.roll(xs, LANES - 1, 1), pltpu.roll(xs, 1, 1))
                return xs * cos + partner * sin

            for j in range(ATTN_WIDTH // LANES):
                q_ref[0, rows, j * LANES:(j + 1) * LANES] = rot(p[:, j * LANES:(j + 1) * LANES]).astype(BF16)
            k_ref[0, rows, :] = rot(p[:, k_lo:v_lo]).astype(BF16)
        else:
            q_ref[0, rows, :] = p[:, :k_lo].astype(BF16)
            k_ref[0, rows, :] = p[:, k_lo:v_lo].astype(BF16)
        vt = p[:, v_lo:u_lo].T.astype(BF16)
        ones = jnp.ones((VT_HEAD_ROWS - HEAD_DIM, per), BF16)
        for h_kv in range(N_KV_HEADS):
            v_ref[0, h_kv * VT_HEAD_ROWS:h_kv * VT_HEAD_ROWS + HEAD_DIM, rows] = (
                vt[h_kv * HEAD_DIM:(h_kv + 1) * HEAD_DIM])
            v_ref[0, h_kv * VT_HEAD_ROWS + HEAD_DIM:(h_kv + 1) * VT_HEAD_ROWS, rows] = ones
        if rope:
            n1 = slice(rows.start // DFT_C, rows.stop // DFT_C)
            for c in range(u_scratch.shape[0]):
                u_scratch[c, rows, :] = p[:, u_lo + c * LANES:u_lo + (c + 1) * LANES]
                for n2 in range(DFT_C):
                    u_ref[0, n2, n1, c * LANES:(c + 1) * LANES] = (
                        u_scratch[c, pl.ds(rows.start + n2, per // DFT_C, stride=DFT_C), :].astype(BF16))
        else:
            u_ref[0, rows, :] = p[:, u_lo:].astype(BF16)

    slices = [slice(r, r + per) for r in range(0, tm, per)]
    h = norm(slices[0])
    pending = None
    for i, rows in enumerate(slices):
        p = jnp.dot(h, w_ref[...], preferred_element_type=F32)
        if i + 1 < len(slices):
            h = norm(slices[i + 1])
        if pending is not None:
            emit(*pending)
        pending = (rows, p)
    emit(*pending)


def _project(layer, x, mod, gains, w_in, rope_tables, ctx_row=None):
    bx, t, d = x.shape
    tm = min(PROJ_TILE, t)
    n_in = w_in.shape[-1]
    rope = rope_tables is not None
    tok = lambda width: pl.BlockSpec((1, tm, width), lambda b, i: (b, i, 0))
    in_specs = [tok(d), _mod_spec(layer, 0, d), _mod_spec(layer, 1, d),
                _layer_resident(layer, (1, d)), _layer_resident(layer, (d, n_in))]
    args = [x, mod, mod, gains, w_in]
    if rope:
        in_specs += [pl.BlockSpec((tm, LANES), lambda b, i: (i, 0))] * 2
        args += list(rope_tables)
    widths = (ATTN_WIDTH, KV_WIDTH, KV_WIDTH, FOURIER_WIDTH)
    out_specs = [tok(w) for w in widths]
    out_shape = [jax.ShapeDtypeStruct((bx, t, w), BF16) for w in widths]
    out_specs[2] = pl.BlockSpec((1, VT_ROWS, tm), lambda b, i: (b, 0, i))
    out_shape[2] = jax.ShapeDtypeStruct((bx, VT_ROWS, t), BF16)
    scratch = []
    if rope:
        assert t == DFT_A * DFT_C and tm % DFT_C == 0
        out_specs[3] = pl.BlockSpec((1, DFT_C, tm // DFT_C, FOURIER_WIDTH), lambda b, i: (b, 0, i, 0))
        out_shape[3] = jax.ShapeDtypeStruct((bx, DFT_C, DFT_A, FOURIER_WIDTH), BF16)
        scratch = [pltpu.VMEM((FOURIER_WIDTH // LANES, tm, LANES), F32)]
    return pl.pallas_call(
        functools.partial(_proj_kernel, rope=rope, ctx_row=ctx_row),
        grid=(bx, t // tm),
        in_specs=in_specs,
        out_specs=out_specs,
        out_shape=out_shape,
        scratch_shapes=scratch,
        compiler_params=pltpu.CompilerParams(vmem_limit_bytes=VMEM_LIMIT_BYTES),
        name="project_rope" if rope else "project",
    )(*args)


def _attn_kernel(sink_ref, q_ref, *rest, local, n_cast):
    w = WINDOW
    n_in = len(rest) - 1 - 2 * n_cast
    cast_src = rest[n_in:n_in + n_cast]
    cast_dst = rest[n_in + n_cast + 1:]
    rest = rest[:n_in] + rest[n_in + n_cast:n_in + n_cast + 1]
    for src, dst in zip(cast_src, cast_dst):
        dst[...] = src[...].astype(BF16)
    if local:
        kp_ref, kc_ref, kn_ref, vp_ref, vc_ref, vn_ref, kx_ref, vx_ref, o_ref = rest
        k_all = jnp.concatenate([kp_ref[0], kc_ref[0], kn_ref[0]], axis=0)
        vt_all = jnp.concatenate([vp_ref[0], vc_ref[0], vn_ref[0]], axis=1)
        step = pl.program_id(1)
        last = pl.num_programs(1) - 1
    else:
        kx_ref, vx_ref, o_ref = rest
    kx = kx_ref[0]
    vxt = vx_ref[0]
    n_sub = q_ref.shape[1] // w
    cols = Q_PER_KV * w
    group = lax.shift_right_logical(lax.broadcasted_iota(jnp.int32, (1, cols), 1), int(math.log2(w)))
    if local:
        kj = lax.broadcasted_iota(jnp.int32, (w, cols), 0)
        qi = lax.broadcasted_iota(jnp.int32, (w, cols), 1) & (w - 1)
        far = 1 << 20
        bias_prev = jnp.where(kj >= qi, 0.0, NEG_INF)
        bias_next = jnp.where(kj <= qi, 0.0, NEG_INF)
        bias_prev_first = jnp.where(kj >= qi + jnp.where(step > 0, 0, far), 0.0, NEG_INF)
        bias_next_last = jnp.where(kj <= qi - jnp.where(step < last, 0, far), 0.0, NEG_INF)
    contract_last = (((1,), (1,)), ((), ()))

    def sink_row(h):
        sink = jnp.full((1, cols), sink_ref[Q_PER_KV * h] * LOG2_E, F32)
        for g in range(1, Q_PER_KV):
            sink = jnp.where(group == g, sink_ref[Q_PER_KV * h + g] * LOG2_E, sink)
        return sink

    sinks = [sink_row(h) for h in range(N_KV_HEADS)]

    def scores(h, s):
        hs = slice(h * HEAD_DIM, (h + 1) * HEAD_DIM)
        q = q_ref[0, s * w:(s + 1) * w, :]
        qs = jnp.concatenate([q[:, (Q_PER_KV * h + g) * HEAD_DIM:(Q_PER_KV * h + g + 1) * HEAD_DIM]
                              for g in range(Q_PER_KV)], axis=0)
        pieces = []
        if local:
            sl = lax.dot_general(k_all[s * w:(s + 3) * w, hs], qs, contract_last, preferred_element_type=F32)
            pieces += [sl[:w] + (bias_prev_first if s == 0 else bias_prev), sl[w:2 * w],
                       sl[2 * w:] + (bias_next_last if s == n_sub - 1 else bias_next)]
        pieces.append(lax.dot_general(kx[:, hs], qs, contract_last, preferred_element_type=F32))
        m = sinks[h]
        for x in pieces:
            m = jnp.maximum(m, jnp.max(x, axis=0, keepdims=True))
        return pieces, m

    def finish(h, s, pieces, m):
        vs = slice(h * VT_HEAD_ROWS, (h + 1) * VT_HEAD_ROWS)
        probs = [jnp.exp2(x - m).astype(BF16) for x in pieces]
        ot = jnp.dot(vxt[vs, :], probs[-1], preferred_element_type=F32)
        if local:
            ot = ot + jnp.dot(vt_all[vs, s * w:(s + 3) * w], jnp.concatenate(probs[:3], axis=0),
                              preferred_element_type=F32)
        denom = ot[HEAD_DIM:HEAD_DIM + 1] + jnp.exp2(sinks[h] - m)
        ot = ot[:HEAD_DIM] / denom
        o = jnp.concatenate([ot[:, g * w:(g + 1) * w].T for g in range(Q_PER_KV)], axis=1)
        lo = h * Q_PER_KV * HEAD_DIM
        o_ref[0, s * w:(s + 1) * w, lo:lo + Q_PER_KV * HEAD_DIM] = o.astype(BF16)

    work = [(h, s) for h in range(N_KV_HEADS) for s in range(n_sub)]
    queue = [scores(*item) for item in work[:ATTN_LOOKAHEAD]]
    for i, (h, s) in enumerate(work):
        if i + ATTN_LOOKAHEAD < len(work):
            queue.append(scores(*work[i + ATTN_LOOKAHEAD]))
        finish(h, s, *queue.pop(0))


def _attention(q, k, v, kx, vx, sink, cx, local, casts=()):
    smem = pl.BlockSpec(memory_space=pltpu.SMEM)
    kx_spec = pl.BlockSpec((1, cx, KV_WIDTH), lambda b, n: (0, b, 0))
    vx_spec = pl.BlockSpec((1, VT_ROWS, cx), lambda b, n: (0, 0, b))
    if local:
        bx, t, _ = q.shape
        tq = min(ATTN_Q_TILE, t)
        q_spec = pl.BlockSpec((1, tq, ATTN_WIDTH), lambda b, n: (b, n, 0))
    else:
        bx, t, tq = q.shape[1] // cx, cx, cx
        q_spec = pl.BlockSpec((1, tq, ATTN_WIDTH), lambda b, n: (0, b, 0))
    nb = t // tq
    if local:
        per = tq // WINDOW
        n_win = t // WINDOW
        before = lambda n: jnp.maximum(n * per - 1, 0)
        after = lambda n: jnp.minimum((n + 1) * per, n_win - 1)
        k_specs = [pl.BlockSpec((1, WINDOW, KV_WIDTH), lambda b, n: (b, before(n), 0)),
                   pl.BlockSpec((1, tq, KV_WIDTH), lambda b, n: (b, n, 0)),
                   pl.BlockSpec((1, WINDOW, KV_WIDTH), lambda b, n: (b, after(n), 0))]
        v_specs = [pl.BlockSpec((1, VT_ROWS, WINDOW), lambda b, n: (b, 0, before(n))),
                   pl.BlockSpec((1, VT_ROWS, tq), lambda b, n: (b, 0, n)),
                   pl.BlockSpec((1, VT_ROWS, WINDOW), lambda b, n: (b, 0, after(n)))]
        in_specs = [smem, q_spec] + k_specs + v_specs + [kx_spec, vx_spec]
        args = [sink, q, k, k, k, v, v, v, kx, vx]
    else:
        in_specs = [smem, q_spec, kx_spec, vx_spec]
        args = [sink, q, kx, vx]
    out_specs = [q_spec]
    out_shape = [jax.ShapeDtypeStruct(q.shape, BF16)]
    steps = bx * nb
    for w, layer, rows in casts:
        slab = rows // steps
        assert rows % steps == 0 and slab % BF16_SUBLANES == 0
        in_specs.append(pl.BlockSpec((None, slab, w.shape[-1]), lambda b, n, layer=layer: (layer, b * nb + n, 0)))
        out_specs.append(pl.BlockSpec((slab, w.shape[-1]), lambda b, n: (b * nb + n, 0)))
        out_shape.append(jax.ShapeDtypeStruct((rows, w.shape[-1]), BF16))
        args.append(w)
    outs = pl.pallas_call(
        functools.partial(_attn_kernel, local=local, n_cast=len(casts)),
        grid=(bx, nb),
        in_specs=in_specs,
        out_specs=out_specs,
        out_shape=out_shape,
        compiler_params=pltpu.CompilerParams(vmem_limit_bytes=VMEM_LIMIT_BYTES),
        name="attn_window" if local else "attn_context",
    )(*args)
    return outs[0], outs[1:]


def _dft_short_kernel(fa_ref, u_ref, ur_ref, ui_ref):
    y = jnp.dot(fa_ref[...].astype(BF16), u_ref[0], preferred_element_type=F32)
    ur_ref[0] = y[:DFT_A].astype(BF16)
    ui_ref[0] = y[DFT_A:].astype(BF16)


def _dft_long_kernel(fa_ref, g_ref, tr_ref, ti_ref, u_ref, ur_ref, ui_ref, yr_s, yi_s):
    kb = DFT_CK
    fa = fa_ref[...].astype(BF16)
    for a in range(DFT_C // 2):
        ys = [jnp.dot(fa, u_ref[0, 2 * a + e], preferred_element_type=F32) for e in range(2)]
        for c in range(ys[0].shape[1] // LANES):
            sl = slice(c * LANES, (c + 1) * LANES)
            re_im = []
            for e in range(2):
                tr, ti = tr_ref[2 * a + e], ti_ref[2 * a + e]
                yr, yi = ys[e][:DFT_A, sl], ys[e][DFT_A:, sl]
                re_im.append(((yr * tr - yi * ti).reshape(DFT_A // kb, kb, LANES),
                              (yr * ti + yi * tr).reshape(DFT_A // kb, kb, LANES)))
            yr_s[:, a, :, sl] = jnp.concatenate([re_im[0][0], re_im[1][0]], axis=1).astype(BF16)
            yi_s[:, a, :, sl] = jnp.concatenate([re_im[0][1], re_im[1][1]], axis=1).astype(BF16)
    g = g_ref[...].astype(BF16)
    half = DFT_C * kb
    w = ur_ref.shape[-1]
    for j in range(DFT_A // (2 * kb)):
        rs = []
        for e in range(2):
            y = jnp.concatenate([yr_s[2 * j + e].reshape(half, w), yi_s[2 * j + e].reshape(half, w)], axis=0)
            rs.append(jnp.dot(g, y, preferred_element_type=F32))
        sl = slice(j * 2 * kb, (j + 1) * 2 * kb)
        for ref, lo in ((ur_ref, 0), (ui_ref, half)):
            ref[0, :, sl, :] = jnp.concatenate(
                [r[lo:lo + half].reshape(DFT_C, kb, w) for r in rs], axis=1).astype(BF16)


def _dft_tables(n_pos):
    a = np.arange(DFT_A)
    ang = 2.0 * np.pi * ((a[:, None] * a[None, :]) % DFT_A) / DFT_A
    norm = 1.0 / math.sqrt(n_pos * FOURIER_GROUP_DIM)
    fa = jnp.asarray(np.concatenate([np.cos(ang), -np.sin(ang)], axis=0) * norm, F32)
    if n_pos == DFT_A:
        return fa, None, None, None
    assert n_pos == DFT_A * DFT_C
    n2 = np.arange(DFT_C)
    tw = 2.0 * np.pi * (n2[:, None] * a[None, :]) / n_pos
    tr = jnp.asarray(np.repeat(np.cos(tw)[:, :, None], LANES, axis=2), F32)
    ti = jnp.asarray(np.repeat(-np.sin(tw)[:, :, None], LANES, axis=2), F32)
    ang16 = 2.0 * np.pi * ((n2[:, None] * n2[None, :]) % DFT_C) / DFT_C
    c16, s16 = np.cos(ang16), np.sin(ang16)
    eye = np.eye(DFT_CK)
    kron = lambda f: np.einsum("kn,ab->kanb", f, eye).reshape(DFT_C * DFT_CK, DFT_C * DFT_CK)
    gmat = np.block([[kron(c16), kron(s16)], [kron(-s16), kron(c16)]])
    return fa, tr, ti, jnp.asarray(gmat, F32)


def _position_dft(u):
    bx, w = u.shape[0], u.shape[-1]
    n_pos = DFT_A if u.ndim == 3 else math.prod(u.shape[1:-1])
    fa, tr, ti, gmat = _dft_tables(n_pos)
    fa_spec = _resident((2 * DFT_A, DFT_A))
    if u.ndim == 3:
        blk = pl.BlockSpec((1, DFT_A, w), lambda b: (0, b, 0))
        out = jax.ShapeDtypeStruct(u.shape, BF16)
        return pl.pallas_call(
            _dft_short_kernel,
            grid=(u.shape[1] // DFT_A,), in_specs=[fa_spec, blk], out_specs=[blk, blk], out_shape=[out, out],
            name="dft_context",
        )(fa, u)
    blk = pl.BlockSpec((1, DFT_C, DFT_A, w), lambda b: (b, 0, 0, 0))
    out = jax.ShapeDtypeStruct((bx, DFT_C, DFT_A, w), BF16)
    ur, ui = pl.pallas_call(
        _dft_long_kernel,
        grid=(bx,),
        in_specs=[fa_spec, _resident(gmat.shape), _resident(tr.shape), _resident(ti.shape), blk],
        out_specs=[blk, blk], out_shape=[out, out],
        scratch_shapes=[pltpu.VMEM((DFT_A // DFT_CK, DFT_C // 2, 2 * DFT_CK, w), BF16)] * 2,
        compiler_params=pltpu.CompilerParams(vmem_limit_bytes=VMEM_LIMIT_BYTES),
        name="dft_long",
    )(fa, gmat, tr, ti, u)
    return ur.reshape(bx, n_pos, w), ui.reshape(bx, n_pos, w)


def _tail_kernel(a_ref, ur_ref, ui_ref, x_ref, gate1_ref, sh_ref, sc_ref, gate2_ref,
                 gpost_mix_ref, gpre_ref, gpost_ref, wa_ref, wr_ref, wi_ref, wg_ref, wu_ref, wd_ref, o_ref,
                 *, ctx_row):
    gate1, shift, scale, gate2 = (_mod_row(r, ctx_row) for r in (gate1_ref, sh_ref, sc_ref, gate2_ref))
    gain_mix = gate1 * gpost_mix_ref[...]
    gain_in = (1.0 + scale) * gpre_ref[...]
    gain_out = gate2 * gpost_ref[...]
    hidden = wg_ref.shape[1]
    tm = x_ref.shape[1]
    per = min(TAIL_SLICE_ROWS, tm)
    assert tm % per == 0

    def mix_dots(rows):
        mix = jnp.dot(a_ref[0, rows, :], wa_ref[...], preferred_element_type=F32)
        mix += jnp.dot(ur_ref[0, rows, :], wr_ref[...], preferred_element_type=F32)
        mix += jnp.dot(ui_ref[0, rows, :], wi_ref[...], preferred_element_type=F32)
        return mix

    def norms(rows, mix):
        x_mid = x_ref[0, rows, :] + _rms(mix, gain_mix)
        return x_mid, (_rms(x_mid, gain_in) + shift).astype(BF16)

    def ffn(h):
        acc = jnp.zeros((h.shape[0], o_ref.shape[2]), F32)
        for lo in range(0, hidden, FFN_CHUNK):
            hi = min(lo + FFN_CHUNK, hidden)
            g = jnp.dot(h, wg_ref[:, lo:hi], preferred_element_type=F32)
            u = jnp.dot(h, wu_ref[:, lo:hi], preferred_element_type=F32)
            a = (g * jax.nn.sigmoid(g) * u).astype(BF16)
            acc += jnp.dot(a, wd_ref[lo:hi, :], preferred_element_type=F32)
        return acc

    def epilogue(rows, x_mid, acc):
        o_ref[0, rows, :] = x_mid + _rms(acc, gain_out)

    slices = [slice(r, r + per) for r in range(0, tm, per)]
    mixes = [mix_dots(rows) for rows in slices]
    staged = norms(slices[0], mixes[0])
    pending = None
    for i, rows in enumerate(slices):
        x_mid, h = staged
        acc = ffn(h)
        if i + 1 < len(slices):
            staged = norms(slices[i + 1], mixes[i + 1])
        if pending is not None:
            epilogue(*pending)
        pending = (rows, x_mid, acc)
    epilogue(*pending)


def _tail(layer, attn, ur, ui, x, mod, g_post_mix, g_pre_ffn, g_post_ffn, w_out, w_four_r, w_four_i,
          wg, wu, wd, ctx_row=None):
    bx, t, d = x.shape
    tm = min(TAIL_TILE, max(t // 2, TAIL_SLICE_ROWS) if bx == 1 else t)
    hidden = wg.shape[-1]
    tok = lambda width: pl.BlockSpec((1, tm, width), lambda b, i: (b, i, 0))
    gain = _layer_resident(layer, (1, d))
    wfour = _layer_resident(layer, (FOURIER_WIDTH, d))
    return pl.pallas_call(
        functools.partial(_tail_kernel, ctx_row=ctx_row),
        grid=(bx, t // tm),
        in_specs=[tok(ATTN_WIDTH), tok(FOURIER_WIDTH), tok(FOURIER_WIDTH), tok(d),
                  _mod_spec(layer, 2, d), _mod_spec(layer, 3, d), _mod_spec(layer, 4, d), _mod_spec(layer, 5, d),
                  gain, gain, gain,
                  _resident((ATTN_WIDTH, d)), wfour, wfour,
                  _resident((d, hidden)), _resident((d, hidden)), _resident((hidden, d))],
        out_specs=tok(d),
        out_shape=jax.ShapeDtypeStruct((bx, t, d), F32),
        compiler_params=pltpu.CompilerParams(vmem_limit_bytes=VMEM_LIMIT_BYTES),
        name="tail",
    )(attn, ur, ui, x, mod, mod, mod, mod, g_post_mix, g_pre_ffn, g_post_ffn,
      w_out, w_four_r, w_four_i, wg, wu, wd)


def _rope_tables(seq):
    t = np.arange(seq)
    freqs = ROPE_THETA ** (-np.arange(ROPE_PAIRS_PER_AXIS, dtype=np.float64) / ROPE_PAIRS_PER_AXIS)
    ang = np.concatenate([(t // GRID_W)[:, None] * freqs, (t % GRID_W)[:, None] * freqs], axis=-1)
    cos = np.repeat(np.cos(ang), 2, axis=-1)
    sin = np.repeat(np.sin(ang), 2, axis=-1) * np.tile([-1.0, 1.0], HEAD_DIM // 2)
    heads_per_row = LANES // HEAD_DIM
    return (jnp.asarray(np.tile(cos, (1, heads_per_row)), F32), jnp.asarray(np.tile(sin, (1, heads_per_row)), F32))


def kernel(x, c, ctx, c_ctx, w_ada, b_ada, norm_pre_mix, norm_post_mix, norm_pre_ffn, norm_post_ffn,
           w_in, w_out, w_fourier, sink, w_gate, w_up, w_down):
    batch, seq, d = x.shape
    depth = w_ada.shape[0]
    assert batch + 1 <= MOD_ROWS

    cc = jnp.zeros((MOD_ROWS, d), F32).at[:batch].set(c).at[batch].set(c_ctx)
    mod = _ada(cc, w_ada, b_ada)
    w_four_r, w_four_i = _fold_fourier_weights(w_fourier, w_out)
    rope = _rope_tables(seq)

    q_scale = np.ones((w_in.shape[-1],), np.float32)
    q_scale[:ATTN_WIDTH] = HEAD_DIM ** -0.5 * LOG2_E
    w_in_b = (w_in * q_scale).astype(BF16)
    gains = [g.reshape(depth, 1, d) for g in (norm_pre_mix, norm_post_mix, norm_pre_ffn, norm_post_ffn)]
    g_pre_mix, g_post_mix, g_pre_ffn, g_post_ffn = gains

    ctx_len = ctx.shape[1]
    assert ctx_len == DFT_A
    xc = ctx.reshape(1, batch * ctx_len, d)
    for i in range(depth):
        q, k, v, u = _project(i, x, mod, g_pre_mix, w_in_b, rope)
        qc, kc, vc, uc = _project(i, xc, mod, g_pre_mix, w_in_b, None, ctx_row=batch)
        casts = [(w_out, i, ATTN_WIDTH), (w_gate, i, d), (w_up, i, d), (w_down, i, w_down.shape[1])]
        attn, tail_weights = _attention(q, k, v, kc, vc, sink[i], ctx_len, local=True, casts=casts)
        w_attn_out, wg, wu, wd = tail_weights

        def finish(stream, attn, ur, ui, ctx_row):
            return _tail(i, attn, ur, ui, stream, mod, g_post_mix, g_pre_ffn, g_post_ffn,
                         w_attn_out, w_four_r, w_four_i, wg, wu, wd, ctx_row)

        ur, ui = _position_dft(u)
        x = finish(x, attn, ur, ui, None)

        if i < depth - 1:
            attn_c, _ = _attention(qc, None, None, kc, vc, sink[i], ctx_len, local=False)
            urc, uic = _position_dft(uc)
            xc = finish(xc, attn_c, urc, uic, batch)
    return x
```

```python
import functools
import math

import numpy as np
import jax
import jax.numpy as jnp
from jax import lax
from jax.experimental import pallas as pl
from jax.experimental.pallas import tpu as pltpu

F32 = jnp.float32
BF16 = jnp.bfloat16

GRID_W = 64
HEAD_DIM = 64
N_Q_HEADS = 8
N_KV_HEADS = 2
Q_PER_KV = N_Q_HEADS // N_KV_HEADS
ATTN_WIDTH = N_Q_HEADS * HEAD_DIM
KV_WIDTH = N_KV_HEADS * HEAD_DIM
N_FOURIER_GROUPS = 8
FOURIER_GROUP_DIM = 64
FOURIER_WIDTH = N_FOURIER_GROUPS * FOURIER_GROUP_DIM
WINDOW = 128
ROPE_THETA = 10000.0
ROPE_PAIRS_PER_AXIS = HEAD_DIM // 4
RMS_EPS = 1e-6
N_MOD = 6
NEG_INF = -1e30
LOG2_E = math.log2(math.e)

LANES = 128
BF16_SUBLANES = 16
VMEM_LIMIT_BYTES = 60 * 1024 * 1024

DFT_A = 256
DFT_C = 16
DFT_CK = 8

MOD_ROWS = 8
ADA_COLS = 1536
PROJ_TILE = 1024
PROJ_SLICE_ROWS = 256
FFN_CHUNK = 512
TAIL_TILE = 1024
TAIL_SLICE_ROWS = 512
ATTN_Q_TILE = 2048
ATTN_LOOKAHEAD = 2
VT_HEAD_ROWS = HEAD_DIM + BF16_SUBLANES
VT_ROWS = N_KV_HEADS * VT_HEAD_ROWS


def _resident(shape):
    zeros = (0,) * len(shape)
    return pl.BlockSpec(shape, lambda *_: zeros, pipeline_mode=pl.Buffered(1))


def _layer_resident(layer, shape, index=None):
    index = (0,) * len(shape) if index is None else index
    return pl.BlockSpec((None,) + tuple(shape), lambda *_: (layer,) + tuple(index), pipeline_mode=pl.Buffered(1))


def _mod_spec(layer, which, d):
    return pl.BlockSpec((None, MOD_ROWS, d), lambda *_: (layer, 0, which))


def _mod_row(ref, ctx_row):
    row = pl.program_id(0) if ctx_row is None else ctx_row
    return ref[pl.ds(row, 1), :]


def _rms(xf, g):
    ms = jnp.mean(xf * xf, axis=-1, keepdims=True)
    return xf * lax.rsqrt(ms + RMS_EPS) * g


def _split_bf16(x):
    hi = x.astype(BF16)
    return hi, (x - hi.astype(F32)).astype(BF16)


def _dot3(a, b):
    (a_hi, a_lo), (b_hi, b_lo) = a, b
    dot = functools.partial(jnp.dot, preferred_element_type=F32)
    return dot(a_hi, b_hi) + dot(a_lo, b_hi) + dot(a_hi, b_lo)


def _ada_kernel(c_ref, w_ref, b_ref, o_ref):
    c = c_ref[...]
    a = c * jax.nn.sigmoid(c)
    a_hi, a_lo = _split_bf16(a)
    w_hi, w_lo = _split_bf16(w_ref[0])
    rows = a.shape[0]
    r = jnp.dot(jnp.concatenate([a_hi, a_lo], axis=0), w_hi, preferred_element_type=F32)
    o_ref[0] = r[:rows] + r[rows:] + jnp.dot(a_hi, w_lo, preferred_element_type=F32) + b_ref[0]


def _ada(cc, w_ada, b_ada):
    depth, d, n = w_ada.shape
    tn = ADA_COLS
    assert n % tn == 0
    return pl.pallas_call(
        _ada_kernel,
        grid=(depth, n // tn),
        in_specs=[
            pl.BlockSpec((MOD_ROWS, d), lambda i, j: (0, 0)),
            pl.BlockSpec((1, d, tn), lambda i, j: (i, 0, j)),
            pl.BlockSpec((1, 1, tn), lambda i, j: (i, 0, j)),
        ],
        out_specs=pl.BlockSpec((1, MOD_ROWS, tn), lambda i, j: (i, 0, j)),
        out_shape=jax.ShapeDtypeStruct((depth, MOD_ROWS, n), F32),
        compiler_params=pltpu.CompilerParams(vmem_limit_bytes=VMEM_LIMIT_BYTES),
        name="ada",
    )(cc, w_ada, b_ada.reshape(depth, 1, n))


def _fold_kernel(c64_ref, s64_ref, wf_ref, wo_ref, wr_ref, wi_ref):
    gd = c64_ref.shape[0]
    for g in range(wf_ref.shape[0]):
        rows = slice(g * gd, (g + 1) * gd)
        t = _split_bf16(_dot3(_split_bf16(wf_ref[g]), _split_bf16(wo_ref[rows, :])))
        wr_ref[rows, :] = _dot3(_split_bf16(c64_ref[...]), t).astype(BF16)
        wi_ref[rows, :] = _dot3(_split_bf16(s64_ref[...]), t).astype(BF16)


def _fold_fourier_weights(w_fourier, w_out):
    depth, groups, gd, _ = w_fourier.shape
    d = w_out.shape[-1]
    width = groups * gd
    assert w_out.shape[1] == ATTN_WIDTH + width and ATTN_WIDTH == width
    j = np.arange(gd)
    ang = 2.0 * np.pi * ((j[:, None] * j[None, :]) % gd) / gd
    c64 = jnp.asarray(np.cos(ang), F32)
    s64 = jnp.asarray(np.sin(ang), F32)
    out = jax.ShapeDtypeStruct((depth, width, d), BF16)
    return pl.pallas_call(
        _fold_kernel,
        grid=(depth,),
        in_specs=[
            pl.BlockSpec((gd, gd), lambda i: (0, 0)),
            pl.BlockSpec((gd, gd), lambda i: (0, 0)),
            pl.BlockSpec((None, groups, gd, gd), lambda i: (i, 0, 0, 0)),
            pl.BlockSpec((None, width, d), lambda i: (i, 1, 0)),
        ],
        out_specs=[pl.BlockSpec((None, width, d), lambda i: (i, 0, 0))] * 2,
        out_shape=[out, out],
        name="fold_fourier",
    )(c64, s64, w_fourier, w_out)


def _proj_kernel(x_ref, sh_ref, sc_ref, g_ref, w_ref, *rest, rope, ctx_row):
    if rope:
        cos_ref, sin_ref, q_ref, k_ref, v_ref, u_ref, u_scratch = rest
    else:
        q_ref, k_ref, v_ref, u_ref = rest
    gain = (1.0 + _mod_row(sc_ref, ctx_row)) * g_ref[...]
    shift = _mod_row(sh_ref, ctx_row)
    k_lo = ATTN_WIDTH
    v_lo = ATTN_WIDTH + KV_WIDTH
    u_lo = ATTN_WIDTH + 2 * KV_WIDTH
    tm = x_ref.shape[1]
    per = min(PROJ_SLICE_ROWS, tm)
    assert tm % per == 0 and per % DFT_C == 0
    if rope:
        even_lane = (lax.broadcasted_iota(jnp.int32, (per, LANES), 1) & 1) == 0

    def norm(rows):
        return (_rms(x_ref[0, rows, :], gain) + shift).astype(BF16)

    def emit(rows, p):
        if rope:
            cos = cos_ref[rows, :]
            sin = sin_ref[rows, :]

            def rot(xs):
                partner = jnp.where(even_lane, pltpu.roll(xs, LANES - 1, 1), pltpu.roll(xs, 1, 1))
                return xs * cos + partner * sin

            for j in range(ATTN_WIDTH // LANES):
                q_ref[0, rows, j * LANES:(j + 1) * LANES] = rot(p[:, j * LANES:(j + 1) * LANES]).astype(BF16)
            k_ref[0, rows, :] = rot(p[:, k_lo:v_lo]).astype(BF16)
        else:
            q_ref[0, rows, :] = p[:, :k_lo].astype(BF16)
            k_ref[0, rows, :] = p[:, k_lo:v_lo].astype(BF16)
        vt = p[:, v_lo:u_lo].T.astype(BF16)
        ones = jnp.ones((VT_HEAD_ROWS - HEAD_DIM, per), BF16)
        for h_kv in range(N_KV_HEADS):
            v_ref[0, h_kv * VT_HEAD_ROWS:h_kv * VT_HEAD_ROWS + HEAD_DIM, rows] = (
                vt[h_kv * HEAD_DIM:(h_kv + 1) * HEAD_DIM])
            v_ref[0, h_kv * VT_HEAD_ROWS + HEAD_DIM:(h_kv + 1) * VT_HEAD_ROWS, rows] = ones
        if rope:
            n1 = slice(rows.start // DFT_C, rows.stop // DFT_C)
            for c in range(u_scratch.shape[0]):
                u_scratch[c, rows, :] = p[:, u_lo + c * LANES:u_lo + (c + 1) * LANES]
                for n2 in range(DFT_C):
                    u_ref[0, n2, n1, c * LANES:(c + 1) * LANES] = (
                        u_scratch[c, pl.ds(rows.start + n2, per // DFT_C, stride=DFT_C), :].astype(BF16))
        else:
            u_ref[0, rows, :] = p[:, u_lo:].astype(BF16)

    slices = [slice(r, r + per) for r in range(0, tm, per)]
    h = norm(slices[0])
    pending = None
    for i, rows in enumerate(slices):
        p = jnp.dot(h, w_ref[...], preferred_element_type=F32)
        if i + 1 < len(slices):
            h = norm(slices[i + 1])
        if pending is not None:
            emit(*pending)
        pending = (rows, p)
    emit(*pending)


def _project(layer, x, mod, gains, w_in, rope_tables, ctx_row=None):
    bx, t, d = x.shape
    tm = min(PROJ_TILE, t)
    n_in = w_in.shape[-1]
    rope = rope_tables is not None
    tok = lambda width: pl.BlockSpec((1, tm, width), lambda b, i: (b, i, 0))
    in_specs = [tok(d), _mod_spec(layer, 0, d), _mod_spec(layer, 1, d),
                _layer_resident(layer, (1, d)), _layer_resident(layer, (d, n_in))]
    args = [x, mod, mod, gains, w_in]
    if rope:
        in_specs += [pl.BlockSpec((tm, LANES), lambda b, i: (i, 0))] * 2
        args += list(rope_tables)
    widths = (ATTN_WIDTH, KV_WIDTH, KV_WIDTH, FOURIER_WIDTH)
    out_specs = [tok(w) for w in widths]
    out_shape = [jax.ShapeDtypeStruct((bx, t, w), BF16) for w in widths]
    out_specs[2] = pl.BlockSpec((1, VT_ROWS, tm), lambda b, i: (b, 0, i))
    out_shape[2] = jax.ShapeDtypeStruct((bx, VT_ROWS, t), BF16)
    scratch = []
    if rope:
        assert t == DFT_A * DFT_C and tm % DFT_C == 0
        out_specs[3] = pl.BlockSpec((1, DFT_C, tm // DFT_C, FOURIER_WIDTH), lambda b, i: (b, 0, i, 0))
        out_shape[3] = jax.ShapeDtypeStruct((bx, DFT_C, DFT_A, FOURIER_WIDTH), BF16)
        scratch = [pltpu.VMEM((FOURIER_WIDTH // LANES, tm, LANES), F32)]
    return pl.pallas_call(
        functools.partial(_proj_kernel, rope=rope, ctx_row=ctx_row),
        grid=(bx, t // tm),
        in_specs=in_specs,
        out_specs=out_specs,
        out_shape=out_shape,
        scratch_shapes=scratch,
        compiler_params=pltpu.CompilerParams(vmem_limit_bytes=VMEM_LIMIT_BYTES),
        name="project_rope" if rope else "project",
    )(*args)


def _attn_kernel(sink_ref, q_ref, *rest, local, n_cast):
    w = WINDOW
    n_in = len(rest) - 1 - 2 * n_cast
    cast_src = rest[n_in:n_in + n_cast]
    cast_dst = rest[n_in + n_cast + 1:]
    rest = rest[:n_in] + rest[n_in + n_cast:n_in + n_cast + 1]
    for src, dst in zip(cast_src, cast_dst):
        dst[...] = src[...].astype(BF16)
    if local:
        kp_ref, kc_ref, kn_ref, vp_ref, vc_ref, vn_ref, kx_ref, vx_ref, o_ref = rest
        k_all = jnp.concatenate([kp_ref[0], kc_ref[0], kn_ref[0]], axis=0)
        vt_all = jnp.concatenate([vp_ref[0], vc_ref[0], vn_ref[0]], axis=1)
        step = pl.program_id(1)
        last = pl.num_programs(1) - 1
    else:
        kx_ref, vx_ref, o_ref = rest
    kx = kx_ref[0]
    vxt = vx_ref[0]
    n_sub = q_ref.shape[1] // w
    cols = Q_PER_KV * w
    group = lax.shift_right_logical(lax.broadcasted_iota(jnp.int32, (1, cols), 1), int(math.log2(w)))
    if local:
        kj = lax.broadcasted_iota(jnp.int32, (w, cols), 0)
        qi = lax.broadcasted_iota(jnp.int32, (w, cols), 1) & (w - 1)
        far = 1 << 20
        bias_prev = jnp.where(kj >= qi, 0.0, NEG_INF)
        bias_next = jnp.where(kj <= qi, 0.0, NEG_INF)
        bias_prev_first = jnp.where(kj >= qi + jnp.where(step > 0, 0, far), 0.0, NEG_INF)
        bias_next_last = jnp.where(kj <= qi - jnp.where(step < last, 0, far), 0.0, NEG_INF)
    contract_last = (((1,), (1,)), ((), ()))

    def sink_row(h):
        sink = jnp.full((1, cols), sink_ref[Q_PER_KV * h] * LOG2_E, F32)
        for g in range(1, Q_PER_KV):
            sink = jnp.where(group == g, sink_ref[Q_PER_KV * h + g] * LOG2_E, sink)
        return sink

    sinks = [sink_row(h) for h in range(N_KV_HEADS)]

    def scores(h, s):
        hs = slice(h * HEAD_DIM, (h + 1) * HEAD_DIM)
        q = q_ref[0, s * w:(s + 1) * w, :]
        qs = jnp.concatenate([q[:, (Q_PER_KV * h + g) * HEAD_DIM:(Q_PER_KV * h + g + 1) * HEAD_DIM]
                              for g in range(Q_PER_KV)], axis=0)
        pieces = []
        if local:
            sl = lax.dot_general(k_all[s * w:(s + 3) * w, hs], qs, contract_last, preferred_element_type=F32)
            pieces += [sl[:w] + (bias_prev_first if s == 0 else bias_prev), sl[w:2 * w],
                       sl[2 * w:] + (bias_next_last if s == n_sub - 1 else bias_next)]
        pieces.append(lax.dot_general(kx[:, hs], qs, contract_last, preferred_element_type=F32))
        m = sinks[h]
        for x in pieces:
            m = jnp.maximum(m, jnp.max(x, axis=0, keepdims=True))
        return pieces, m

    def finish(h, s, pieces, m):
        vs = slice(h * VT_HEAD_ROWS, (h + 1) * VT_HEAD_ROWS)
        probs = [jnp.exp2(x - m).astype(BF16) for x in pieces]
        ot = jnp.dot(vxt[vs, :], probs[-1], preferred_element_type=F32)
        if local:
            ot = ot + jnp.dot(vt_all[vs, s * w:(s + 3) * w], jnp.concatenate(probs[:3], axis=0),
                              preferred_element_type=F32)
        denom = ot[HEAD_DIM:HEAD_DIM + 1] + jnp.exp2(sinks[h] - m)
        ot = ot[:HEAD_DIM] / denom
        o = jnp.concatenate([ot[:, g * w:(g + 1) * w].T for g in range(Q_PER_KV)], axis=1)
        lo = h * Q_PER_KV * HEAD_DIM
        o_ref[0, s * w:(s + 1) * w, lo:lo + Q_PER_KV * HEAD_DIM] = o.astype(BF16)

    work = [(h, s) for h in range(N_KV_HEADS) for s in range(n_sub)]
    queue = [scores(*item) for item in work[:ATTN_LOOKAHEAD]]
    for i, (h, s) in enumerate(work):
        if i + ATTN_LOOKAHEAD < len(work):
            queue.append(scores(*work[i + ATTN_LOOKAHEAD]))
        finish(h, s, *queue.pop(0))


def _attention(q, k, v, kx, vx, sink, cx, local, casts=()):
    smem = pl.BlockSpec(memory_space=pltpu.SMEM)
    kx_spec = pl.BlockSpec((1, cx, KV_WIDTH), lambda b, n: (0, b, 0))
    vx_spec = pl.BlockSpec((1, VT_ROWS, cx), lambda b, n: (0, 0, b))
    if local:
        bx, t, _ = q.shape
        tq = min(ATTN_Q_TILE, t)
        q_spec = pl.BlockSpec((1, tq, ATTN_WIDTH), lambda b, n: (b, n, 0))
    else:
        bx, t, tq = q.shape[1] // cx, cx, cx
        q_spec = pl.BlockSpec((1, tq, ATTN_WIDTH), lambda b, n: (0, b, 0))
    nb = t // tq
    if local:
        per = tq // WINDOW
        n_win = t // WINDOW
        before = lambda n: jnp.maximum(n * per - 1, 0)
        after = lambda n: jnp.minimum((n + 1) * per, n_win - 1)
        k_specs = [pl.BlockSpec((1, WINDOW, KV_WIDTH), lambda b, n: (b, before(n), 0)),
                   pl.BlockSpec((1, tq, KV_WIDTH), lambda b, n: (b, n, 0)),
                   pl.BlockSpec((1, WINDOW, KV_WIDTH), lambda b, n: (b, after(n), 0))]
        v_specs = [pl.BlockSpec((1, VT_ROWS, WINDOW), lambda b, n: (b, 0, before(n))),
                   pl.BlockSpec((1, VT_ROWS, tq), lambda b, n: (b, 0, n)),
                   pl.BlockSpec((1, VT_ROWS, WINDOW), lambda b, n: (b, 0, after(n)))]
        in_specs = [smem, q_spec] + k_specs + v_specs + [kx_spec, vx_spec]
        args = [sink, q, k, k, k, v, v, v, kx, vx]
    else:
        in_specs = [smem, q_spec, kx_spec, vx_spec]
        args = [sink, q, kx, vx]
    out_specs = [q_spec]
    out_shape = [jax.ShapeDtypeStruct(q.shape, BF16)]
    steps = bx * nb
    for w, layer, rows in casts:
        slab = rows // steps
        assert rows % steps == 0 and slab % BF16_SUBLANES == 0
        in_specs.append(pl.BlockSpec((None, slab, w.shape[-1]), lambda b, n, layer=layer: (layer, b * nb + n, 0)))
        out_specs.append(pl.BlockSpec((slab, w.shape[-1]), lambda b, n: (b * nb + n, 0)))
        out_shape.append(jax.ShapeDtypeStruct((rows, w.shape[-1]), BF16))
        args.append(w)
    outs = pl.pallas_call(
        functools.partial(_attn_kernel, local=local, n_cast=len(casts)),
        grid=(bx, nb),
        in_specs=in_specs,
        out_specs=out_specs,
        out_shape=out_shape,
        compiler_params=pltpu.CompilerParams(vmem_limit_bytes=VMEM_LIMIT_BYTES),
        name="attn_window" if local else "attn_context",
    )(*args)
    return outs[0], outs[1:]


def _dft_short_kernel(fa_ref, u_ref, ur_ref, ui_ref):
    y = jnp.dot(fa_ref[...].astype(BF16), u_ref[0], preferred_element_type=F32)
    ur_ref[0] = y[:DFT_A].astype(BF16)
    ui_ref[0] = y[DFT_A:].astype(BF16)


def _dft_long_kernel(fa_ref, g_ref, tr_ref, ti_ref, u_ref, ur_ref, ui_ref, yr_s, yi_s):
    kb = DFT_CK
    fa = fa_ref[...].astype(BF16)
    for a in range(DFT_C // 2):
        ys = [jnp.dot(fa, u_ref[0, 2 * a + e], preferred_element_type=F32) for e in range(2)]
        for c in range(ys[0].shape[1] // LANES):
            sl = slice(c * LANES, (c + 1) * LANES)
            re_im = []
            for e in range(2):
                tr, ti = tr_ref[2 * a + e], ti_ref[2 * a + e]
                yr, yi = ys[e][:DFT_A, sl], ys[e][DFT_A:, sl]
                re_im.append(((yr * tr - yi * ti).reshape(DFT_A // kb, kb, LANES),
                              (yr * ti + yi * tr).reshape(DFT_A // kb, kb, LANES)))
            yr_s[:, a, :, sl] = jnp.concatenate([re_im[0][0], re_im[1][0]], axis=1).astype(BF16)
            yi_s[:, a, :, sl] = jnp.concatenate([re_im[0][1], re_im[1][1]], axis=1).astype(BF16)
    g = g_ref[...].astype(BF16)
    half = DFT_C * kb
    w = ur_ref.shape[-1]
    for j in range(DFT_A // (2 * kb)):
        rs = []
        for e in range(2):
            y = jnp.concatenate([yr_s[2 * j + e].reshape(half, w), yi_s[2 * j + e].reshape(half, w)], axis=0)
            rs.append(jnp.dot(g, y, preferred_element_type=F32))
        sl = slice(j * 2 * kb, (j + 1) * 2 * kb)
        for ref, lo in ((ur_ref, 0), (ui_ref, half)):
            ref[0, :, sl, :] = jnp.concatenate(
                [r[lo:lo + half].reshape(DFT_C, kb, w) for r in rs], axis=1).astype(BF16)


def _dft_tables(n_pos):
    a = np.arange(DFT_A)
    ang = 2.0 * np.pi * ((a[:, None] * a[None, :]) % DFT_A) / DFT_A
    norm = 1.0 / math.sqrt(n_pos * FOURIER_GROUP_DIM)
    fa = jnp.asarray(np.concatenate([np.cos(ang), -np.sin(ang)], axis=0) * norm, F32)
    if n_pos == DFT_A:
        return fa, None, None, None
    assert n_pos == DFT_A * DFT_C
    n2 = np.arange(DFT_C)
    tw = 2.0 * np.pi * (n2[:, None] * a[None, :]) / n_pos
    tr = jnp.asarray(np.repeat(np.cos(tw)[:, :, None], LANES, axis=2), F32)
    ti = jnp.asarray(np.repeat(-np.sin(tw)[:, :, None], LANES, axis=2), F32)
    ang16 = 2.0 * np.pi * ((n2[:, None] * n2[None, :]) % DFT_C) / DFT_C
    c16, s16 = np.cos(ang16), np.sin(ang16)
    eye = np.eye(DFT_CK)
    kron = lambda f: np.einsum("kn,ab->kanb", f, eye).reshape(DFT_C * DFT_CK, DFT_C * DFT_CK)
    gmat = np.block([[kron(c16), kron(s16)], [kron(-s16), kron(c16)]])
    return fa, tr, ti, jnp.asarray(gmat, F32)


def _position_dft(u):
    bx, w = u.shape[0], u.shape[-1]
    n_pos = DFT_A if u.ndim == 3 else math.prod(u.shape[1:-1])
    fa, tr, ti, gmat = _dft_tables(n_pos)
    fa_spec = _resident((2 * DFT_A, DFT_A))
    if u.ndim == 3:
        blk = pl.BlockSpec((1, DFT_A, w), lambda b: (0, b, 0))
        out = jax.ShapeDtypeStruct(u.shape, BF16)
        return pl.pallas_call(
            _dft_short_kernel,
            grid=(u.shape[1] // DFT_A,), in_specs=[fa_spec, blk], out_specs=[blk, blk], out_shape=[out, out],
            name="dft_context",
        )(fa, u)
    blk = pl.BlockSpec((1, DFT_C, DFT_A, w), lambda b: (b, 0, 0, 0))
    out = jax.ShapeDtypeStruct((bx, DFT_C, DFT_A, w), BF16)
    ur, ui = pl.pallas_call(
        _dft_long_kernel,
        grid=(bx,),
        in_specs=[fa_spec, _resident(gmat.shape), _resident(tr.shape), _resident(ti.shape), blk],
        out_specs=[blk, blk], out_shape=[out, out],
        scratch_shapes=[pltpu.VMEM((DFT_A // DFT_CK, DFT_C // 2, 2 * DFT_CK, w), BF16)] * 2,
        compiler_params=pltpu.CompilerParams(vmem_limit_bytes=VMEM_LIMIT_BYTES),
        name="dft_long",
    )(fa, gmat, tr, ti, u)
    return ur.reshape(bx, n_pos, w), ui.reshape(bx, n_pos, w)


def _tail_kernel(a_ref, ur_ref, ui_ref, x_ref, gate1_ref, sh_ref, sc_ref, gate2_ref,
                 gpost_mix_ref, gpre_ref, gpost_ref, wa_ref, wr_ref, wi_ref, wg_ref, wu_ref, wd_ref, o_ref,
                 *, ctx_row):
    gate1, shift, scale, gate2 = (_mod_row(r, ctx_row) for r in (gate1_ref, sh_ref, sc_ref, gate2_ref))
    gain_mix = gate1 * gpost_mix_ref[...]
    gain_in = (1.0 + scale) * gpre_ref[...]
    gain_out = gate2 * gpost_ref[...]
    hidden = wg_ref.shape[1]
    tm = x_ref.shape[1]
    per = min(TAIL_SLICE_ROWS, tm)
    assert tm % per == 0

    def mix_dots(rows):
        mix = jnp.dot(a_ref[0, rows, :], wa_ref[...], preferred_element_type=F32)
        mix += jnp.dot(ur_ref[0, rows, :], wr_ref[...], preferred_element_type=F32)
        mix += jnp.dot(ui_ref[0, rows, :], wi_ref[...], preferred_element_type=F32)
        return mix

    def norms(rows, mix):
        x_mid = x_ref[0, rows, :] + _rms(mix, gain_mix)
        return x_mid, (_rms(x_mid, gain_in) + shift).astype(BF16)

    def ffn(h):
        acc = jnp.zeros((h.shape[0], o_ref.shape[2]), F32)
        for lo in range(0, hidden, FFN_CHUNK):
            hi = min(lo + FFN_CHUNK, hidden)
            g = jnp.dot(h, wg_ref[:, lo:hi], preferred_element_type=F32)
            u = jnp.dot(h, wu_ref[:, lo:hi], preferred_element_type=F32)
            a = (g * jax.nn.sigmoid(g) * u).astype(BF16)
            acc += jnp.dot(a, wd_ref[lo:hi, :], preferred_element_type=F32)
        return acc

    def epilogue(rows, x_mid, acc):
        o_ref[0, rows, :] = x_mid + _rms(acc, gain_out)

    slices = [slice(r, r + per) for r in range(0, tm, per)]
    mixes = [mix_dots(rows) for rows in slices]
    staged = norms(slices[0], mixes[0])
    pending = None
    for i, rows in enumerate(slices):
        x_mid, h = staged
        acc = ffn(h)
        if i + 1 < len(slices):
            staged = norms(slices[i + 1], mixes[i + 1])
        if pending is not None:
            epilogue(*pending)
        pending = (rows, x_mid, acc)
    epilogue(*pending)


def _tail(layer, attn, ur, ui, x, mod, g_post_mix, g_pre_ffn, g_post_ffn, w_out, w_four_r, w_four_i,
          wg, wu, wd, ctx_row=None):
    bx, t, d = x.shape
    tm = min(TAIL_TILE, max(t // 2, TAIL_SLICE_ROWS) if bx == 1 else t)
    hidden = wg.shape[-1]
    tok = lambda width: pl.BlockSpec((1, tm, width), lambda b, i: (b, i, 0))
    gain = _layer_resident(layer, (1, d))
    wfour = _layer_resident(layer, (FOURIER_WIDTH, d))
    return pl.pallas_call(
        functools.partial(_tail_kernel, ctx_row=ctx_row),
        grid=(bx, t // tm),
        in_specs=[tok(ATTN_WIDTH), tok(FOURIER_WIDTH), tok(FOURIER_WIDTH), tok(d),
                  _mod_spec(layer, 2, d), _mod_spec(layer, 3, d), _mod_spec(layer, 4, d), _mod_spec(layer, 5, d),
                  gain, gain, gain,
                  _resident((ATTN_WIDTH, d)), wfour, wfour,
                  _resident((d, hidden)), _resident((d, hidden)), _resident((hidden, d))],
        out_specs=tok(d),
        out_shape=jax.ShapeDtypeStruct((bx, t, d), F32),
        compiler_params=pltpu.CompilerParams(vmem_limit_bytes=VMEM_LIMIT_BYTES),
        name="tail",
    )(attn, ur, ui, x, mod, mod, mod, mod, g_post_mix, g_pre_ffn, g_post_ffn,
      w_out, w_four_r, w_four_i, wg, wu, wd)


def _rope_tables(seq):
    t = np.arange(seq)
    freqs = ROPE_THETA ** (-np.arange(ROPE_PAIRS_PER_AXIS, dtype=np.float64) / ROPE_PAIRS_PER_AXIS)
    ang = np.concatenate([(t // GRID_W)[:, None] * freqs, (t % GRID_W)[:, None] * freqs], axis=-1)
    cos = np.repeat(np.cos(ang), 2, axis=-1)
    sin = np.repeat(np.sin(ang), 2, axis=-1) * np.tile([-1.0, 1.0], HEAD_DIM // 2)
    heads_per_row = LANES // HEAD_DIM
    return (jnp.asarray(np.tile(cos, (1, heads_per_row)), F32), jnp.asarray(np.tile(sin, (1, heads_per_row)), F32))


def kernel(x, c, ctx, c_ctx, w_ada, b_ada, norm_pre_mix, norm_post_mix, norm_pre_ffn, norm_post_ffn,
           w_in, w_out, w_fourier, sink, w_gate, w_up, w_down):
    batch, seq, d = x.shape
    depth = w_ada.shape[0]
    assert batch + 1 <= MOD_ROWS

    cc = jnp.zeros((MOD_ROWS, d), F32).at[:batch].set(c).at[batch].set(c_ctx)
    mod = _ada(cc, w_ada, b_ada)
    w_four_r, w_four_i = _fold_fourier_weights(w_fourier, w_out)
    rope = _rope_tables(seq)

    q_scale = np.ones((w_in.shape[-1],), np.float32)
    q_scale[:ATTN_WIDTH] = HEAD_DIM ** -0.5 * LOG2_E
    w_in_b = (w_in * q_scale).astype(BF16)
    gains = [g.reshape(depth, 1, d) for g in (norm_pre_mix, norm_post_mix, norm_pre_ffn, norm_post_ffn)]
    g_pre_mix, g_post_mix, g_pre_ffn, g_post_ffn = gains

    ctx_len = ctx.shape[1]
    assert ctx_len == DFT_A
    xc = ctx.reshape(1, batch * ctx_len, d)
    for i in range(depth):
        q, k, v, u = _project(i, x, mod, g_pre_mix, w_in_b, rope)
        qc, kc, vc, uc = _project(i, xc, mod, g_pre_mix, w_in_b, None, ctx_row=batch)
        casts = [(w_out, i, ATTN_WIDTH), (w_gate, i, d), (w_up, i, d), (w_down, i, w_down.shape[1])]
        attn, tail_weights = _attention(q, k, v, kc, vc, sink[i], ctx_len, local=True, casts=casts)
        w_attn_out, wg, wu, wd = tail_weights

        def finish(stream, attn, ur, ui, ctx_row):
            return _tail(i, attn, ur, ui, stream, mod, g_post_mix, g_pre_ffn, g_post_ffn,
                         w_attn_out, w_four_r, w_four_i, wg, wu, wd, ctx_row)

        ur, ui = _position_dft(u)
        x = finish(x, attn, ur, ui, None)

        if i < depth - 1:
            attn_c, _ = _attention(qc, None, None, kc, vc, sink[i], ctx_len, local=False)
            urc, uic = _position_dft(uc)
            xc = finish(xc, attn_c, urc, uic, batch)
    return x
```

```python
import functools
import math

import numpy as np
import jax
import jax.numpy as jnp
from jax import lax
from jax.experimental import pallas as pl
from jax.experimental.pallas import tpu as pltpu

F32 = jnp.float32
BF16 = jnp.bfloat16

GRID_W = 64
HEAD_DIM = 64
N_Q_HEADS = 8
N_KV_HEADS = 2
Q_PER_KV = N_Q_HEADS // N_KV_HEADS
ATTN_WIDTH = N_Q_HEADS * HEAD_DIM
KV_WIDTH = N_KV_HEADS * HEAD_DIM
N_FOURIER_GROUPS = 8
FOURIER_GROUP_DIM = 64
FOURIER_WIDTH = N_FOURIER_GROUPS * FOURIER_GROUP_DIM
WINDOW = 128
ROPE_THETA = 10000.0
ROPE_PAIRS_PER_AXIS = HEAD_DIM // 4
RMS_EPS = 1e-6
N_MOD = 6
NEG_INF = -1e30
LOG2_E = math.log2(math.e)

LANES = 128
BF16_SUBLANES = 16
VMEM_LIMIT_BYTES = 60 * 1024 * 1024

DFT_A = 256
DFT_C = 16
DFT_CK = 8

MOD_ROWS = 8
ADA_COLS = 1536
PROJ_TILE = 1024
PROJ_SLICE_ROWS = 256
FFN_CHUNK = 512
TAIL_TILE = 1024
TAIL_SLICE_ROWS = 512
ATTN_Q_TILE = 2048
ATTN_LOOKAHEAD = 2
VT_HEAD_ROWS = HEAD_DIM + BF16_SUBLANES
VT_ROWS = N_KV_HEADS * VT_HEAD_ROWS


def _resident(shape):
    zeros = (0,) * len(shape)
    return pl.BlockSpec(shape, lambda *_: zeros, pipeline_mode=pl.Buffered(1))


def _layer_resident(layer, shape, index=None):
    index = (0,) * len(shape) if index is None else index
    return pl.BlockSpec((None,) + tuple(shape), lambda *_: (layer,) + tuple(index), pipeline_mode=pl.Buffered(1))


def _mod_spec(layer, which, d):
    return pl.BlockSpec((None, MOD_ROWS, d), lambda *_: (layer, 0, which))


def _mod_row(ref, ctx_row):
    row = pl.program_id(0) if ctx_row is None else ctx_row
    return ref[pl.ds(row, 1), :]


def _rms(xf, g):
    ms = jnp.mean(xf * xf, axis=-1, keepdims=True)
    return xf * lax.rsqrt(ms + RMS_EPS) * g


def _split_bf16(x):
    hi = x.astype(BF16)
    return hi, (x - hi.astype(F32)).astype(BF16)


def _dot3(a, b):
    (a_hi, a_lo), (b_hi, b_lo) = a, b
    dot = functools.partial(jnp.dot, preferred_element_type=F32)
    return dot(a_hi, b_hi) + dot(a_lo, b_hi) + dot(a_hi, b_lo)


def _ada_kernel(c_ref, w_ref, b_ref, o_ref):
    c = c_ref[...]
    a = c * jax.nn.sigmoid(c)
    a_hi, a_lo = _split_bf16(a)
    w_hi, w_lo = _split_bf16(w_ref[0])
    rows = a.shape[0]
    r = jnp.dot(jnp.concatenate([a_hi, a_lo], axis=0), w_hi, preferred_element_type=F32)
    o_ref[0] = r[:rows] + r[rows:] + jnp.dot(a_hi, w_lo, preferred_element_type=F32) + b_ref[0]


def _ada(cc, w_ada, b_ada):
    depth, d, n = w_ada.shape
    tn = ADA_COLS
    assert n % tn == 0
    return pl.pallas_call(
        _ada_kernel,
        grid=(depth, n // tn),
        in_specs=[
            pl.BlockSpec((MOD_ROWS, d), lambda i, j: (0, 0)),
            pl.BlockSpec((1, d, tn), lambda i, j: (i, 0, j)),
            pl.BlockSpec((1, 1, tn), lambda i, j: (i, 0, j)),
        ],
        out_specs=pl.BlockSpec((1, MOD_ROWS, tn), lambda i, j: (i, 0, j)),
        out_shape=jax.ShapeDtypeStruct((depth, MOD_ROWS, n), F32),
        compiler_params=pltpu.CompilerParams(vmem_limit_bytes=VMEM_LIMIT_BYTES),
        name="ada",
    )(cc, w_ada, b_ada.reshape(depth, 1, n))


def _fold_kernel(c64_ref, s64_ref, wf_ref, wo_ref, wr_ref, wi_ref):
    gd = c64_ref.shape[0]
    for g in range(wf_ref.shape[0]):
        rows = slice(g * gd, (g + 1) * gd)
        t = _split_bf16(_dot3(_split_bf16(wf_ref[g]), _split_bf16(wo_ref[rows, :])))
        wr_ref[rows, :] = _dot3(_split_bf16(c64_ref[...]), t).astype(BF16)
        wi_ref[rows, :] = _dot3(_split_bf16(s64_ref[...]), t).astype(BF16)


def _fold_fourier_weights(w_fourier, w_out):
    depth, groups, gd, _ = w_fourier.shape
    d = w_out.shape[-1]
    width = groups * gd
    assert w_out.shape[1] == ATTN_WIDTH + width and ATTN_WIDTH == width
    j = np.arange(gd)
    ang = 2.0 * np.pi * ((j[:, None] * j[None, :]) % gd) / gd
    c64 = jnp.asarray(np.cos(ang), F32)
    s64 = jnp.asarray(np.sin(ang), F32)
    out = jax.ShapeDtypeStruct((depth, width, d), BF16)
    return pl.pallas_call(
        _fold_kernel,
        grid=(depth,),
        in_specs=[
            pl.BlockSpec((gd, gd), lambda i: (0, 0)),
            pl.BlockSpec((gd, gd), lambda i: (0, 0)),
            pl.BlockSpec((None, groups, gd, gd), lambda i: (i, 0, 0, 0)),
            pl.BlockSpec((None, width, d), lambda i: (i, 1, 0)),
        ],
        out_specs=[pl.BlockSpec((None, width, d), lambda i: (i, 0, 0))] * 2,
        out_shape=[out, out],
        name="fold_fourier",
    )(c64, s64, w_fourier, w_out)


def _proj_kernel(x_ref, sh_ref, sc_ref, g_ref, w_ref, *rest, rope, ctx_row):
    if rope:
        cos_ref, sin_ref, q_ref, k_ref, v_ref, u_ref, u_scratch = rest
    else:
        q_ref, k_ref, v_ref, u_ref = rest
    gain = (1.0 + _mod_row(sc_ref, ctx_row)) * g_ref[...]
    shift = _mod_row(sh_ref, ctx_row)
    k_lo = ATTN_WIDTH
    v_lo = ATTN_WIDTH + KV_WIDTH
    u_lo = ATTN_WIDTH + 2 * KV_WIDTH
    tm = x_ref.shape[1]
    per = min(PROJ_SLICE_ROWS, tm)
    assert tm % per == 0 and per % DFT_C == 0
    if rope:
        even_lane = (lax.broadcasted_iota(jnp.int32, (per, LANES), 1) & 1) == 0

    def norm(rows):
        return (_rms(x_ref[0, rows, :], gain) + shift).astype(BF16)

    def emit(rows, p):
        if rope:
            cos = cos_ref[rows, :]
            sin = sin_ref[rows, :]

            def rot(xs):
                partner = jnp.where(even_lane, pltpu.roll(xs, LANES - 1, 1), pltpu.roll(xs, 1, 1))
                return xs * cos + partner * sin

            for j in range(ATTN_WIDTH // LANES):
                q_ref[0, rows, j * LANES:(j + 1) * LANES] = rot(p[:, j * LANES:(j + 1) * LANES]).astype(BF16)
            k_ref[0, rows, :] = rot(p[:, k_lo:v_lo]).astype(BF16)
        else:
            q_ref[0, rows, :] = p[:, :k_lo].astype(BF16)
            k_ref[0, rows, :] = p[:, k_lo:v_lo].astype(BF16)
        vt = p[:, v_lo:u_lo].T.astype(BF16)
        ones = jnp.ones((VT_HEAD_ROWS - HEAD_DIM, per), BF16)
        for h_kv in range(N_KV_HEADS):
            v_ref[0, h_kv * VT_HEAD_ROWS:h_kv * VT_HEAD_ROWS + HEAD_DIM, rows] = (
                vt[h_kv * HEAD_DIM:(h_kv + 1) * HEAD_DIM])
            v_ref[0, h_kv * VT_HEAD_ROWS + HEAD_DIM:(h_kv + 1) * VT_HEAD_ROWS, rows] = ones
        if rope:
            n1 = slice(rows.start // DFT_C, rows.stop // DFT_C)
            for c in range(u_scratch.shape[0]):
                u_scratch[c, rows, :] = p[:, u_lo + c * LANES:u_lo + (c + 1) * LANES]
                for n2 in range(DFT_C):
                    u_ref[0, n2, n1, c * LANES:(c + 1) * LANES] = (
                        u_scratch[c, pl.ds(rows.start + n2, per // DFT_C, stride=DFT_C), :].astype(BF16))
        else:
            u_ref[0, rows, :] = p[:, u_lo:].astype(BF16)

    slices = [slice(r, r + per) for r in range(0, tm, per)]
    h = norm(slices[0])
    pending = None
    for i, rows in enumerate(slices):
        p = jnp.dot(h, w_ref[...], preferred_element_type=F32)
        if i + 1 < len(slices):
            h = norm(slices[i + 1])
        if pending is not None:
            emit(*pending)
        pending = (rows, p)
    emit(*pending)


def _project(layer, x, mod, gains, w_in, rope_tables, ctx_row=None):
    bx, t, d = x.shape
    tm = min(PROJ_TILE, t)
    n_in = w_in.shape[-1]
    rope = rope_tables is not None
    tok = lambda width: pl.BlockSpec((1, tm, width), lambda b, i: (b, i, 0))
    in_specs = [tok(d), _mod_spec(layer, 0, d), _mod_spec(layer, 1, d),
                _layer_resident(layer, (1, d)), _layer_resident(layer, (d, n_in))]
    args = [x, mod, mod, gains, w_in]
    if rope:
        in_specs += [pl.BlockSpec((tm, LANES), lambda b, i: (i, 0))] * 2
        args += list(rope_tables)
    widths = (ATTN_WIDTH, KV_WIDTH, KV_WIDTH, FOURIER_WIDTH)
    out_specs = [tok(w) for w in widths]
    out_shape = [jax.ShapeDtypeStruct((bx, t, w), BF16) for w in widths]
    out_specs[2] = pl.BlockSpec((1, VT_ROWS, tm), lambda b, i: (b, 0, i))
    out_shape[2] = jax.ShapeDtypeStruct((bx, VT_ROWS, t), BF16)
    scratch = []
    if rope:
        assert t == DFT_A * DFT_C and tm % DFT_C == 0
        out_specs[3] = pl.BlockSpec((1, DFT_C, tm // DFT_C, FOURIER_WIDTH), lambda b, i: (b, 0, i, 0))
        out_shape[3] = jax.ShapeDtypeStruct((bx, DFT_C, DFT_A, FOURIER_WIDTH), BF16)
        scratch = [pltpu.VMEM((FOURIER_WIDTH // LANES, tm, LANES), F32)]
    return pl.pallas_call(
        functools.partial(_proj_kernel, rope=rope, ctx_row=ctx_row),
        grid=(bx, t // tm),
        in_specs=in_specs,
        out_specs=out_specs,
        out_shape=out_shape,
        scratch_shapes=scratch,
        compiler_params=pltpu.CompilerParams(vmem_limit_bytes=VMEM_LIMIT_BYTES),
        name="project_rope" if rope else "project",
    )(*args)


def _attn_kernel(sink_ref, q_ref, *rest, local, n_cast):
    w = WINDOW
    n_in = len(rest) - 1 - 2 * n_cast
    cast_src = rest[n_in:n_in + n_cast]
    cast_dst = rest[n_in + n_cast + 1:]
    rest = rest[:n_in] + rest[n_in + n_cast:n_in + n_cast + 1]
    for src, dst in zip(cast_src, cast_dst):
        dst[...] = src[...].astype(BF16)
    if local:
        kp_ref, kc_ref, kn_ref, vp_ref, vc_ref, vn_ref, kx_ref, vx_ref, o_ref = rest
        k_all = jnp.concatenate([kp_ref[0], kc_ref[0], kn_ref[0]], axis=0)
        vt_all = jnp.concatenate([vp_ref[0], vc_ref[0], vn_ref[0]], axis=1)
        step = pl.program_id(1)
        last = pl.num_programs(1) - 1
    else:
        kx_ref, vx_ref, o_ref = rest
    kx = kx_ref[0]
    vxt = vx_ref[0]
    n_sub = q_ref.shape[1] // w
    cols = Q_PER_KV * w
    group = lax.shift_right_logical(lax.broadcasted_iota(jnp.int32, (1, cols), 1), int(math.log2(w)))
    if local:
        kj = lax.broadcasted_iota(jnp.int32, (w, cols), 0)
        qi = lax.broadcasted_iota(jnp.int32, (w, cols), 1) & (w - 1)
        far = 1 << 20
        bias_prev = jnp.where(kj >= qi, 0.0, NEG_INF)
        bias_next = jnp.where(kj <= qi, 0.0, NEG_INF)
        bias_prev_first = jnp.where(kj >= qi + jnp.where(step > 0, 0, far), 0.0, NEG_INF)
        bias_next_last = jnp.where(kj <= qi - jnp.where(step < last, 0, far), 0.0, NEG_INF)
    contract_last = (((1,), (1,)), ((), ()))

    def sink_row(h):
        sink = jnp.full((1, cols), sink_ref[Q_PER_KV * h] * LOG2_E, F32)
        for g in range(1, Q_PER_KV):
            sink = jnp.where(group == g, sink_ref[Q_PER_KV * h + g] * LOG2_E, sink)
        return sink

    sinks = [sink_row(h) for h in range(N_KV_HEADS)]

    def scores(h, s):
        hs = slice(h * HEAD_DIM, (h + 1) * HEAD_DIM)
        q = q_ref[0, s * w:(s + 1) * w, :]
        qs = jnp.concatenate([q[:, (Q_PER_KV * h + g) * HEAD_DIM:(Q_PER_KV * h + g + 1) * HEAD_DIM]
                              for g in range(Q_PER_KV)], axis=0)
        pieces = []
        if local:
            sl = lax.dot_general(k_all[s * w:(s + 3) * w, hs], qs, contract_last, preferred_element_type=F32)
            pieces += [sl[:w] + (bias_prev_first if s == 0 else bias_prev), sl[w:2 * w],
                       sl[2 * w:] + (bias_next_last if s == n_sub - 1 else bias_next)]
        pieces.append(lax.dot_general(kx[:, hs], qs, contract_last, preferred_element_type=F32))
        m = sinks[h]
        for x in pieces:
            m = jnp.maximum(m, jnp.max(x, axis=0, keepdims=True))
        return pieces, m

    def finish(h, s, pieces, m):
        vs = slice(h * VT_HEAD_ROWS, (h + 1) * VT_HEAD_ROWS)
        probs = [jnp.exp2(x - m).astype(BF16) for x in pieces]
        ot = jnp.dot(vxt[vs, :], probs[-1], preferred_element_type=F32)
        if local:
            ot = ot + jnp.dot(vt_all[vs, s * w:(s + 3) * w], jnp.concatenate(probs[:3], axis=0),
                              preferred_element_type=F32)
        denom = ot[HEAD_DIM:HEAD_DIM + 1] + jnp.exp2(sinks[h] - m)
        ot = ot[:HEAD_DIM] / denom
        o = jnp.concatenate([ot[:, g * w:(g + 1) * w].T for g in range(Q_PER_KV)], axis=1)
        lo = h * Q_PER_KV * HEAD_DIM
        o_ref[0, s * w:(s + 1) * w, lo:lo + Q_PER_KV * HEAD_DIM] = o.astype(BF16)

    work = [(h, s) for h in range(N_KV_HEADS) for s in range(n_sub)]
    queue = [scores(*item) for item in work[:ATTN_LOOKAHEAD]]
    for i, (h, s) in enumerate(work):
        if i + ATTN_LOOKAHEAD < len(work):
            queue.append(scores(*work[i + ATTN_LOOKAHEAD]))
        finish(h, s, *queue.pop(0))


def _attention(q, k, v, kx, vx, sink, cx, local, casts=()):
    smem = pl.BlockSpec(memory_space=pltpu.SMEM)
    kx_spec = pl.BlockSpec((1, cx, KV_WIDTH), lambda b, n: (0, b, 0))
    vx_spec = pl.BlockSpec((1, VT_ROWS, cx), lambda b, n: (0, 0, b))
    if local:
        bx, t, _ = q.shape
        tq = min(ATTN_Q_TILE, t)
        q_spec = pl.BlockSpec((1, tq, ATTN_WIDTH), lambda b, n: (b, n, 0))
    else:
        bx, t, tq = q.shape[1] // cx, cx, cx
        q_spec = pl.BlockSpec((1, tq, ATTN_WIDTH), lambda b, n: (0, b, 0))
    nb = t // tq
    if local:
        per = tq // WINDOW
        n_win = t // WINDOW
        before = lambda n: jnp.maximum(n * per - 1, 0)
        after = lambda n: jnp.minimum((n + 1) * per, n_win - 1)
        k_specs = [pl.BlockSpec((1, WINDOW, KV_WIDTH), lambda b, n: (b, before(n), 0)),
                   pl.BlockSpec((1, tq, KV_WIDTH), lambda b, n: (b, n, 0)),
                   pl.BlockSpec((1, WINDOW, KV_WIDTH), lambda b, n: (b, after(n), 0))]
        v_specs = [pl.BlockSpec((1, VT_ROWS, WINDOW), lambda b, n: (b, 0, before(n))),
                   pl.BlockSpec((1, VT_ROWS, tq), lambda b, n: (b, 0, n)),
                   pl.BlockSpec((1, VT_ROWS, WINDOW), lambda b, n: (b, 0, after(n)))]
        in_specs = [smem, q_spec] + k_specs + v_specs + [kx_spec, vx_spec]
        args = [sink, q, k, k, k, v, v, v, kx, vx]
    else:
        in_specs = [smem, q_spec, kx_spec, vx_spec]
        args = [sink, q, kx, vx]
    out_specs = [q_spec]
    out_shape = [jax.ShapeDtypeStruct(q.shape, BF16)]
    steps = bx * nb
    for w, layer, rows in casts:
        slab = rows // steps
        assert rows % steps == 0 and slab % BF16_SUBLANES == 0
        in_specs.append(pl.BlockSpec((None, slab, w.shape[-1]), lambda b, n, layer=layer: (layer, b * nb + n, 0)))
        out_specs.append(pl.BlockSpec((slab, w.shape[-1]), lambda b, n: (b * nb + n, 0)))
        out_shape.append(jax.ShapeDtypeStruct((rows, w.shape[-1]), BF16))
        args.append(w)
    outs = pl.pallas_call(
        functools.partial(_attn_kernel, local=local, n_cast=len(casts)),
        grid=(bx, nb),
        in_specs=in_specs,
        out_specs=out_specs,
        out_shape=out_shape,
        compiler_params=pltpu.CompilerParams(vmem_limit_bytes=VMEM_LIMIT_BYTES),
        name="attn_window" if local else "attn_context",
    )(*args)
    return outs[0], outs[1:]


def _dft_short_kernel(fa_ref, u_ref, ur_ref, ui_ref):
    y = jnp.dot(fa_ref[...].astype(BF16), u_ref[0], preferred_element_type=F32)
    ur_ref[0] = y[:DFT_A].astype(BF16)
    ui_ref[0] = y[DFT_A:].astype(BF16)


def _dft_long_kernel(fa_ref, g_ref, tr_ref, ti_ref, u_ref, ur_ref, ui_ref, yr_s, yi_s):
    kb = DFT_CK
    fa = fa_ref[...].astype(BF16)
    for a in range(DFT_C // 2):
        ys = [jnp.dot(fa, u_ref[0, 2 * a + e], preferred_element_type=F32) for e in range(2)]
        for c in range(ys[0].shape[1] // LANES):
            sl = slice(c * LANES, (c + 1) * LANES)
            re_im = []
            for e in range(2):
                tr, ti = tr_ref[2 * a + e], ti_ref[2 * a + e]
                yr, yi = ys[e][:DFT_A, sl], ys[e][DFT_A:, sl]
                re_im.append(((yr * tr - yi * ti).reshape(DFT_A // kb, kb, LANES),
                              (yr * ti + yi * tr).reshape(DFT_A // kb, kb, LANES)))
            yr_s[:, a, :, sl] = jnp.concatenate([re_im[0][0], re_im[1][0]], axis=1).astype(BF16)
            yi_s[:, a, :, sl] = jnp.concatenate([re_im[0][1], re_im[1][1]], axis=1).astype(BF16)
    g = g_ref[...].astype(BF16)
    half = DFT_C * kb
    w = ur_ref.shape[-1]
    for j in range(DFT_A // (2 * kb)):
        rs = []
        for e in range(2):
            y = jnp.concatenate([yr_s[2 * j + e].reshape(half, w), yi_s[2 * j + e].reshape(half, w)], axis=0)
            rs.append(jnp.dot(g, y, preferred_element_type=F32))
        sl = slice(j * 2 * kb, (j + 1) * 2 * kb)
        for ref, lo in ((ur_ref, 0), (ui_ref, half)):
            ref[0, :, sl, :] = jnp.concatenate(
                [r[lo:lo + half].reshape(DFT_C, kb, w) for r in rs], axis=1).astype(BF16)


def _dft_tables(n_pos):
    a = np.arange(DFT_A)
    ang = 2.0 * np.pi * ((a[:, None] * a[None, :]) % DFT_A) / DFT_A
    norm = 1.0 / math.sqrt(n_pos * FOURIER_GROUP_DIM)
    fa = jnp.asarray(np.concatenate([np.cos(ang), -np.sin(ang)], axis=0) * norm, F32)
    if n_pos == DFT_A:
        return fa, None, None, None
    assert n_pos == DFT_A * DFT_C
    n2 = np.arange(DFT_C)
    tw = 2.0 * np.pi * (n2[:, None] * a[None, :]) / n_pos
    tr = jnp.asarray(np.repeat(np.cos(tw)[:, :, None], LANES, axis=2), F32)
    ti = jnp.asarray(np.repeat(-np.sin(tw)[:, :, None], LANES, axis=2), F32)
    ang16 = 2.0 * np.pi * ((n2[:, None] * n2[None, :]) % DFT_C) / DFT_C
    c16, s16 = np.cos(ang16), np.sin(ang16)
    eye = np.eye(DFT_CK)
    kron = lambda f: np.einsum("kn,ab->kanb", f, eye).reshape(DFT_C * DFT_CK, DFT_C * DFT_CK)
    gmat = np.block([[kron(c16), kron(s16)], [kron(-s16), kron(c16)]])
    return fa, tr, ti, jnp.asarray(gmat, F32)


def _position_dft(u):
    bx, w = u.shape[0], u.shape[-1]
    n_pos = DFT_A if u.ndim == 3 else math.prod(u.shape[1:-1])
    fa, tr, ti, gmat = _dft_tables(n_pos)
    fa_spec = _resident((2 * DFT_A, DFT_A))
    if u.ndim == 3:
        blk = pl.BlockSpec((1, DFT_A, w), lambda b: (0, b, 0))
        out = jax.ShapeDtypeStruct(u.shape, BF16)
        return pl.pallas_call(
            _dft_short_kernel,
            grid=(u.shape[1] // DFT_A,), in_specs=[fa_spec, blk], out_specs=[blk, blk], out_shape=[out, out],
            name="dft_context",
        )(fa, u)
    blk = pl.BlockSpec((1, DFT_C, DFT_A, w), lambda b: (b, 0, 0, 0))
    out = jax.ShapeDtypeStruct((bx, DFT_C, DFT_A, w), BF16)
    ur, ui = pl.pallas_call(
        _dft_long_kernel,
        grid=(bx,),
        in_specs=[fa_spec, _resident(gmat.shape), _resident(tr.shape), _resident(ti.shape), blk],
        out_specs=[blk, blk], out_shape=[out, out],
        scratch_shapes=[pltpu.VMEM((DFT_A // DFT_CK, DFT_C // 2, 2 * DFT_CK, w), BF16)] * 2,
        compiler_params=pltpu.CompilerParams(vmem_limit_bytes=VMEM_LIMIT_BYTES),
        name="dft_long",
    )(fa, gmat, tr, ti, u)
    return ur.reshape(bx, n_pos, w), ui.reshape(bx, n_pos, w)


def _tail_kernel(a_ref, ur_ref, ui_ref, x_ref, gate1_ref, sh_ref, sc_ref, gate2_ref,
                 gpost_mix_ref, gpre_ref, gpost_ref, wa_ref, wr_ref, wi_ref, wg_hbm, wu_hbm, wd_hbm, o_ref,
                 wg_ref, wu_ref, wd_ref, sems, *, ctx_row):
    gate1, shift, scale, gate2 = (_mod_row(r, ctx_row) for r in (gate1_ref, sh_ref, sc_ref, gate2_ref))
    gain_mix = gate1 * gpost_mix_ref[...]
    gain_in = (1.0 + scale) * gpre_ref[...]
    gain_out = gate2 * gpost_ref[...]
    hidden = wg_ref.shape[1]
    tm = x_ref.shape[1]
    per = min(TAIL_SLICE_ROWS, tm)
    assert tm % per == 0
    chunks = [(lo, min(lo + FFN_CHUNK, hidden)) for lo in range(0, hidden, FFN_CHUNK)]

    def weight_copies(c):
        lo, hi = chunks[c]
        return (pltpu.make_async_copy(wg_hbm.at[:, lo:hi], wg_ref.at[:, lo:hi], sems.at[0, c]),
                pltpu.make_async_copy(wu_hbm.at[:, lo:hi], wu_ref.at[:, lo:hi], sems.at[1, c]),
                pltpu.make_async_copy(wd_hbm.at[lo:hi, :], wd_ref.at[lo:hi, :], sems.at[2, c]))

    def mix_dots(rows):
        mix = jnp.dot(a_ref[0, rows, :], wa_ref[...], preferred_element_type=F32)
        mix += jnp.dot(ur_ref[0, rows, :], wr_ref[...], preferred_element_type=F32)
        mix += jnp.dot(ui_ref[0, rows, :], wi_ref[...], preferred_element_type=F32)
        return mix

    def norms(rows, mix):
        x_mid = x_ref[0, rows, :] + _rms(mix, gain_mix)
        return x_mid, (_rms(x_mid, gain_in) + shift).astype(BF16)

    def ffn(h, wait_weights):
        acc = jnp.zeros((h.shape[0], o_ref.shape[2]), F32)
        for c, (lo, hi) in enumerate(chunks):
            if wait_weights:
                for copy in weight_copies(c):
                    copy.wait()
            g = jnp.dot(h, wg_ref[:, lo:hi], preferred_element_type=F32)
            u = jnp.dot(h, wu_ref[:, lo:hi], preferred_element_type=F32)
            a = (g * jax.nn.sigmoid(g) * u).astype(BF16)
            acc += jnp.dot(a, wd_ref[lo:hi, :], preferred_element_type=F32)
        return acc

    def epilogue(rows, x_mid, acc):
        o_ref[0, rows, :] = x_mid + _rms(acc, gain_out)

    def body(first_step):
        if first_step:
            for c in range(len(chunks)):
                for copy in weight_copies(c):
                    copy.start()
        slices = [slice(r, r + per) for r in range(0, tm, per)]
        mixes = [mix_dots(rows) for rows in slices]
        staged = norms(slices[0], mixes[0])
        pending = None
        for i, rows in enumerate(slices):
            x_mid, h = staged
            acc = ffn(h, wait_weights=first_step and i == 0)
            if i + 1 < len(slices):
                staged = norms(slices[i + 1], mixes[i + 1])
            if pending is not None:
                epilogue(*pending)
            pending = (rows, x_mid, acc)
        epilogue(*pending)

    first = jnp.logical_and(pl.program_id(0) == 0, pl.program_id(1) == 0)
    pl.when(first)(functools.partial(body, True))
    pl.when(jnp.logical_not(first))(functools.partial(body, False))


def _tail(layer, attn, ur, ui, x, mod, g_post_mix, g_pre_ffn, g_post_ffn, w_out, w_four_r, w_four_i,
          wg, wu, wd, ctx_row=None):
    bx, t, d = x.shape
    tm = min(TAIL_TILE, max(t // 2, TAIL_SLICE_ROWS) if bx == 1 else t)
    hidden = wg.shape[-1]
    tok = lambda width: pl.BlockSpec((1, tm, width), lambda b, i: (b, i, 0))
    gain = _layer_resident(layer, (1, d))
    wfour = _layer_resident(layer, (FOURIER_WIDTH, d))
    return pl.pallas_call(
        functools.partial(_tail_kernel, ctx_row=ctx_row),
        grid=(bx, t // tm),
        in_specs=[tok(ATTN_WIDTH), tok(FOURIER_WIDTH), tok(FOURIER_WIDTH), tok(d),
                  _mod_spec(layer, 2, d), _mod_spec(layer, 3, d), _mod_spec(layer, 4, d), _mod_spec(layer, 5, d),
                  gain, gain, gain,
                  _resident((ATTN_WIDTH, d)), wfour, wfour,
                  pl.BlockSpec(memory_space=pl.ANY), pl.BlockSpec(memory_space=pl.ANY),
                  pl.BlockSpec(memory_space=pl.ANY)],
        out_specs=tok(d),
        out_shape=jax.ShapeDtypeStruct((bx, t, d), F32),
        scratch_shapes=[pltpu.VMEM((d, hidden), BF16), pltpu.VMEM((d, hidden), BF16),
                        pltpu.VMEM((hidden, d), BF16),
                        pltpu.SemaphoreType.DMA((3, pl.cdiv(hidden, FFN_CHUNK)))],
        compiler_params=pltpu.CompilerParams(vmem_limit_bytes=VMEM_LIMIT_BYTES,
                                             dimension_semantics=("arbitrary", "arbitrary")),
        name="tail",
    )(attn, ur, ui, x, mod, mod, mod, mod, g_post_mix, g_pre_ffn, g_post_ffn,
      w_out, w_four_r, w_four_i, wg, wu, wd)


def _rope_tables(seq):
    t = np.arange(seq)
    freqs = ROPE_THETA ** (-np.arange(ROPE_PAIRS_PER_AXIS, dtype=np.float64) / ROPE_PAIRS_PER_AXIS)
    ang = np.concatenate([(t // GRID_W)[:, None] * freqs, (t % GRID_W)[:, None] * freqs], axis=-1)
    cos = np.repeat(np.cos(ang), 2, axis=-1)
    sin = np.repeat(np.sin(ang), 2, axis=-1) * np.tile([-1.0, 1.0], HEAD_DIM // 2)
    heads_per_row = LANES // HEAD_DIM
    return (jnp.asarray(np.tile(cos, (1, heads_per_row)), F32), jnp.asarray(np.tile(sin, (1, heads_per_row)), F32))


def kernel(x, c, ctx, c_ctx, w_ada, b_ada, norm_pre_mix, norm_post_mix, norm_pre_ffn, norm_post_ffn,
           w_in, w_out, w_fourier, sink, w_gate, w_up, w_down):
    batch, seq, d = x.shape
    depth = w_ada.shape[0]
    assert batch + 1 <= MOD_ROWS

    cc = jnp.zeros((MOD_ROWS, d), F32).at[:batch].set(c).at[batch].set(c_ctx)
    mod = _ada(cc, w_ada, b_ada)
    w_four_r, w_four_i = _fold_fourier_weights(w_fourier, w_out)
    rope = _rope_tables(seq)

    q_scale = np.ones((w_in.shape[-1],), np.float32)
    q_scale[:ATTN_WIDTH] = HEAD_DIM ** -0.5 * LOG2_E
    w_in_b = (w_in * q_scale).astype(BF16)
    gains = [g.reshape(depth, 1, d) for g in (norm_pre_mix, norm_post_mix, norm_pre_ffn, norm_post_ffn)]
    g_pre_mix, g_post_mix, g_pre_ffn, g_post_ffn = gains

    ctx_len = ctx.shape[1]
    assert ctx_len == DFT_A
    xc = ctx.reshape(1, batch * ctx_len, d)
    for i in range(depth):
        q, k, v, u = _project(i, x, mod, g_pre_mix, w_in_b, rope)
        qc, kc, vc, uc = _project(i, xc, mod, g_pre_mix, w_in_b, None, ctx_row=batch)
        casts = [(w_out, i, ATTN_WIDTH), (w_gate, i, d), (w_up, i, d), (w_down, i, w_down.shape[1])]
        attn, tail_weights = _attention(q, k, v, kc, vc, sink[i], ctx_len, local=True, casts=casts)
        w_attn_out, wg, wu, wd = tail_weights

        def finish(stream, attn, ur, ui, ctx_row):
            return _tail(i, attn, ur, ui, stream, mod, g_post_mix, g_pre_ffn, g_post_ffn,
                         w_attn_out, w_four_r, w_four_i, wg, wu, wd, ctx_row)

        ur, ui = _position_dft(u)
        x = finish(x, attn, ur, ui, None)

        if i < depth - 1:
            attn_c, _ = _attention(qc, None, None, kc, vc, sink[i], ctx_len, local=False)
            urc, uic = _position_dft(uc)
            xc = finish(xc, attn_c, urc, uic, batch)
    return x
```

```python
import functools
import math

import numpy as np
import jax
import jax.numpy as jnp
from jax import lax
from jax.experimental import pallas as pl
from jax.experimental.pallas import tpu as pltpu

F32 = jnp.float32
BF16 = jnp.bfloat16

GRID_W = 64
HEAD_DIM = 64
N_Q_HEADS = 8
N_KV_HEADS = 2
Q_PER_KV = N_Q_HEADS // N_KV_HEADS
ATTN_WIDTH = N_Q_HEADS * HEAD_DIM
KV_WIDTH = N_KV_HEADS * HEAD_DIM
N_FOURIER_GROUPS = 8
FOURIER_GROUP_DIM = 64
FOURIER_WIDTH = N_FOURIER_GROUPS * FOURIER_GROUP_DIM
WINDOW = 128
ROPE_THETA = 10000.0
ROPE_PAIRS_PER_AXIS = HEAD_DIM // 4
RMS_EPS = 1e-6
N_MOD = 6
NEG_INF = -1e30
LOG2_E = math.log2(math.e)

LANES = 128
BF16_SUBLANES = 16
VMEM_LIMIT_BYTES = 60 * 1024 * 1024

DFT_A = 256
DFT_C = 16
DFT_CK = 8

MOD_ROWS = 8
ADA_COLS = 1536
PROJ_TILE = 1024
PROJ_SLICE_ROWS = 256
FFN_CHUNK = 512
TAIL_TILE = 1024
TAIL_SLICE_ROWS = 512
ATTN_Q_TILE = 2048
ATTN_LOOKAHEAD = 2
VT_HEAD_ROWS = HEAD_DIM + BF16_SUBLANES
VT_ROWS = N_KV_HEADS * VT_HEAD_ROWS


def _resident(shape):
    zeros = (0,) * len(shape)
    return pl.BlockSpec(shape, lambda *_: zeros, pipeline_mode=pl.Buffered(1))


def _layer_resident(layer, shape, index=None):
    index = (0,) * len(shape) if index is None else index
    return pl.BlockSpec((None,) + tuple(shape), lambda *_: (layer,) + tuple(index), pipeline_mode=pl.Buffered(1))


def _mod_spec(layer, which, d):
    return pl.BlockSpec((None, MOD_ROWS, d), lambda *_: (layer, 0, which))


def _mod_row(ref, ctx_row):
    row = pl.program_id(0) if ctx_row is None else ctx_row
    return ref[pl.ds(row, 1), :]


def _rms(xf, g):
    ms = jnp.mean(xf * xf, axis=-1, keepdims=True)
    return xf * lax.rsqrt(ms + RMS_EPS) * g


def _split_bf16(x):
    hi = x.astype(BF16)
    return hi, (x - hi.astype(F32)).astype(BF16)


def _dot3(a, b):
    (a_hi, a_lo), (b_hi, b_lo) = a, b
    dot = functools.partial(jnp.dot, preferred_element_type=F32)
    return dot(a_hi, b_hi) + dot(a_lo, b_hi) + dot(a_hi, b_lo)


def _ada_kernel(c_ref, w_ref, b_ref, o_ref):
    c = c_ref[...]
    a = c * jax.nn.sigmoid(c)
    a_hi, a_lo = _split_bf16(a)
    w_hi, w_lo = _split_bf16(w_ref[0])
    rows = a.shape[0]
    r = jnp.dot(jnp.concatenate([a_hi, a_lo], axis=0), w_hi, preferred_element_type=F32)
    o_ref[0] = r[:rows] + r[rows:] + jnp.dot(a_hi, w_lo, preferred_element_type=F32) + b_ref[0]


def _ada(cc, w_ada, b_ada):
    depth, d, n = w_ada.shape
    tn = ADA_COLS
    assert n % tn == 0
    return pl.pallas_call(
        _ada_kernel,
        grid=(depth, n // tn),
        in_specs=[
            pl.BlockSpec((MOD_ROWS, d), lambda i, j: (0, 0)),
            pl.BlockSpec((1, d, tn), lambda i, j: (i, 0, j)),
            pl.BlockSpec((1, 1, tn), lambda i, j: (i, 0, j)),
        ],
        out_specs=pl.BlockSpec((1, MOD_ROWS, tn), lambda i, j: (i, 0, j)),
        out_shape=jax.ShapeDtypeStruct((depth, MOD_ROWS, n), F32),
        compiler_params=pltpu.CompilerParams(vmem_limit_bytes=VMEM_LIMIT_BYTES),
        name="ada",
    )(cc, w_ada, b_ada.reshape(depth, 1, n))


def _fold_kernel(c64_ref, s64_ref, wf_ref, wo_ref, wr_ref, wi_ref):
    gd = c64_ref.shape[0]
    for g in range(wf_ref.shape[0]):
        rows = slice(g * gd, (g + 1) * gd)
        t = _split_bf16(_dot3(_split_bf16(wf_ref[g]), _split_bf16(wo_ref[rows, :])))
        wr_ref[rows, :] = _dot3(_split_bf16(c64_ref[...]), t).astype(BF16)
        wi_ref[rows, :] = _dot3(_split_bf16(s64_ref[...]), t).astype(BF16)


def _fold_fourier_weights(w_fourier, w_out):
    depth, groups, gd, _ = w_fourier.shape
    d = w_out.shape[-1]
    width = groups * gd
    assert w_out.shape[1] == ATTN_WIDTH + width and ATTN_WIDTH == width
    j = np.arange(gd)
    ang = 2.0 * np.pi * ((j[:, None] * j[None, :]) % gd) / gd
    c64 = jnp.asarray(np.cos(ang), F32)
    s64 = jnp.asarray(np.sin(ang), F32)
    out = jax.ShapeDtypeStruct((depth, width, d), BF16)
    return pl.pallas_call(
        _fold_kernel,
        grid=(depth,),
        in_specs=[
            pl.BlockSpec((gd, gd), lambda i: (0, 0)),
            pl.BlockSpec((gd, gd), lambda i: (0, 0)),
            pl.BlockSpec((None, groups, gd, gd), lambda i: (i, 0, 0, 0)),
            pl.BlockSpec((None, width, d), lambda i: (i, 1, 0)),
        ],
        out_specs=[pl.BlockSpec((None, width, d), lambda i: (i, 0, 0))] * 2,
        out_shape=[out, out],
        name="fold_fourier",
    )(c64, s64, w_fourier, w_out)


def _proj_kernel(x_ref, sh_ref, sc_ref, g_ref, w_ref, *rest, rope, ctx_row):
    if rope:
        cos_ref, sin_ref, q_ref, k_ref, v_ref, u_ref, u_scratch = rest
    else:
        q_ref, k_ref, v_ref, u_ref = rest
    gain = (1.0 + _mod_row(sc_ref, ctx_row)) * g_ref[...]
    shift = _mod_row(sh_ref, ctx_row)
    k_lo = ATTN_WIDTH
    v_lo = ATTN_WIDTH + KV_WIDTH
    u_lo = ATTN_WIDTH + 2 * KV_WIDTH
    tm = x_ref.shape[1]
    per = min(PROJ_SLICE_ROWS, tm)
    assert tm % per == 0 and per % DFT_C == 0
    if rope:
        even_lane = (lax.broadcasted_iota(jnp.int32, (per, LANES), 1) & 1) == 0

    def norm(rows):
        return (_rms(x_ref[0, rows, :], gain) + shift).astype(BF16)

    def emit(rows, p):
        if rope:
            cos = cos_ref[rows, :]
            sin = sin_ref[rows, :]

            def rot(xs):
                partner = jnp.where(even_lane, pltpu.roll(xs, LANES - 1, 1), pltpu.roll(xs, 1, 1))
                return xs * cos + partner * sin

            for j in range(ATTN_WIDTH // LANES):
                q_ref[0, rows, j * LANES:(j + 1) * LANES] = rot(p[:, j * LANES:(j + 1) * LANES]).astype(BF16)
            k_ref[0, rows, :] = rot(p[:, k_lo:v_lo]).astype(BF16)
        else:
            q_ref[0, rows, :] = p[:, :k_lo].astype(BF16)
            k_ref[0, rows, :] = p[:, k_lo:v_lo].astype(BF16)
        vt = p[:, v_lo:u_lo].T.astype(BF16)
        ones = jnp.ones((VT_HEAD_ROWS - HEAD_DIM, per), BF16)
        for h_kv in range(N_KV_HEADS):
            v_ref[0, h_kv * VT_HEAD_ROWS:h_kv * VT_HEAD_ROWS + HEAD_DIM, rows] = (
                vt[h_kv * HEAD_DIM:(h_kv + 1) * HEAD_DIM])
            v_ref[0, h_kv * VT_HEAD_ROWS + HEAD_DIM:(h_kv + 1) * VT_HEAD_ROWS, rows] = ones
        if rope:
            n1 = slice(rows.start // DFT_C, rows.stop // DFT_C)
            for c in range(u_scratch.shape[0]):
                u_scratch[c, rows, :] = p[:, u_lo + c * LANES:u_lo + (c + 1) * LANES]
                for n2 in range(DFT_C):
                    u_ref[0, n2, n1, c * LANES:(c + 1) * LANES] = (
                        u_scratch[c, pl.ds(rows.start + n2, per // DFT_C, stride=DFT_C), :].astype(BF16))
        else:
            u_ref[0, rows, :] = p[:, u_lo:].astype(BF16)

    slices = [slice(r, r + per) for r in range(0, tm, per)]
    h = norm(slices[0])
    pending = None
    for i, rows in enumerate(slices):
        p = jnp.dot(h, w_ref[...], preferred_element_type=F32)
        if i + 1 < len(slices):
            h = norm(slices[i + 1])
        if pending is not None:
            emit(*pending)
        pending = (rows, p)
    emit(*pending)


def _project(layer, x, mod, gains, w_in, rope_tables, ctx_row=None):
    bx, t, d = x.shape
    tm = min(PROJ_TILE, t)
    n_in = w_in.shape[-1]
    rope = rope_tables is not None
    tok = lambda width: pl.BlockSpec((1, tm, width), lambda b, i: (b, i, 0))
    in_specs = [tok(d), _mod_spec(layer, 0, d), _mod_spec(layer, 1, d),
                _layer_resident(layer, (1, d)), _layer_resident(layer, (d, n_in))]
    args = [x, mod, mod, gains, w_in]
    if rope:
        in_specs += [pl.BlockSpec((tm, LANES), lambda b, i: (i, 0))] * 2
        args += list(rope_tables)
    widths = (ATTN_WIDTH, KV_WIDTH, KV_WIDTH, FOURIER_WIDTH)
    out_specs = [tok(w) for w in widths]
    out_shape = [jax.ShapeDtypeStruct((bx, t, w), BF16) for w in widths]
    out_specs[2] = pl.BlockSpec((1, VT_ROWS, tm), lambda b, i: (b, 0, i))
    out_shape[2] = jax.ShapeDtypeStruct((bx, VT_ROWS, t), BF16)
    scratch = []
    if rope:
        assert t == DFT_A * DFT_C and tm % DFT_C == 0
        out_specs[3] = pl.BlockSpec((1, DFT_C, tm // DFT_C, FOURIER_WIDTH), lambda b, i: (b, 0, i, 0))
        out_shape[3] = jax.ShapeDtypeStruct((bx, DFT_C, DFT_A, FOURIER_WIDTH), BF16)
        scratch = [pltpu.VMEM((FOURIER_WIDTH // LANES, tm, LANES), F32)]
    return pl.pallas_call(
        functools.partial(_proj_kernel, rope=rope, ctx_row=ctx_row),
        grid=(bx, t // tm),
        in_specs=in_specs,
        out_specs=out_specs,
        out_shape=out_shape,
        scratch_shapes=scratch,
        compiler_params=pltpu.CompilerParams(vmem_limit_bytes=VMEM_LIMIT_BYTES),
        name="project_rope" if rope else "project",
    )(*args)


def _attn_kernel(sink_ref, q_ref, *rest, local, n_cast):
    w = WINDOW
    n_in = len(rest) - 1 - 2 * n_cast
    cast_src = rest[n_in:n_in + n_cast]
    cast_dst = rest[n_in + n_cast + 1:]
    rest = rest[:n_in] + rest[n_in + n_cast:n_in + n_cast + 1]
    for src, dst in zip(cast_src, cast_dst):
        dst[...] = src[...].astype(BF16)
    if local:
        kp_ref, kc_ref, kn_ref, vp_ref, vc_ref, vn_ref, kx_ref, vx_ref, o_ref = rest
        k_all = jnp.concatenate([kp_ref[0], kc_ref[0], kn_ref[0]], axis=0)
        vt_all = jnp.concatenate([vp_ref[0], vc_ref[0], vn_ref[0]], axis=1)
        step = pl.program_id(1)
        last = pl.num_programs(1) - 1
    else:
        kx_ref, vx_ref, o_ref = rest
    kx = kx_ref[0]
    vxt = vx_ref[0]
    n_sub = q_ref.shape[1] // w
    cols = Q_PER_KV * w
    group = lax.shift_right_logical(lax.broadcasted_iota(jnp.int32, (1, cols), 1), int(math.log2(w)))
    if local:
        kj = lax.broadcasted_iota(jnp.int32, (w, cols), 0)
        qi = lax.broadcasted_iota(jnp.int32, (w, cols), 1) & (w - 1)
        far = 1 << 20
        bias_prev = jnp.where(kj >= qi, 0.0, NEG_INF)
        bias_next = jnp.where(kj <= qi, 0.0, NEG_INF)
        bias_prev_first = jnp.where(kj >= qi + jnp.where(step > 0, 0, far), 0.0, NEG_INF)
        bias_next_last = jnp.where(kj <= qi - jnp.where(step < last, 0, far), 0.0, NEG_INF)

    def sink_row(h):
        sink = jnp.full((1, cols), sink_ref[Q_PER_KV * h] * LOG2_E, F32)
        for g in range(1, Q_PER_KV):
            sink = jnp.where(group == g, sink_ref[Q_PER_KV * h + g] * LOG2_E, sink)
        return sink

    sinks = [sink_row(h) for h in range(N_KV_HEADS)]

    def scores(h, s):
        hs = slice(h * HEAD_DIM, (h + 1) * HEAD_DIM)
        qt = q_ref[0, s * w:(s + 1) * w, :].astype(F32).T.astype(BF16)
        qs = jnp.concatenate([qt[(Q_PER_KV * h + g) * HEAD_DIM:(Q_PER_KV * h + g + 1) * HEAD_DIM]
                              for g in range(Q_PER_KV)], axis=1)
        pieces = []
        if local:
            sl = jnp.dot(k_all[s * w:(s + 3) * w, hs], qs, preferred_element_type=F32)
            pieces += [sl[:w] + (bias_prev_first if s == 0 else bias_prev), sl[w:2 * w],
                       sl[2 * w:] + (bias_next_last if s == n_sub - 1 else bias_next)]
        pieces.append(jnp.dot(kx[:, hs], qs, preferred_element_type=F32))
        m = sinks[h]
        for x in pieces:
            m = jnp.maximum(m, jnp.max(x, axis=0, keepdims=True))
        return pieces, m

    def finish(h, s, pieces, m):
        vs = slice(h * VT_HEAD_ROWS, (h + 1) * VT_HEAD_ROWS)
        probs = [jnp.exp2(x - m).astype(BF16) for x in pieces]
        ot = jnp.dot(vxt[vs, :], probs[-1], preferred_element_type=F32)
        if local:
            ot = ot + jnp.dot(vt_all[vs, s * w:(s + 3) * w], jnp.concatenate(probs[:3], axis=0),
                              preferred_element_type=F32)
        denom = ot[HEAD_DIM:HEAD_DIM + 1] + jnp.exp2(sinks[h] - m)
        ot = ot[:HEAD_DIM] / denom
        o = jnp.concatenate([ot[:, g * w:(g + 1) * w].T for g in range(Q_PER_KV)], axis=1)
        lo = h * Q_PER_KV * HEAD_DIM
        o_ref[0, s * w:(s + 1) * w, lo:lo + Q_PER_KV * HEAD_DIM] = o.astype(BF16)

    work = [(h, s) for h in range(N_KV_HEADS) for s in range(n_sub)]
    queue = [scores(*item) for item in work[:ATTN_LOOKAHEAD]]
    for i, (h, s) in enumerate(work):
        if i + ATTN_LOOKAHEAD < len(work):
            queue.append(scores(*work[i + ATTN_LOOKAHEAD]))
        finish(h, s, *queue.pop(0))


def _attention(q, k, v, kx, vx, sink, cx, local, casts=()):
    smem = pl.BlockSpec(memory_space=pltpu.SMEM)
    kx_spec = pl.BlockSpec((1, cx, KV_WIDTH), lambda b, n: (0, b, 0))
    vx_spec = pl.BlockSpec((1, VT_ROWS, cx), lambda b, n: (0, 0, b))
    if local:
        bx, t, _ = q.shape
        tq = min(ATTN_Q_TILE, t)
        q_spec = pl.BlockSpec((1, tq, ATTN_WIDTH), lambda b, n: (b, n, 0))
    else:
        bx, t, tq = q.shape[1] // cx, cx, cx
        q_spec = pl.BlockSpec((1, tq, ATTN_WIDTH), lambda b, n: (0, b, 0))
    nb = t // tq
    if local:
        per = tq // WINDOW
        n_win = t // WINDOW
        before = lambda n: jnp.maximum(n * per - 1, 0)
        after = lambda n: jnp.minimum((n + 1) * per, n_win - 1)
        k_specs = [pl.BlockSpec((1, WINDOW, KV_WIDTH), lambda b, n: (b, before(n), 0)),
                   pl.BlockSpec((1, tq, KV_WIDTH), lambda b, n: (b, n, 0)),
                   pl.BlockSpec((1, WINDOW, KV_WIDTH), lambda b, n: (b, after(n), 0))]
        v_specs = [pl.BlockSpec((1, VT_ROWS, WINDOW), lambda b, n: (b, 0, before(n))),
                   pl.BlockSpec((1, VT_ROWS, tq), lambda b, n: (b, 0, n)),
                   pl.BlockSpec((1, VT_ROWS, WINDOW), lambda b, n: (b, 0, after(n)))]
        in_specs = [smem, q_spec] + k_specs + v_specs + [kx_spec, vx_spec]
        args = [sink, q, k, k, k, v, v, v, kx, vx]
    else:
        in_specs = [smem, q_spec, kx_spec, vx_spec]
        args = [sink, q, kx, vx]
    out_specs = [q_spec]
    out_shape = [jax.ShapeDtypeStruct(q.shape, BF16)]
    steps = bx * nb
    for w, layer, rows in casts:
        slab = rows // steps
        assert rows % steps == 0 and slab % BF16_SUBLANES == 0
        in_specs.append(pl.BlockSpec((None, slab, w.shape[-1]), lambda b, n, layer=layer: (layer, b * nb + n, 0)))
        out_specs.append(pl.BlockSpec((slab, w.shape[-1]), lambda b, n: (b * nb + n, 0)))
        out_shape.append(jax.ShapeDtypeStruct((rows, w.shape[-1]), BF16))
        args.append(w)
    outs = pl.pallas_call(
        functools.partial(_attn_kernel, local=local, n_cast=len(casts)),
        grid=(bx, nb),
        in_specs=in_specs,
        out_specs=out_specs,
        out_shape=out_shape,
        compiler_params=pltpu.CompilerParams(vmem_limit_bytes=VMEM_LIMIT_BYTES),
        name="attn_window" if local else "attn_context",
    )(*args)
    return outs[0], outs[1:]


def _dft_short_kernel(fa_ref, u_ref, ur_ref, ui_ref):
    y = jnp.dot(fa_ref[...].astype(BF16), u_ref[0], preferred_element_type=F32)
    ur_ref[0] = y[:DFT_A].astype(BF16)
    ui_ref[0] = y[DFT_A:].astype(BF16)


def _dft_long_kernel(fa_ref, g_ref, tr_ref, ti_ref, u_ref, ur_ref, ui_ref, yr_s, yi_s):
    kb = DFT_CK
    fa = fa_ref[...].astype(BF16)
    for a in range(DFT_C // 2):
        ys = [jnp.dot(fa, u_ref[0, 2 * a + e], preferred_element_type=F32) for e in range(2)]
        for c in range(ys[0].shape[1] // LANES):
            sl = slice(c * LANES, (c + 1) * LANES)
            re_im = []
            for e in range(2):
                tr, ti = tr_ref[2 * a + e], ti_ref[2 * a + e]
                yr, yi = ys[e][:DFT_A, sl], ys[e][DFT_A:, sl]
                re_im.append(((yr * tr - yi * ti).reshape(DFT_A // kb, kb, LANES),
                              (yr * ti + yi * tr).reshape(DFT_A // kb, kb, LANES)))
            yr_s[:, a, :, sl] = jnp.concatenate([re_im[0][0], re_im[1][0]], axis=1).astype(BF16)
            yi_s[:, a, :, sl] = jnp.concatenate([re_im[0][1], re_im[1][1]], axis=1).astype(BF16)
    g = g_ref[...].astype(BF16)
    half = DFT_C * kb
    w = ur_ref.shape[-1]
    for j in range(DFT_A // (2 * kb)):
        rs = []
        for e in range(2):
            y = jnp.concatenate([yr_s[2 * j + e].reshape(half, w), yi_s[2 * j + e].reshape(half, w)], axis=0)
            rs.append(jnp.dot(g, y, preferred_element_type=F32))
        sl = slice(j * 2 * kb, (j + 1) * 2 * kb)
        for ref, lo in ((ur_ref, 0), (ui_ref, half)):
            ref[0, :, sl, :] = jnp.concatenate(
                [r[lo:lo + half].reshape(DFT_C, kb, w) for r in rs], axis=1).astype(BF16)


def _dft_tables(n_pos):
    a = np.arange(DFT_A)
    ang = 2.0 * np.pi * ((a[:, None] * a[None, :]) % DFT_A) / DFT_A
    norm = 1.0 / math.sqrt(n_pos * FOURIER_GROUP_DIM)
    fa = jnp.asarray(np.concatenate([np.cos(ang), -np.sin(ang)], axis=0) * norm, F32)
    if n_pos == DFT_A:
        return fa, None, None, None
    assert n_pos == DFT_A * DFT_C
    n2 = np.arange(DFT_C)
    tw = 2.0 * np.pi * (n2[:, None] * a[None, :]) / n_pos
    tr = jnp.asarray(np.repeat(np.cos(tw)[:, :, None], LANES, axis=2), F32)
    ti = jnp.asarray(np.repeat(-np.sin(tw)[:, :, None], LANES, axis=2), F32)
    ang16 = 2.0 * np.pi * ((n2[:, None] * n2[None, :]) % DFT_C) / DFT_C
    c16, s16 = np.cos(ang16), np.sin(ang16)
    eye = np.eye(DFT_CK)
    kron = lambda f: np.einsum("kn,ab->kanb", f, eye).reshape(DFT_C * DFT_CK, DFT_C * DFT_CK)
    gmat = np.block([[kron(c16), kron(s16)], [kron(-s16), kron(c16)]])
    return fa, tr, ti, jnp.asarray(gmat, F32)


def _position_dft(u):
    bx, w = u.shape[0], u.shape[-1]
    n_pos = DFT_A if u.ndim == 3 else math.prod(u.shape[1:-1])
    fa, tr, ti, gmat = _dft_tables(n_pos)
    fa_spec = _resident((2 * DFT_A, DFT_A))
    if u.ndim == 3:
        blk = pl.BlockSpec((1, DFT_A, w), lambda b: (0, b, 0))
        out = jax.ShapeDtypeStruct(u.shape, BF16)
        return pl.pallas_call(
            _dft_short_kernel,
            grid=(u.shape[1] // DFT_A,), in_specs=[fa_spec, blk], out_specs=[blk, blk], out_shape=[out, out],
            name="dft_context",
        )(fa, u)
    blk = pl.BlockSpec((1, DFT_C, DFT_A, w), lambda b: (b, 0, 0, 0))
    out = jax.ShapeDtypeStruct((bx, DFT_C, DFT_A, w), BF16)
    ur, ui = pl.pallas_call(
        _dft_long_kernel,
        grid=(bx,),
        in_specs=[fa_spec, _resident(gmat.shape), _resident(tr.shape), _resident(ti.shape), blk],
        out_specs=[blk, blk], out_shape=[out, out],
        scratch_shapes=[pltpu.VMEM((DFT_A // DFT_CK, DFT_C // 2, 2 * DFT_CK, w), BF16)] * 2,
        compiler_params=pltpu.CompilerParams(vmem_limit_bytes=VMEM_LIMIT_BYTES),
        name="dft_long",
    )(fa, gmat, tr, ti, u)
    return ur.reshape(bx, n_pos, w), ui.reshape(bx, n_pos, w)


def _tail_kernel(a_ref, ur_ref, ui_ref, x_ref, gate1_ref, sh_ref, sc_ref, gate2_ref,
                 gpost_mix_ref, gpre_ref, gpost_ref, wa_ref, wr_ref, wi_ref, wg_ref, wu_ref, wd_ref, o_ref,
                 *, ctx_row):
    gate1, shift, scale, gate2 = (_mod_row(r, ctx_row) for r in (gate1_ref, sh_ref, sc_ref, gate2_ref))
    gain_mix = gate1 * gpost_mix_ref[...]
    gain_in = (1.0 + scale) * gpre_ref[...]
    gain_out = gate2 * gpost_ref[...]
    hidden = wg_ref.shape[1]
    tm = x_ref.shape[1]
    per = min(TAIL_SLICE_ROWS, tm)
    assert tm % per == 0

    def mix_dots(rows):
        mix = jnp.dot(a_ref[0, rows, :], wa_ref[...], preferred_element_type=F32)
        mix += jnp.dot(ur_ref[0, rows, :], wr_ref[...], preferred_element_type=F32)
        mix += jnp.dot(ui_ref[0, rows, :], wi_ref[...], preferred_element_type=F32)
        return mix

    def norms(rows, mix):
        x_mid = x_ref[0, rows, :] + _rms(mix, gain_mix)
        return x_mid, (_rms(x_mid, gain_in) + shift).astype(BF16)

    def ffn(h):
        acc = jnp.zeros((h.shape[0], o_ref.shape[2]), F32)
        for lo in range(0, hidden, FFN_CHUNK):
            hi = min(lo + FFN_CHUNK, hidden)
            g = jnp.dot(h, wg_ref[:, lo:hi], preferred_element_type=F32)
            u = jnp.dot(h, wu_ref[:, lo:hi], preferred_element_type=F32)
            a = (g * jax.nn.sigmoid(g) * u).astype(BF16)
            acc += jnp.dot(a, wd_ref[lo:hi, :], preferred_element_type=F32)
        return acc

    def epilogue(rows, x_mid, acc):
        o_ref[0, rows, :] = x_mid + _rms(acc, gain_out)

    slices = [slice(r, r + per) for r in range(0, tm, per)]
    mixes = [mix_dots(rows) for rows in slices]
    staged = norms(slices[0], mixes[0])
    pending = None
    for i, rows in enumerate(slices):
        x_mid, h = staged
        acc = ffn(h)
        if i + 1 < len(slices):
            staged = norms(slices[i + 1], mixes[i + 1])
        if pending is not None:
            epilogue(*pending)
        pending = (rows, x_mid, acc)
    epilogue(*pending)


def _tail(layer, attn, ur, ui, x, mod, g_post_mix, g_pre_ffn, g_post_ffn, w_out, w_four_r, w_four_i,
          wg, wu, wd, ctx_row=None):
    bx, t, d = x.shape
    tm = min(TAIL_TILE, max(t // 2, TAIL_SLICE_ROWS) if bx == 1 else t)
    hidden = wg.shape[-1]
    tok = lambda width: pl.BlockSpec((1, tm, width), lambda b, i: (b, i, 0))
    gain = _layer_resident(layer, (1, d))
    wfour = _layer_resident(layer, (FOURIER_WIDTH, d))
    return pl.pallas_call(
        functools.partial(_tail_kernel, ctx_row=ctx_row),
        grid=(bx, t // tm),
        in_specs=[tok(ATTN_WIDTH), tok(FOURIER_WIDTH), tok(FOURIER_WIDTH), tok(d),
                  _mod_spec(layer, 2, d), _mod_spec(layer, 3, d), _mod_spec(layer, 4, d), _mod_spec(layer, 5, d),
                  gain, gain, gain,
                  _resident((ATTN_WIDTH, d)), wfour, wfour,
                  _resident((d, hidden)), _resident((d, hidden)), _resident((hidden, d))],
        out_specs=tok(d),
        out_shape=jax.ShapeDtypeStruct((bx, t, d), F32),
        compiler_params=pltpu.CompilerParams(vmem_limit_bytes=VMEM_LIMIT_BYTES),
        name="tail",
    )(attn, ur, ui, x, mod, mod, mod, mod, g_post_mix, g_pre_ffn, g_post_ffn,
      w_out, w_four_r, w_four_i, wg, wu, wd)


def _rope_tables(seq):
    t = np.arange(seq)
    freqs = ROPE_THETA ** (-np.arange(ROPE_PAIRS_PER_AXIS, dtype=np.float64) / ROPE_PAIRS_PER_AXIS)
    ang = np.concatenate([(t // GRID_W)[:, None] * freqs, (t % GRID_W)[:, None] * freqs], axis=-1)
    cos = np.repeat(np.cos(ang), 2, axis=-1)
    sin = np.repeat(np.sin(ang), 2, axis=-1) * np.tile([-1.0, 1.0], HEAD_DIM // 2)
    heads_per_row = LANES // HEAD_DIM
    return (jnp.asarray(np.tile(cos, (1, heads_per_row)), F32), jnp.asarray(np.tile(sin, (1, heads_per_row)), F32))


def kernel(x, c, ctx, c_ctx, w_ada, b_ada, norm_pre_mix, norm_post_mix, norm_pre_ffn, norm_post_ffn,
           w_in, w_out, w_fourier, sink, w_gate, w_up, w_down):
    batch, seq, d = x.shape
    depth = w_ada.shape[0]
    assert batch + 1 <= MOD_ROWS

    cc = jnp.zeros((MOD_ROWS, d), F32).at[:batch].set(c).at[batch].set(c_ctx)
    mod = _ada(cc, w_ada, b_ada)
    w_four_r, w_four_i = _fold_fourier_weights(w_fourier, w_out)
    rope = _rope_tables(seq)

    q_scale = np.ones((w_in.shape[-1],), np.float32)
    q_scale[:ATTN_WIDTH] = HEAD_DIM ** -0.5 * LOG2_E
    w_in_b = (w_in * q_scale).astype(BF16)
    gains = [g.reshape(depth, 1, d) for g in (norm_pre_mix, norm_post_mix, norm_pre_ffn, norm_post_ffn)]
    g_pre_mix, g_post_mix, g_pre_ffn, g_post_ffn = gains

    ctx_len = ctx.shape[1]
    assert ctx_len == DFT_A
    xc = ctx.reshape(1, batch * ctx_len, d)
    for i in range(depth):
        q, k, v, u = _project(i, x, mod, g_pre_mix, w_in_b, rope)
        qc, kc, vc, uc = _project(i, xc, mod, g_pre_mix, w_in_b, None, ctx_row=batch)
        casts = [(w_out, i, ATTN_WIDTH), (w_gate, i, d), (w_up, i, d), (w_down, i, w_down.shape[1])]
        attn, tail_weights = _attention(q, k, v, kc, vc, sink[i], ctx_len, local=True, casts=casts)
        w_attn_out, wg, wu, wd = tail_weights

        def finish(stream, attn, ur, ui, ctx_row):
            return _tail(i, attn, ur, ui, stream, mod, g_post_mix, g_pre_ffn, g_post_ffn,
                         w_attn_out, w_four_r, w_four_i, wg, wu, wd, ctx_row)

        ur, ui = _position_dft(u)
        x = finish(x, attn, ur, ui, None)

        if i < depth - 1:
            attn_c, _ = _attention(qc, None, None, kc, vc, sink[i], ctx_len, local=False)
            urc, uic = _position_dft(uc)
            xc = finish(xc, attn_c, urc, uic, batch)
    return x
```

```python
import functools
import math

import numpy as np
import jax
import jax.numpy as jnp
from jax import lax
from jax.experimental import pallas as pl
from jax.experimental.pallas import tpu as pltpu

F32 = jnp.float32
BF16 = jnp.bfloat16

GRID_W = 64
HEAD_DIM = 64
N_Q_HEADS = 8
N_KV_HEADS = 2
Q_PER_KV = N_Q_HEADS // N_KV_HEADS
ATTN_WIDTH = N_Q_HEADS * HEAD_DIM
KV_WIDTH = N_KV_HEADS * HEAD_DIM
N_FOURIER_GROUPS = 8
FOURIER_GROUP_DIM = 64
FOURIER_WIDTH = N_FOURIER_GROUPS * FOURIER_GROUP_DIM
WINDOW = 128
ROPE_THETA = 10000.0
ROPE_PAIRS_PER_AXIS = HEAD_DIM // 4
RMS_EPS = 1e-6
NEG_INF = -1e30
LOG2_E = math.log2(math.e)

LANES = 128
BF16_SUBLANES = 16
VMEM_LIMIT_BYTES = 60 * 1024 * 1024

DFT_A = 256
DFT_C = 16
DFT_CK = 8

MOD_ROWS = 8
ADA_COLS = 1536
PROJ_TILE = 2048
PROJ_SLICE_ROWS = 256
FFN_CHUNK = 512
TAIL_TILE = 1024
TAIL_SLICE_ROWS = 512
ATTN_Q_TILE = 2048
ATTN_LOOKAHEAD = 2
VT_HEAD_ROWS = HEAD_DIM + BF16_SUBLANES
VT_ROWS = N_KV_HEADS * VT_HEAD_ROWS


def _resident(shape):
    zeros = (0,) * len(shape)
    return pl.BlockSpec(shape, lambda *_: zeros, pipeline_mode=pl.Buffered(1))


def _layer_resident(layer, shape):
    index = (layer,) + (0,) * len(shape)
    return pl.BlockSpec((None,) + tuple(shape), lambda *_: index, pipeline_mode=pl.Buffered(1))


def _mod_spec(layer, which, d):
    return pl.BlockSpec((None, MOD_ROWS, d), lambda *_: (layer, 0, which))


def _mod_row(ref, ctx_row):
    row = pl.program_id(0) if ctx_row is None else ctx_row
    return ref[pl.ds(row, 1), :]


def _rms(xf, g):
    ms = jnp.mean(xf * xf, axis=-1, keepdims=True)
    return xf * lax.rsqrt(ms + RMS_EPS) * g


def _split_bf16(x):
    hi = x.astype(BF16)
    return hi, (x - hi.astype(F32)).astype(BF16)


def _dot3(a, b):
    (a_hi, a_lo), (b_hi, b_lo) = a, b
    dot = functools.partial(jnp.dot, preferred_element_type=F32)
    return dot(a_hi, b_hi) + dot(a_lo, b_hi) + dot(a_hi, b_lo)


def _ada_kernel(c_ref, w_ref, b_ref, o_ref):
    c = c_ref[...]
    a = c * jax.nn.sigmoid(c)
    a_hi, a_lo = _split_bf16(a)
    w_hi, w_lo = _split_bf16(w_ref[0])
    rows = a.shape[0]
    r = jnp.dot(jnp.concatenate([a_hi, a_lo], axis=0), w_hi, preferred_element_type=F32)
    o_ref[0] = r[:rows] + r[rows:] + jnp.dot(a_hi, w_lo, preferred_element_type=F32) + b_ref[0]


def _ada(cc, w_ada, b_ada):
    depth, d, n = w_ada.shape
    tn = ADA_COLS
    assert n % tn == 0
    return pl.pallas_call(
        _ada_kernel,
        grid=(depth, n // tn),
        in_specs=[
            pl.BlockSpec((MOD_ROWS, d), lambda i, j: (0, 0)),
            pl.BlockSpec((1, d, tn), lambda i, j: (i, 0, j)),
            pl.BlockSpec((1, 1, tn), lambda i, j: (i, 0, j)),
        ],
        out_specs=pl.BlockSpec((1, MOD_ROWS, tn), lambda i, j: (i, 0, j)),
        out_shape=jax.ShapeDtypeStruct((depth, MOD_ROWS, n), F32),
        compiler_params=pltpu.CompilerParams(vmem_limit_bytes=VMEM_LIMIT_BYTES),
        name="ada",
    )(cc, w_ada, b_ada.reshape(depth, 1, n))


def _fold_kernel(c64_ref, s64_ref, wf_ref, wo_ref, wr_ref, wi_ref):
    gd = c64_ref.shape[0]
    for g in range(wf_ref.shape[0]):
        rows = slice(g * gd, (g + 1) * gd)
        t = _split_bf16(_dot3(_split_bf16(wf_ref[g]), _split_bf16(wo_ref[rows, :])))
        wr_ref[rows, :] = _dot3(_split_bf16(c64_ref[...]), t).astype(BF16)
        wi_ref[rows, :] = _dot3(_split_bf16(s64_ref[...]), t).astype(BF16)


def _fold_fourier_weights(w_fourier, w_out):
    depth, groups, gd, _ = w_fourier.shape
    d = w_out.shape[-1]
    width = groups * gd
    assert w_out.shape[1] == ATTN_WIDTH + width and ATTN_WIDTH == width
    j = np.arange(gd)
    ang = 2.0 * np.pi * ((j[:, None] * j[None, :]) % gd) / gd
    c64 = jnp.asarray(np.cos(ang), F32)
    s64 = jnp.asarray(np.sin(ang), F32)
    out = jax.ShapeDtypeStruct((depth, width, d), BF16)
    return pl.pallas_call(
        _fold_kernel,
        grid=(depth,),
        in_specs=[
            pl.BlockSpec((gd, gd), lambda i: (0, 0)),
            pl.BlockSpec((gd, gd), lambda i: (0, 0)),
            pl.BlockSpec((None, groups, gd, gd), lambda i: (i, 0, 0, 0)),
            pl.BlockSpec((None, width, d), lambda i: (i, 1, 0)),
        ],
        out_specs=[pl.BlockSpec((None, width, d), lambda i: (i, 0, 0))] * 2,
        out_shape=[out, out],
        name="fold_fourier",
    )(c64, s64, w_fourier, w_out)


def _proj_kernel(x_ref, sh_ref, sc_ref, g_ref, w_ref, *rest, rope, ctx_row):
    if rope:
        cos_ref, sin_ref, q_ref, k_ref, v_ref, u_ref, u_scratch = rest
    else:
        q_ref, k_ref, v_ref, u_ref = rest
    gain = (1.0 + _mod_row(sc_ref, ctx_row)) * g_ref[...]
    shift = _mod_row(sh_ref, ctx_row)
    k_lo = ATTN_WIDTH
    v_lo = ATTN_WIDTH + KV_WIDTH
    u_lo = ATTN_WIDTH + 2 * KV_WIDTH
    tm = x_ref.shape[1]
    per = min(PROJ_SLICE_ROWS, tm)
    assert tm % per == 0 and per % DFT_C == 0
    if rope:
        even_lane = (lax.broadcasted_iota(jnp.int32, (per, LANES), 1) & 1) == 0

    def norm(rows):
        return (_rms(x_ref[0, rows, :], gain) + shift).astype(BF16)

    def emit(rows, p):
        if rope:
            cos = cos_ref[rows, :]
            sin = sin_ref[rows, :]

            def rot(xs):
                partner = jnp.where(even_lane, pltpu.roll(xs, LANES - 1, 1), pltpu.roll(xs, 1, 1))
                return xs * cos + partner * sin

            for j in range(ATTN_WIDTH // LANES):
                q_ref[0, rows, j * LANES:(j + 1) * LANES] = rot(p[:, j * LANES:(j + 1) * LANES]).astype(BF16)
            k_ref[0, rows, :] = rot(p[:, k_lo:v_lo]).astype(BF16)
        else:
            q_ref[0, rows, :] = p[:, :k_lo].astype(BF16)
            k_ref[0, rows, :] = p[:, k_lo:v_lo].astype(BF16)
        vt = p[:, v_lo:u_lo].T.astype(BF16)
        ones = jnp.ones((VT_HEAD_ROWS - HEAD_DIM, per), BF16)
        for h_kv in range(N_KV_HEADS):
            v_ref[0, h_kv * VT_HEAD_ROWS:h_kv * VT_HEAD_ROWS + HEAD_DIM, rows] = (
                vt[h_kv * HEAD_DIM:(h_kv + 1) * HEAD_DIM])
            v_ref[0, h_kv * VT_HEAD_ROWS + HEAD_DIM:(h_kv + 1) * VT_HEAD_ROWS, rows] = ones
        if rope:
            n1 = slice(rows.start // DFT_C, rows.stop // DFT_C)
            for c in range(u_scratch.shape[0]):
                u_scratch[c, rows, :] = p[:, u_lo + c * LANES:u_lo + (c + 1) * LANES]
                for n2 in range(DFT_C):
                    u_ref[0, n2, n1, c * LANES:(c + 1) * LANES] = (
                        u_scratch[c, pl.ds(rows.start + n2, per // DFT_C, stride=DFT_C), :].astype(BF16))
        else:
            u_ref[0, rows, :] = p[:, u_lo:].astype(BF16)

    slices = [slice(r, r + per) for r in range(0, tm, per)]
    h = norm(slices[0])
    pending = None
    for i, rows in enumerate(slices):
        p = jnp.dot(h, w_ref[...], preferred_element_type=F32)
        if i + 1 < len(slices):
            h = norm(slices[i + 1])
        if pending is not None:
            emit(*pending)
        pending = (rows, p)
    emit(*pending)


def _project(layer, x, mod, gains, w_in, rope_tables, ctx_row=None):
    bx, t, d = x.shape
    tm = min(PROJ_TILE, t)
    n_in = w_in.shape[-1]
    rope = rope_tables is not None
    tok = lambda width: pl.BlockSpec((1, tm, width), lambda b, i: (b, i, 0))
    in_specs = [tok(d), _mod_spec(layer, 0, d), _mod_spec(layer, 1, d),
                _layer_resident(layer, (1, d)), _layer_resident(layer, (d, n_in))]
    args = [x, mod, mod, gains, w_in]
    if rope:
        in_specs += [pl.BlockSpec((tm, LANES), lambda b, i: (i, 0))] * 2
        args += list(rope_tables)
    widths = (ATTN_WIDTH, KV_WIDTH, KV_WIDTH, FOURIER_WIDTH)
    out_specs = [tok(w) for w in widths]
    out_shape = [jax.ShapeDtypeStruct((bx, t, w), BF16) for w in widths]
    out_specs[2] = pl.BlockSpec((1, VT_ROWS, tm), lambda b, i: (b, 0, i))
    out_shape[2] = jax.ShapeDtypeStruct((bx, VT_ROWS, t), BF16)
    scratch = []
    if rope:
        assert t == DFT_A * DFT_C and tm % DFT_C == 0
        out_specs[3] = pl.BlockSpec((1, DFT_C, tm // DFT_C, FOURIER_WIDTH), lambda b, i: (b, 0, i, 0))
        out_shape[3] = jax.ShapeDtypeStruct((bx, DFT_C, DFT_A, FOURIER_WIDTH), BF16)
        scratch = [pltpu.VMEM((FOURIER_WIDTH // LANES, tm, LANES), F32)]
    return pl.pallas_call(
        functools.partial(_proj_kernel, rope=rope, ctx_row=ctx_row),
        grid=(bx, t // tm),
        in_specs=in_specs,
        out_specs=out_specs,
        out_shape=out_shape,
        scratch_shapes=scratch,
        compiler_params=pltpu.CompilerParams(vmem_limit_bytes=VMEM_LIMIT_BYTES),
        name="project_rope" if rope else "project",
    )(*args)


def _attn_kernel(sink_ref, q_ref, *rest, local, n_cast):
    w = WINDOW
    n_in = len(rest) - 1 - 2 * n_cast
    cast_src = rest[n_in:n_in + n_cast]
    cast_dst = rest[n_in + n_cast + 1:]
    rest = rest[:n_in] + rest[n_in + n_cast:n_in + n_cast + 1]
    for src, dst in zip(cast_src, cast_dst):
        dst[...] = src[...].astype(BF16)
    if local:
        kp_ref, kc_ref, kn_ref, vp_ref, vc_ref, vn_ref, kx_ref, vx_ref, o_ref = rest
        k_all = jnp.concatenate([kp_ref[0], kc_ref[0], kn_ref[0]], axis=0)
        vt_all = jnp.concatenate([vp_ref[0], vc_ref[0], vn_ref[0]], axis=1)
        step = pl.program_id(1)
        last = pl.num_programs(1) - 1
    else:
        kx_ref, vx_ref, o_ref = rest
    kx = kx_ref[0]
    vxt = vx_ref[0]
    n_sub = q_ref.shape[1] // w
    cols = Q_PER_KV * w
    group = lax.shift_right_logical(lax.broadcasted_iota(jnp.int32, (1, cols), 1), int(math.log2(w)))
    if local:
        kj = lax.broadcasted_iota(jnp.int32, (w, cols), 0)
        qi = lax.broadcasted_iota(jnp.int32, (w, cols), 1) & (w - 1)
        far = 1 << 20
        bias_prev = jnp.where(kj >= qi, 0.0, NEG_INF)
        bias_next = jnp.where(kj <= qi, 0.0, NEG_INF)
        bias_prev_first = jnp.where(kj >= qi + jnp.where(step > 0, 0, far), 0.0, NEG_INF)
        bias_next_last = jnp.where(kj <= qi - jnp.where(step < last, 0, far), 0.0, NEG_INF)
    contract_last = (((1,), (1,)), ((), ()))

    def sink_row(h):
        sink = jnp.full((1, cols), sink_ref[Q_PER_KV * h] * LOG2_E, F32)
        for g in range(1, Q_PER_KV):
            sink = jnp.where(group == g, sink_ref[Q_PER_KV * h + g] * LOG2_E, sink)
        return sink

    sinks = [sink_row(h) for h in range(N_KV_HEADS)]

    def scores(h, s):
        hs = slice(h * HEAD_DIM, (h + 1) * HEAD_DIM)
        q = q_ref[0, s * w:(s + 1) * w, :]
        qs = jnp.concatenate([q[:, (Q_PER_KV * h + g) * HEAD_DIM:(Q_PER_KV * h + g + 1) * HEAD_DIM]
                              for g in range(Q_PER_KV)], axis=0)
        pieces = []
        if local:
            sl = lax.dot_general(k_all[s * w:(s + 3) * w, hs], qs, contract_last, preferred_element_type=F32)
            pieces += [sl[:w] + (bias_prev_first if s == 0 else bias_prev), sl[w:2 * w],
                       sl[2 * w:] + (bias_next_last if s == n_sub - 1 else bias_next)]
        pieces.append(lax.dot_general(kx[:, hs], qs, contract_last, preferred_element_type=F32))
        m = sinks[h]
        for x in pieces:
            m = jnp.maximum(m, jnp.max(x, axis=0, keepdims=True))
        return pieces, m

    def finish(h, s, pieces, m):
        vs = slice(h * VT_HEAD_ROWS, (h + 1) * VT_HEAD_ROWS)
        probs = [jnp.exp2(x - m).astype(BF16) for x in pieces]
        ot = jnp.dot(vxt[vs, :], probs[-1], preferred_element_type=F32)
        if local:
            ot = ot + jnp.dot(vt_all[vs, s * w:(s + 3) * w], jnp.concatenate(probs[:3], axis=0),
                              preferred_element_type=F32)
        denom = ot[HEAD_DIM:HEAD_DIM + 1] + jnp.exp2(sinks[h] - m)
        ot = ot[:HEAD_DIM] / denom
        o = jnp.concatenate([ot[:, g * w:(g + 1) * w].T for g in range(Q_PER_KV)], axis=1)
        lo = h * Q_PER_KV * HEAD_DIM
        o_ref[0, s * w:(s + 1) * w, lo:lo + Q_PER_KV * HEAD_DIM] = o.astype(BF16)

    work = [(h, s) for h in range(N_KV_HEADS) for s in range(n_sub)]
    queue = [scores(*item) for item in work[:ATTN_LOOKAHEAD]]
    for i, (h, s) in enumerate(work):
        if i + ATTN_LOOKAHEAD < len(work):
            queue.append(scores(*work[i + ATTN_LOOKAHEAD]))
        finish(h, s, *queue.pop(0))


def _attention(q, k, v, kx, vx, sink, cx, local, casts=()):
    smem = pl.BlockSpec(memory_space=pltpu.SMEM)
    kx_spec = pl.BlockSpec((1, cx, KV_WIDTH), lambda b, n: (0, b, 0))
    vx_spec = pl.BlockSpec((1, VT_ROWS, cx), lambda b, n: (0, 0, b))
    if local:
        bx, t, _ = q.shape
        tq = min(ATTN_Q_TILE, t)
        q_spec = pl.BlockSpec((1, tq, ATTN_WIDTH), lambda b, n: (b, n, 0))
    else:
        bx, t, tq = q.shape[1] // cx, cx, cx
        q_spec = pl.BlockSpec((1, tq, ATTN_WIDTH), lambda b, n: (0, b, 0))
    nb = t // tq
    if local:
        per = tq // WINDOW
        n_win = t // WINDOW
        before = lambda n: jnp.maximum(n * per - 1, 0)
        after = lambda n: jnp.minimum((n + 1) * per, n_win - 1)
        k_specs = [pl.BlockSpec((1, WINDOW, KV_WIDTH), lambda b, n: (b, before(n), 0)),
                   pl.BlockSpec((1, tq, KV_WIDTH), lambda b, n: (b, n, 0)),
                   pl.BlockSpec((1, WINDOW, KV_WIDTH), lambda b, n: (b, after(n), 0))]
        v_specs = [pl.BlockSpec((1, VT_ROWS, WINDOW), lambda b, n: (b, 0, before(n))),
                   pl.BlockSpec((1, VT_ROWS, tq), lambda b, n: (b, 0, n)),
                   pl.BlockSpec((1, VT_ROWS, WINDOW), lambda b, n: (b, 0, after(n)))]
        in_specs = [smem, q_spec] + k_specs + v_specs + [kx_spec, vx_spec]
        args = [sink, q, k, k, k, v, v, v, kx, vx]
    else:
        in_specs = [smem, q_spec, kx_spec, vx_spec]
        args = [sink, q, kx, vx]
    out_specs = [q_spec]
    out_shape = [jax.ShapeDtypeStruct(q.shape, BF16)]
    steps = bx * nb
    for w, layer, rows in casts:
        slab = rows // steps
        assert rows % steps == 0 and slab % BF16_SUBLANES == 0
        in_specs.append(pl.BlockSpec((None, slab, w.shape[-1]), lambda b, n, layer=layer: (layer, b * nb + n, 0)))
        out_specs.append(pl.BlockSpec((slab, w.shape[-1]), lambda b, n: (b * nb + n, 0)))
        out_shape.append(jax.ShapeDtypeStruct((rows, w.shape[-1]), BF16))
        args.append(w)
    outs = pl.pallas_call(
        functools.partial(_attn_kernel, local=local, n_cast=len(casts)),
        grid=(bx, nb),
        in_specs=in_specs,
        out_specs=out_specs,
        out_shape=out_shape,
        compiler_params=pltpu.CompilerParams(vmem_limit_bytes=VMEM_LIMIT_BYTES),
        name="attn_window" if local else "attn_context",
    )(*args)
    return outs[0], outs[1:]


def _dft_short_kernel(fa_ref, u_ref, ur_ref, ui_ref):
    y = jnp.dot(fa_ref[...].astype(BF16), u_ref[0], preferred_element_type=F32)
    ur_ref[0] = y[:DFT_A].astype(BF16)
    ui_ref[0] = y[DFT_A:].astype(BF16)


def _dft_long_kernel(fa_ref, g_ref, tr_ref, ti_ref, u_ref, ur_ref, ui_ref, yr_s, yi_s):
    kb = DFT_CK
    fa = fa_ref[...].astype(BF16)
    for a in range(DFT_C // 2):
        ys = [jnp.dot(fa, u_ref[0, 2 * a + e], preferred_element_type=F32) for e in range(2)]
        for c in range(ys[0].shape[1] // LANES):
            sl = slice(c * LANES, (c + 1) * LANES)
            re_im = []
            for e in range(2):
                tr, ti = tr_ref[2 * a + e], ti_ref[2 * a + e]
                yr, yi = ys[e][:DFT_A, sl], ys[e][DFT_A:, sl]
                re_im.append(((yr * tr - yi * ti).reshape(DFT_A // kb, kb, LANES),
                              (yr * ti + yi * tr).reshape(DFT_A // kb, kb, LANES)))
            yr_s[:, a, :, sl] = jnp.concatenate([re_im[0][0], re_im[1][0]], axis=1).astype(BF16)
            yi_s[:, a, :, sl] = jnp.concatenate([re_im[0][1], re_im[1][1]], axis=1).astype(BF16)
    g = g_ref[...].astype(BF16)
    half = DFT_C * kb
    w = ur_ref.shape[-1]
    for j in range(DFT_A // (2 * kb)):
        rs = []
        for e in range(2):
            y = jnp.concatenate([yr_s[2 * j + e].reshape(half, w), yi_s[2 * j + e].reshape(half, w)], axis=0)
            rs.append(jnp.dot(g, y, preferred_element_type=F32))
        sl = slice(j * 2 * kb, (j + 1) * 2 * kb)
        for ref, lo in ((ur_ref, 0), (ui_ref, half)):
            ref[0, :, sl, :] = jnp.concatenate(
                [r[lo:lo + half].reshape(DFT_C, kb, w) for r in rs], axis=1).astype(BF16)


def _dft_tables(n_pos):
    a = np.arange(DFT_A)
    ang = 2.0 * np.pi * ((a[:, None] * a[None, :]) % DFT_A) / DFT_A
    norm = 1.0 / math.sqrt(n_pos * FOURIER_GROUP_DIM)
    fa = jnp.asarray(np.concatenate([np.cos(ang), -np.sin(ang)], axis=0) * norm, F32)
    if n_pos == DFT_A:
        return fa, None, None, None
    assert n_pos == DFT_A * DFT_C
    n2 = np.arange(DFT_C)
    tw = 2.0 * np.pi * (n2[:, None] * a[None, :]) / n_pos
    tr = jnp.asarray(np.repeat(np.cos(tw)[:, :, None], LANES, axis=2), F32)
    ti = jnp.asarray(np.repeat(-np.sin(tw)[:, :, None], LANES, axis=2), F32)
    ang16 = 2.0 * np.pi * ((n2[:, None] * n2[None, :]) % DFT_C) / DFT_C
    c16, s16 = np.cos(ang16), np.sin(ang16)
    eye = np.eye(DFT_CK)
    kron = lambda f: np.einsum("kn,ab->kanb", f, eye).reshape(DFT_C * DFT_CK, DFT_C * DFT_CK)
    gmat = np.block([[kron(c16), kron(s16)], [kron(-s16), kron(c16)]])
    return fa, tr, ti, jnp.asarray(gmat, F32)


def _position_dft(u):
    bx, w = u.shape[0], u.shape[-1]
    n_pos = DFT_A if u.ndim == 3 else math.prod(u.shape[1:-1])
    fa, tr, ti, gmat = _dft_tables(n_pos)
    fa_spec = _resident((2 * DFT_A, DFT_A))
    if u.ndim == 3:
        blk = pl.BlockSpec((1, DFT_A, w), lambda b: (0, b, 0))
        out = jax.ShapeDtypeStruct(u.shape, BF16)
        return pl.pallas_call(
            _dft_short_kernel,
            grid=(u.shape[1] // DFT_A,), in_specs=[fa_spec, blk], out_specs=[blk, blk], out_shape=[out, out],
            name="dft_context",
        )(fa, u)
    blk = pl.BlockSpec((1, DFT_C, DFT_A, w), lambda b: (b, 0, 0, 0))
    out = jax.ShapeDtypeStruct((bx, DFT_C, DFT_A, w), BF16)
    ur, ui = pl.pallas_call(
        _dft_long_kernel,
        grid=(bx,),
        in_specs=[fa_spec, _resident(gmat.shape), _resident(tr.shape), _resident(ti.shape), blk],
        out_specs=[blk, blk], out_shape=[out, out],
        scratch_shapes=[pltpu.VMEM((DFT_A // DFT_CK, DFT_C // 2, 2 * DFT_CK, w), BF16)] * 2,
        compiler_params=pltpu.CompilerParams(vmem_limit_bytes=VMEM_LIMIT_BYTES),
        name="dft_long",
    )(fa, gmat, tr, ti, u)
    return ur.reshape(bx, n_pos, w), ui.reshape(bx, n_pos, w)


def _tail_kernel(a_ref, ur_ref, ui_ref, x_ref, gate1_ref, sh_ref, sc_ref, gate2_ref,
                 gpost_mix_ref, gpre_ref, gpost_ref, wa_ref, wr_ref, wi_ref, wg_ref, wu_ref, wd_ref, o_ref,
                 *, ctx_row):
    gate1, shift, scale, gate2 = (_mod_row(r, ctx_row) for r in (gate1_ref, sh_ref, sc_ref, gate2_ref))
    gain_mix = gate1 * gpost_mix_ref[...]
    gain_in = (1.0 + scale) * gpre_ref[...]
    gain_out = gate2 * gpost_ref[...]
    hidden = wg_ref.shape[1]
    tm = x_ref.shape[1]
    per = min(TAIL_SLICE_ROWS, tm)
    assert tm % per == 0

    def mix_dots(rows):
        mix = jnp.dot(a_ref[0, rows, :], wa_ref[...], preferred_element_type=F32)
        mix += jnp.dot(ur_ref[0, rows, :], wr_ref[...], preferred_element_type=F32)
        mix += jnp.dot(ui_ref[0, rows, :], wi_ref[...], preferred_element_type=F32)
        return mix

    def norms(rows, mix):
        x_mid = x_ref[0, rows, :] + _rms(mix, gain_mix)
        return x_mid, (_rms(x_mid, gain_in) + shift).astype(BF16)

    def ffn(h):
        acc = jnp.zeros((h.shape[0], o_ref.shape[2]), F32)
        for lo in range(0, hidden, FFN_CHUNK):
            hi = min(lo + FFN_CHUNK, hidden)
            g = jnp.dot(h, wg_ref[:, lo:hi], preferred_element_type=F32)
            u = jnp.dot(h, wu_ref[:, lo:hi], preferred_element_type=F32)
            a = (g * jax.nn.sigmoid(g) * u).astype(BF16)
            acc += jnp.dot(a, wd_ref[lo:hi, :], preferred_element_type=F32)
        return acc

    def epilogue(rows, x_mid, acc):
        o_ref[0, rows, :] = x_mid + _rms(acc, gain_out)

    slices = [slice(r, r + per) for r in range(0, tm, per)]
    mixes = [mix_dots(rows) for rows in slices]
    staged = norms(slices[0], mixes[0])
    pending = None
    for i, rows in enumerate(slices):
        x_mid, h = staged
        acc = ffn(h)
        if i + 1 < len(slices):
            staged = norms(slices[i + 1], mixes[i + 1])
        if pending is not None:
            epilogue(*pending)
        pending = (rows, x_mid, acc)
    epilogue(*pending)


def _tail(layer, attn, ur, ui, x, mod, g_post_mix, g_pre_ffn, g_post_ffn, w_out, w_four_r, w_four_i,
          wg, wu, wd, ctx_row=None):
    bx, t, d = x.shape
    tm = min(TAIL_TILE, max(t // 2, TAIL_SLICE_ROWS) if bx == 1 else t)
    hidden = wg.shape[-1]
    tok = lambda width: pl.BlockSpec((1, tm, width), lambda b, i: (b, i, 0))
    gain = _layer_resident(layer, (1, d))
    wfour = _layer_resident(layer, (FOURIER_WIDTH, d))
    return pl.pallas_call(
        functools.partial(_tail_kernel, ctx_row=ctx_row),
        grid=(bx, t // tm),
        in_specs=[tok(ATTN_WIDTH), tok(FOURIER_WIDTH), tok(FOURIER_WIDTH), tok(d),
                  _mod_spec(layer, 2, d), _mod_spec(layer, 3, d), _mod_spec(layer, 4, d), _mod_spec(layer, 5, d),
                  gain, gain, gain,
                  _resident((ATTN_WIDTH, d)), wfour, wfour,
                  _resident((d, hidden)), _resident((d, hidden)), _resident((hidden, d))],
        out_specs=tok(d),
        out_shape=jax.ShapeDtypeStruct((bx, t, d), F32),
        compiler_params=pltpu.CompilerParams(vmem_limit_bytes=VMEM_LIMIT_BYTES),
        name="tail",
    )(attn, ur, ui, x, mod, mod, mod, mod, g_post_mix, g_pre_ffn, g_post_ffn,
      w_out, w_four_r, w_four_i, wg, wu, wd)


def _rope_tables(seq):
    t = np.arange(seq)
    freqs = ROPE_THETA ** (-np.arange(ROPE_PAIRS_PER_AXIS, dtype=np.float64) / ROPE_PAIRS_PER_AXIS)
    ang = np.concatenate([(t // GRID_W)[:, None] * freqs, (t % GRID_W)[:, None] * freqs], axis=-1)
    cos = np.repeat(np.cos(ang), 2, axis=-1)
    sin = np.repeat(np.sin(ang), 2, axis=-1) * np.tile([-1.0, 1.0], HEAD_DIM // 2)
    heads_per_row = LANES // HEAD_DIM
    return (jnp.asarray(np.tile(cos, (1, heads_per_row)), F32), jnp.asarray(np.tile(sin, (1, heads_per_row)), F32))


def kernel(x, c, ctx, c_ctx, w_ada, b_ada, norm_pre_mix, norm_post_mix, norm_pre_ffn, norm_post_ffn,
           w_in, w_out, w_fourier, sink, w_gate, w_up, w_down):
    batch, seq, d = x.shape
    depth = w_ada.shape[0]
    assert batch + 1 <= MOD_ROWS

    cc = jnp.zeros((MOD_ROWS, d), F32).at[:batch].set(c).at[batch].set(c_ctx)
    mod = _ada(cc, w_ada, b_ada)
    w_four_r, w_four_i = _fold_fourier_weights(w_fourier, w_out)
    rope = _rope_tables(seq)

    q_scale = np.ones((w_in.shape[-1],), np.float32)
    q_scale[:ATTN_WIDTH] = HEAD_DIM ** -0.5 * LOG2_E
    w_in_b = (w_in * q_scale).astype(BF16)
    gains = [g.reshape(depth, 1, d) for g in (norm_pre_mix, norm_post_mix, norm_pre_ffn, norm_post_ffn)]
    g_pre_mix, g_post_mix, g_pre_ffn, g_post_ffn = gains

    ctx_len = ctx.shape[1]
    assert ctx_len == DFT_A
    xc = ctx.reshape(1, batch * ctx_len, d)
    for i in range(depth):
        q, k, v, u = _project(i, x, mod, g_pre_mix, w_in_b, rope)
        qc, kc, vc, uc = _project(i, xc, mod, g_pre_mix, w_in_b, None, ctx_row=batch)
        casts = [(w_out, i, ATTN_WIDTH), (w_gate, i, d), (w_up, i, d), (w_down, i, w_down.shape[1])]
        attn, tail_weights = _attention(q, k, v, kc, vc, sink[i], ctx_len, local=True, casts=casts)
        w_attn_out, wg, wu, wd = tail_weights

        def finish(stream, attn, ur, ui, ctx_row):
            return _tail(i, attn, ur, ui, stream, mod, g_post_mix, g_pre_ffn, g_post_ffn,
                         w_attn_out, w_four_r, w_four_i, wg, wu, wd, ctx_row)

        ur, ui = _position_dft(u)
        x = finish(x, attn, ur, ui, None)

        if i < depth - 1:
            attn_c, _ = _attention(qc, None, None, kc, vc, sink[i], ctx_len, local=False)
            urc, uic = _position_dft(uc)
            xc = finish(xc, attn_c, urc, uic, batch)
    return x
```

```python
import functools
import math

import numpy as np
import jax
import jax.numpy as jnp
from jax import lax
from jax.experimental import pallas as pl
from jax.experimental.pallas import tpu as pltpu

F32 = jnp.float32
BF16 = jnp.bfloat16

GRID_W = 64
HEAD_DIM = 64
N_Q_HEADS = 8
N_KV_HEADS = 2
Q_PER_KV = N_Q_HEADS // N_KV_HEADS
ATTN_WIDTH = N_Q_HEADS * HEAD_DIM
KV_WIDTH = N_KV_HEADS * HEAD_DIM
N_FOURIER_GROUPS = 8
FOURIER_GROUP_DIM = 64
FOURIER_WIDTH = N_FOURIER_GROUPS * FOURIER_GROUP_DIM
WINDOW = 128
ROPE_THETA = 10000.0
ROPE_PAIRS_PER_AXIS = HEAD_DIM // 4
RMS_EPS = 1e-6
NEG_INF = -1e30
LOG2_E = math.log2(math.e)

LANES = 128
BF16_SUBLANES = 16
VMEM_LIMIT_BYTES = 60 * 1024 * 1024

DFT_A = 256
DFT_C = 16
DFT_CK = 8

MOD_ROWS = 8
ADA_COLS = 1536
PROJ_TILE = 1024
PROJ_SLICE_ROWS = 256
PROJ_RING = 3
FFN_CHUNK = 512
TAIL_TILE = 1024
TAIL_SLICE_ROWS = 512
ATTN_Q_TILE = 2048
ATTN_LOOKAHEAD = 2
VT_HEAD_ROWS = HEAD_DIM + BF16_SUBLANES
VT_ROWS = N_KV_HEADS * VT_HEAD_ROWS


def _resident(shape):
    zeros = (0,) * len(shape)
    return pl.BlockSpec(shape, lambda *_: zeros, pipeline_mode=pl.Buffered(1))


def _layer_resident(layer, shape):
    index = (layer,) + (0,) * len(shape)
    return pl.BlockSpec((None,) + tuple(shape), lambda *_: index, pipeline_mode=pl.Buffered(1))


def _mod_spec(layer, which, d):
    return pl.BlockSpec((None, MOD_ROWS, d), lambda *_: (layer, 0, which))


def _mod_row(ref, ctx_row):
    row = pl.program_id(0) if ctx_row is None else ctx_row
    return ref[pl.ds(row, 1), :]


def _rms(xf, g):
    ms = jnp.mean(xf * xf, axis=-1, keepdims=True)
    return xf * lax.rsqrt(ms + RMS_EPS) * g


def _split_bf16(x):
    hi = x.astype(BF16)
    return hi, (x - hi.astype(F32)).astype(BF16)


def _dot3(a, b):
    (a_hi, a_lo), (b_hi, b_lo) = a, b
    dot = functools.partial(jnp.dot, preferred_element_type=F32)
    return dot(a_hi, b_hi) + dot(a_lo, b_hi) + dot(a_hi, b_lo)


def _ada_kernel(c_ref, w_ref, b_ref, o_ref):
    c = c_ref[...]
    a = c * jax.nn.sigmoid(c)
    a_hi, a_lo = _split_bf16(a)
    w_hi, w_lo = _split_bf16(w_ref[0])
    rows = a.shape[0]
    r = jnp.dot(jnp.concatenate([a_hi, a_lo], axis=0), w_hi, preferred_element_type=F32)
    o_ref[0] = r[:rows] + r[rows:] + jnp.dot(a_hi, w_lo, preferred_element_type=F32) + b_ref[0]


def _ada(cc, w_ada, b_ada):
    depth, d, n = w_ada.shape
    tn = ADA_COLS
    assert n % tn == 0
    return pl.pallas_call(
        _ada_kernel,
        grid=(depth, n // tn),
        in_specs=[
            pl.BlockSpec((MOD_ROWS, d), lambda i, j: (0, 0)),
            pl.BlockSpec((1, d, tn), lambda i, j: (i, 0, j)),
            pl.BlockSpec((1, 1, tn), lambda i, j: (i, 0, j)),
        ],
        out_specs=pl.BlockSpec((1, MOD_ROWS, tn), lambda i, j: (i, 0, j)),
        out_shape=jax.ShapeDtypeStruct((depth, MOD_ROWS, n), F32),
        compiler_params=pltpu.CompilerParams(vmem_limit_bytes=VMEM_LIMIT_BYTES),
        name="ada",
    )(cc, w_ada, b_ada.reshape(depth, 1, n))


def _fold_kernel(c64_ref, s64_ref, wf_ref, wo_ref, wr_ref, wi_ref):
    gd = c64_ref.shape[0]
    for g in range(wf_ref.shape[0]):
        rows = slice(g * gd, (g + 1) * gd)
        t = _split_bf16(_dot3(_split_bf16(wf_ref[g]), _split_bf16(wo_ref[rows, :])))
        wr_ref[rows, :] = _dot3(_split_bf16(c64_ref[...]), t).astype(BF16)
        wi_ref[rows, :] = _dot3(_split_bf16(s64_ref[...]), t).astype(BF16)


def _fold_fourier_weights(w_fourier, w_out):
    depth, groups, gd, _ = w_fourier.shape
    d = w_out.shape[-1]
    width = groups * gd
    assert w_out.shape[1] == ATTN_WIDTH + width and ATTN_WIDTH == width
    j = np.arange(gd)
    ang = 2.0 * np.pi * ((j[:, None] * j[None, :]) % gd) / gd
    c64 = jnp.asarray(np.cos(ang), F32)
    s64 = jnp.asarray(np.sin(ang), F32)
    out = jax.ShapeDtypeStruct((depth, width, d), BF16)
    return pl.pallas_call(
        _fold_kernel,
        grid=(depth,),
        in_specs=[
            pl.BlockSpec((gd, gd), lambda i: (0, 0)),
            pl.BlockSpec((gd, gd), lambda i: (0, 0)),
            pl.BlockSpec((None, groups, gd, gd), lambda i: (i, 0, 0, 0)),
            pl.BlockSpec((None, width, d), lambda i: (i, 1, 0)),
        ],
        out_specs=[pl.BlockSpec((None, width, d), lambda i: (i, 0, 0))] * 2,
        out_shape=[out, out],
        name="fold_fourier",
    )(c64, s64, w_fourier, w_out)


def _proj_kernel(x_ref, sh_ref, sc_ref, g_ref, w_ref, *rest, rope, ctx_row, grid):
    if rope:
        cos_ref, sin_ref, q_ref, k_ref, v_ref, u_ref, u_scratch, x_ring, x_sems = rest
        x_hbm, tm = x_ref, x_ring.shape[1]
        n_batch, nb = grid
        step = pl.program_id(0) * nb + pl.program_id(1)
        n_steps = n_batch * nb

        def x_copy(s):
            slot = lax.rem(s, PROJ_RING)
            src = x_hbm.at[s // nb, pl.ds(pl.multiple_of(lax.rem(s, nb) * tm, tm), tm), :]
            return pltpu.make_async_copy(src, x_ring.at[slot], x_sems.at[slot])

        @pl.when(step == 0)
        def _prime():
            for s in range(PROJ_RING - 1):
                x_copy(jnp.int32(s)).start()

        @pl.when(step + (PROJ_RING - 1) < n_steps)
        def _prefetch():
            x_copy(step + (PROJ_RING - 1)).start()

        x_copy(step).wait()
        x_tile = x_ring.at[lax.rem(step, PROJ_RING)]
    else:
        q_ref, k_ref, v_ref, u_ref = rest
        tm = x_ref.shape[1]
        x_tile = x_ref.at[0]
    gain = (1.0 + _mod_row(sc_ref, ctx_row)) * g_ref[...]
    shift = _mod_row(sh_ref, ctx_row)
    k_lo = ATTN_WIDTH
    v_lo = ATTN_WIDTH + KV_WIDTH
    u_lo = ATTN_WIDTH + 2 * KV_WIDTH
    per = min(PROJ_SLICE_ROWS, tm)
    assert tm % per == 0 and per % DFT_C == 0
    if rope:
        even_lane = (lax.broadcasted_iota(jnp.int32, (per, LANES), 1) & 1) == 0

    def norm(rows):
        return (_rms(x_tile[rows, :], gain) + shift).astype(BF16)

    def emit(rows, p):
        if rope:
            cos = cos_ref[rows, :]
            sin = sin_ref[rows, :]

            def rot(xs):
                partner = jnp.where(even_lane, pltpu.roll(xs, LANES - 1, 1), pltpu.roll(xs, 1, 1))
                return xs * cos + partner * sin

            for j in range(ATTN_WIDTH // LANES):
                q_ref[0, rows, j * LANES:(j + 1) * LANES] = rot(p[:, j * LANES:(j + 1) * LANES]).astype(BF16)
            k_ref[0, rows, :] = rot(p[:, k_lo:v_lo]).astype(BF16)
        else:
            q_ref[0, rows, :] = p[:, :k_lo].astype(BF16)
            k_ref[0, rows, :] = p[:, k_lo:v_lo].astype(BF16)
        vt = p[:, v_lo:u_lo].T.astype(BF16)
        ones = jnp.ones((VT_HEAD_ROWS - HEAD_DIM, per), BF16)
        for h_kv in range(N_KV_HEADS):
            v_ref[0, h_kv * VT_HEAD_ROWS:h_kv * VT_HEAD_ROWS + HEAD_DIM, rows] = (
                vt[h_kv * HEAD_DIM:(h_kv + 1) * HEAD_DIM])
            v_ref[0, h_kv * VT_HEAD_ROWS + HEAD_DIM:(h_kv + 1) * VT_HEAD_ROWS, rows] = ones
        if rope:
            n1 = slice(rows.start // DFT_C, rows.stop // DFT_C)
            for c in range(u_scratch.shape[0]):
                u_scratch[c, rows, :] = p[:, u_lo + c * LANES:u_lo + (c + 1) * LANES]
                for n2 in range(DFT_C):
                    u_ref[0, n2, n1, c * LANES:(c + 1) * LANES] = (
                        u_scratch[c, pl.ds(rows.start + n2, per // DFT_C, stride=DFT_C), :].astype(BF16))
        else:
            u_ref[0, rows, :] = p[:, u_lo:].astype(BF16)

    slices = [slice(r, r + per) for r in range(0, tm, per)]
    h = norm(slices[0])
    pending = None
    for i, rows in enumerate(slices):
        p = jnp.dot(h, w_ref[...], preferred_element_type=F32)
        if i + 1 < len(slices):
            h = norm(slices[i + 1])
        if pending is not None:
            emit(*pending)
        pending = (rows, p)
    emit(*pending)


def _project(layer, x, mod, gains, w_in, rope_tables, ctx_row=None):
    bx, t, d = x.shape
    tm = min(PROJ_TILE, t)
    n_in = w_in.shape[-1]
    rope = rope_tables is not None
    tok = lambda width: pl.BlockSpec((1, tm, width), lambda b, i: (b, i, 0))
    in_specs = [tok(d), _mod_spec(layer, 0, d), _mod_spec(layer, 1, d),
                _layer_resident(layer, (1, d)), _layer_resident(layer, (d, n_in))]
    args = [x, mod, mod, gains, w_in]
    if rope:
        in_specs += [pl.BlockSpec((tm, LANES), lambda b, i: (i, 0))] * 2
        args += list(rope_tables)
    widths = (ATTN_WIDTH, KV_WIDTH, KV_WIDTH, FOURIER_WIDTH)
    out_specs = [tok(w) for w in widths]
    out_shape = [jax.ShapeDtypeStruct((bx, t, w), BF16) for w in widths]
    out_specs[2] = pl.BlockSpec((1, VT_ROWS, tm), lambda b, i: (b, 0, i))
    out_shape[2] = jax.ShapeDtypeStruct((bx, VT_ROWS, t), BF16)
    scratch = []
    if rope:
        assert t == DFT_A * DFT_C and tm % DFT_C == 0
        out_specs[3] = pl.BlockSpec((1, DFT_C, tm // DFT_C, FOURIER_WIDTH), lambda b, i: (b, 0, i, 0))
        out_shape[3] = jax.ShapeDtypeStruct((bx, DFT_C, DFT_A, FOURIER_WIDTH), BF16)
        assert bx * (t // tm) >= PROJ_RING
        in_specs[0] = pl.BlockSpec(memory_space=pl.ANY)
        scratch = [pltpu.VMEM((FOURIER_WIDTH // LANES, tm, LANES), F32),
                   pltpu.VMEM((PROJ_RING, tm, d), F32), pltpu.SemaphoreType.DMA((PROJ_RING,))]
    grid = (bx, t // tm)
    return pl.pallas_call(
        functools.partial(_proj_kernel, rope=rope, ctx_row=ctx_row, grid=grid),
        grid=grid,
        in_specs=in_specs,
        out_specs=out_specs,
        out_shape=out_shape,
        scratch_shapes=scratch,
        compiler_params=pltpu.CompilerParams(vmem_limit_bytes=VMEM_LIMIT_BYTES,
                                             dimension_semantics=("arbitrary", "arbitrary")),
        name="project_rope" if rope else "project",
    )(*args)


def _attn_kernel(sink_ref, q_ref, *rest, local, n_cast):
    w = WINDOW
    n_in = len(rest) - 1 - 2 * n_cast
    cast_src = rest[n_in:n_in + n_cast]
    cast_dst = rest[n_in + n_cast + 1:]
    rest = rest[:n_in] + rest[n_in + n_cast:n_in + n_cast + 1]
    for src, dst in zip(cast_src, cast_dst):
        dst[...] = src[...].astype(BF16)
    if local:
        kp_ref, kc_ref, kn_ref, vp_ref, vc_ref, vn_ref, kx_ref, vx_ref, o_ref = rest
        k_all = jnp.concatenate([kp_ref[0], kc_ref[0], kn_ref[0]], axis=0)
        vt_all = jnp.concatenate([vp_ref[0], vc_ref[0], vn_ref[0]], axis=1)
        step = pl.program_id(1)
        last = pl.num_programs(1) - 1
    else:
        kx_ref, vx_ref, o_ref = rest
    kx = kx_ref[0]
    vxt = vx_ref[0]
    n_sub = q_ref.shape[1] // w
    cols = Q_PER_KV * w
    group = lax.shift_right_logical(lax.broadcasted_iota(jnp.int32, (1, cols), 1), int(math.log2(w)))
    if local:
        kj = lax.broadcasted_iota(jnp.int32, (w, cols), 0)
        qi = lax.broadcasted_iota(jnp.int32, (w, cols), 1) & (w - 1)
        far = 1 << 20
        bias_prev = jnp.where(kj >= qi, 0.0, NEG_INF)
        bias_next = jnp.where(kj <= qi, 0.0, NEG_INF)
        bias_prev_first = jnp.where(kj >= qi + jnp.where(step > 0, 0, far), 0.0, NEG_INF)
        bias_next_last = jnp.where(kj <= qi - jnp.where(step < last, 0, far), 0.0, NEG_INF)
    contract_last = (((1,), (1,)), ((), ()))

    def sink_row(h):
        sink = jnp.full((1, cols), sink_ref[Q_PER_KV * h] * LOG2_E, F32)
        for g in range(1, Q_PER_KV):
            sink = jnp.where(group == g, sink_ref[Q_PER_KV * h + g] * LOG2_E, sink)
        return sink

    sinks = [sink_row(h) for h in range(N_KV_HEADS)]

    def scores(h, s):
        hs = slice(h * HEAD_DIM, (h + 1) * HEAD_DIM)
        q = q_ref[0, s * w:(s + 1) * w, :]
        qs = jnp.concatenate([q[:, (Q_PER_KV * h + g) * HEAD_DIM:(Q_PER_KV * h + g + 1) * HEAD_DIM]
                              for g in range(Q_PER_KV)], axis=0)
        pieces = []
        if local:
            sl = lax.dot_general(k_all[s * w:(s + 3) * w, hs], qs, contract_last, preferred_element_type=F32)
            pieces += [sl[:w] + (bias_prev_first if s == 0 else bias_prev), sl[w:2 * w],
                       sl[2 * w:] + (bias_next_last if s == n_sub - 1 else bias_next)]
        pieces.append(lax.dot_general(kx[:, hs], qs, contract_last, preferred_element_type=F32))
        m = sinks[h]
        for x in pieces:
            m = jnp.maximum(m, jnp.max(x, axis=0, keepdims=True))
        return pieces, m

    def finish(h, s, pieces, m):
        vs = slice(h * VT_HEAD_ROWS, (h + 1) * VT_HEAD_ROWS)
        probs = [jnp.exp2(x - m).astype(BF16) for x in pieces]
        ot = jnp.dot(vxt[vs, :], probs[-1], preferred_element_type=F32)
        if local:
            ot = ot + jnp.dot(vt_all[vs, s * w:(s + 3) * w], jnp.concatenate(probs[:3], axis=0),
                              preferred_element_type=F32)
        denom = ot[HEAD_DIM:HEAD_DIM + 1] + jnp.exp2(sinks[h] - m)
        ot = ot[:HEAD_DIM] / denom
        o = jnp.concatenate([ot[:, g * w:(g + 1) * w].T for g in range(Q_PER_KV)], axis=1)
        lo = h * Q_PER_KV * HEAD_DIM
        o_ref[0, s * w:(s + 1) * w, lo:lo + Q_PER_KV * HEAD_DIM] = o.astype(BF16)

    work = [(h, s) for h in range(N_KV_HEADS) for s in range(n_sub)]
    queue = [scores(*item) for item in work[:ATTN_LOOKAHEAD]]
    for i, (h, s) in enumerate(work):
        if i + ATTN_LOOKAHEAD < len(work):
            queue.append(scores(*work[i + ATTN_LOOKAHEAD]))
        finish(h, s, *queue.pop(0))


def _attention(q, k, v, kx, vx, sink, cx, local, casts=()):
    smem = pl.BlockSpec(memory_space=pltpu.SMEM)
    kx_spec = pl.BlockSpec((1, cx, KV_WIDTH), lambda b, n: (0, b, 0))
    vx_spec = pl.BlockSpec((1, VT_ROWS, cx), lambda b, n: (0, 0, b))
    if local:
        bx, t, _ = q.shape
        tq = min(ATTN_Q_TILE, t)
        q_spec = pl.BlockSpec((1, tq, ATTN_WIDTH), lambda b, n: (b, n, 0))
    else:
        bx, t, tq = q.shape[1] // cx, cx, cx
        q_spec = pl.BlockSpec((1, tq, ATTN_WIDTH), lambda b, n: (0, b, 0))
    nb = t // tq
    if local:
        per = tq // WINDOW
        n_win = t // WINDOW
        before = lambda n: jnp.maximum(n * per - 1, 0)
        after = lambda n: jnp.minimum((n + 1) * per, n_win - 1)
        k_specs = [pl.BlockSpec((1, WINDOW, KV_WIDTH), lambda b, n: (b, before(n), 0)),
                   pl.BlockSpec((1, tq, KV_WIDTH), lambda b, n: (b, n, 0)),
                   pl.BlockSpec((1, WINDOW, KV_WIDTH), lambda b, n: (b, after(n), 0))]
        v_specs = [pl.BlockSpec((1, VT_ROWS, WINDOW), lambda b, n: (b, 0, before(n))),
                   pl.BlockSpec((1, VT_ROWS, tq), lambda b, n: (b, 0, n)),
                   pl.BlockSpec((1, VT_ROWS, WINDOW), lambda b, n: (b, 0, after(n)))]
        in_specs = [smem, q_spec] + k_specs + v_specs + [kx_spec, vx_spec]
        args = [sink, q, k, k, k, v, v, v, kx, vx]
    else:
        in_specs = [smem, q_spec, kx_spec, vx_spec]
        args = [sink, q, kx, vx]
    out_specs = [q_spec]
    out_shape = [jax.ShapeDtypeStruct(q.shape, BF16)]
    steps = bx * nb
    for w, layer, rows in casts:
        slab = rows // steps
        assert rows % steps == 0 and slab % BF16_SUBLANES == 0
        in_specs.append(pl.BlockSpec((None, slab, w.shape[-1]), lambda b, n, layer=layer: (layer, b * nb + n, 0)))
        out_specs.append(pl.BlockSpec((slab, w.shape[-1]), lambda b, n: (b * nb + n, 0)))
        out_shape.append(jax.ShapeDtypeStruct((rows, w.shape[-1]), BF16))
        args.append(w)
    outs = pl.pallas_call(
        functools.partial(_attn_kernel, local=local, n_cast=len(casts)),
        grid=(bx, nb),
        in_specs=in_specs,
        out_specs=out_specs,
        out_shape=out_shape,
        compiler_params=pltpu.CompilerParams(vmem_limit_bytes=VMEM_LIMIT_BYTES),
        name="attn_window" if local else "attn_context",
    )(*args)
    return outs[0], outs[1:]


def _dft_short_kernel(fa_ref, u_ref, ur_ref, ui_ref):
    y = jnp.dot(fa_ref[...].astype(BF16), u_ref[0], preferred_element_type=F32)
    ur_ref[0] = y[:DFT_A].astype(BF16)
    ui_ref[0] = y[DFT_A:].astype(BF16)


def _dft_long_kernel(fa_ref, g_ref, tr_ref, ti_ref, u_ref, ur_ref, ui_ref, yr_s, yi_s):
    kb = DFT_CK
    fa = fa_ref[...].astype(BF16)
    for a in range(DFT_C // 2):
        ys = [jnp.dot(fa, u_ref[0, 2 * a + e], preferred_element_type=F32) for e in range(2)]
        for c in range(ys[0].shape[1] // LANES):
            sl = slice(c * LANES, (c + 1) * LANES)
            re_im = []
            for e in range(2):
                tr, ti = tr_ref[2 * a + e], ti_ref[2 * a + e]
                yr, yi = ys[e][:DFT_A, sl], ys[e][DFT_A:, sl]
                re_im.append(((yr * tr - yi * ti).reshape(DFT_A // kb, kb, LANES),
                              (yr * ti + yi * tr).reshape(DFT_A // kb, kb, LANES)))
            yr_s[:, a, :, sl] = jnp.concatenate([re_im[0][0], re_im[1][0]], axis=1).astype(BF16)
            yi_s[:, a, :, sl] = jnp.concatenate([re_im[0][1], re_im[1][1]], axis=1).astype(BF16)
    g = g_ref[...].astype(BF16)
    half = DFT_C * kb
    w = ur_ref.shape[-1]
    for j in range(DFT_A // (2 * kb)):
        rs = []
        for e in range(2):
            y = jnp.concatenate([yr_s[2 * j + e].reshape(half, w), yi_s[2 * j + e].reshape(half, w)], axis=0)
            rs.append(jnp.dot(g, y, preferred_element_type=F32))
        sl = slice(j * 2 * kb, (j + 1) * 2 * kb)
        for ref, lo in ((ur_ref, 0), (ui_ref, half)):
            ref[0, :, sl, :] = jnp.concatenate(
                [r[lo:lo + half].reshape(DFT_C, kb, w) for r in rs], axis=1).astype(BF16)


def _dft_tables(n_pos):
    a = np.arange(DFT_A)
    ang = 2.0 * np.pi * ((a[:, None] * a[None, :]) % DFT_A) / DFT_A
    norm = 1.0 / math.sqrt(n_pos * FOURIER_GROUP_DIM)
    fa = jnp.asarray(np.concatenate([np.cos(ang), -np.sin(ang)], axis=0) * norm, F32)
    if n_pos == DFT_A:
        return fa, None, None, None
    assert n_pos == DFT_A * DFT_C
    n2 = np.arange(DFT_C)
    tw = 2.0 * np.pi * (n2[:, None] * a[None, :]) / n_pos
    tr = jnp.asarray(np.repeat(np.cos(tw)[:, :, None], LANES, axis=2), F32)
    ti = jnp.asarray(np.repeat(-np.sin(tw)[:, :, None], LANES, axis=2), F32)
    ang16 = 2.0 * np.pi * ((n2[:, None] * n2[None, :]) % DFT_C) / DFT_C
    c16, s16 = np.cos(ang16), np.sin(ang16)
    eye = np.eye(DFT_CK)
    kron = lambda f: np.einsum("kn,ab->kanb", f, eye).reshape(DFT_C * DFT_CK, DFT_C * DFT_CK)
    gmat = np.block([[kron(c16), kron(s16)], [kron(-s16), kron(c16)]])
    return fa, tr, ti, jnp.asarray(gmat, F32)


def _position_dft(u):
    bx, w = u.shape[0], u.shape[-1]
    n_pos = DFT_A if u.ndim == 3 else math.prod(u.shape[1:-1])
    fa, tr, ti, gmat = _dft_tables(n_pos)
    fa_spec = _resident((2 * DFT_A, DFT_A))
    if u.ndim == 3:
        blk = pl.BlockSpec((1, DFT_A, w), lambda b: (0, b, 0))
        out = jax.ShapeDtypeStruct(u.shape, BF16)
        return pl.pallas_call(
            _dft_short_kernel,
            grid=(u.shape[1] // DFT_A,), in_specs=[fa_spec, blk], out_specs=[blk, blk], out_shape=[out, out],
            name="dft_context",
        )(fa, u)
    blk = pl.BlockSpec((1, DFT_C, DFT_A, w), lambda b: (b, 0, 0, 0))
    out = jax.ShapeDtypeStruct((bx, DFT_C, DFT_A, w), BF16)
    ur, ui = pl.pallas_call(
        _dft_long_kernel,
        grid=(bx,),
        in_specs=[fa_spec, _resident(gmat.shape), _resident(tr.shape), _resident(ti.shape), blk],
        out_specs=[blk, blk], out_shape=[out, out],
        scratch_shapes=[pltpu.VMEM((DFT_A // DFT_CK, DFT_C // 2, 2 * DFT_CK, w), BF16)] * 2,
        compiler_params=pltpu.CompilerParams(vmem_limit_bytes=VMEM_LIMIT_BYTES),
        name="dft_long",
    )(fa, gmat, tr, ti, u)
    return ur.reshape(bx, n_pos, w), ui.reshape(bx, n_pos, w)


def _tail_kernel(a_ref, ur_ref, ui_ref, x_ref, gate1_ref, sh_ref, sc_ref, gate2_ref,
                 gpost_mix_ref, gpre_ref, gpost_ref, wa_ref, wr_ref, wi_ref, wg_ref, wu_ref, wd_ref, o_ref,
                 *, ctx_row):
    gate1, shift, scale, gate2 = (_mod_row(r, ctx_row) for r in (gate1_ref, sh_ref, sc_ref, gate2_ref))
    gain_mix = gate1 * gpost_mix_ref[...]
    gain_in = (1.0 + scale) * gpre_ref[...]
    gain_out = gate2 * gpost_ref[...]
    hidden = wg_ref.shape[1]
    tm = x_ref.shape[1]
    per = min(TAIL_SLICE_ROWS, tm)
    assert tm % per == 0

    def mix_dots(rows):
        mix = jnp.dot(a_ref[0, rows, :], wa_ref[...], preferred_element_type=F32)
        mix += jnp.dot(ur_ref[0, rows, :], wr_ref[...], preferred_element_type=F32)
        mix += jnp.dot(ui_ref[0, rows, :], wi_ref[...], preferred_element_type=F32)
        return mix

    def norms(rows, mix):
        x_mid = x_ref[0, rows, :] + _rms(mix, gain_mix)
        return x_mid, (_rms(x_mid, gain_in) + shift).astype(BF16)

    def ffn(h):
        acc = jnp.zeros((h.shape[0], o_ref.shape[2]), F32)
        for lo in range(0, hidden, FFN_CHUNK):
            hi = min(lo + FFN_CHUNK, hidden)
            g = jnp.dot(h, wg_ref[:, lo:hi], preferred_element_type=F32)
            u = jnp.dot(h, wu_ref[:, lo:hi], preferred_element_type=F32)
            a = (g * jax.nn.sigmoid(g) * u).astype(BF16)
            acc += jnp.dot(a, wd_ref[lo:hi, :], preferred_element_type=F32)
        return acc

    def epilogue(rows, x_mid, acc):
        o_ref[0, rows, :] = x_mid + _rms(acc, gain_out)

    slices = [slice(r, r + per) for r in range(0, tm, per)]
    mixes = [mix_dots(rows) for rows in slices]
    staged = norms(slices[0], mixes[0])
    pending = None
    for i, rows in enumerate(slices):
        x_mid, h = staged
        acc = ffn(h)
        if i + 1 < len(slices):
            staged = norms(slices[i + 1], mixes[i + 1])
        if pending is not None:
            epilogue(*pending)
        pending = (rows, x_mid, acc)
    epilogue(*pending)


def _tail(layer, attn, ur, ui, x, mod, g_post_mix, g_pre_ffn, g_post_ffn, w_out, w_four_r, w_four_i,
          wg, wu, wd, ctx_row=None):
    bx, t, d = x.shape
    tm = min(TAIL_TILE, max(t // 2, TAIL_SLICE_ROWS) if bx == 1 else t)
    hidden = wg.shape[-1]
    tok = lambda width: pl.BlockSpec((1, tm, width), lambda b, i: (b, i, 0))
    gain = _layer_resident(layer, (1, d))
    wfour = _layer_resident(layer, (FOURIER_WIDTH, d))
    return pl.pallas_call(
        functools.partial(_tail_kernel, ctx_row=ctx_row),
        grid=(bx, t // tm),
        in_specs=[tok(ATTN_WIDTH), tok(FOURIER_WIDTH), tok(FOURIER_WIDTH), tok(d),
                  _mod_spec(layer, 2, d), _mod_spec(layer, 3, d), _mod_spec(layer, 4, d), _mod_spec(layer, 5, d),
                  gain, gain, gain,
                  _resident((ATTN_WIDTH, d)), wfour, wfour,
                  _resident((d, hidden)), _resident((d, hidden)), _resident((hidden, d))],
        out_specs=tok(d),
        out_shape=jax.ShapeDtypeStruct((bx, t, d), F32),
        compiler_params=pltpu.CompilerParams(vmem_limit_bytes=VMEM_LIMIT_BYTES),
        name="tail",
    )(attn, ur, ui, x, mod, mod, mod, mod, g_post_mix, g_pre_ffn, g_post_ffn,
      w_out, w_four_r, w_four_i, wg, wu, wd)


def _rope_tables(seq):
    t = np.arange(seq)
    freqs = ROPE_THETA ** (-np.arange(ROPE_PAIRS_PER_AXIS, dtype=np.float64) / ROPE_PAIRS_PER_AXIS)
    ang = np.concatenate([(t // GRID_W)[:, None] * freqs, (t % GRID_W)[:, None] * freqs], axis=-1)
    cos = np.repeat(np.cos(ang), 2, axis=-1)
    sin = np.repeat(np.sin(ang), 2, axis=-1) * np.tile([-1.0, 1.0], HEAD_DIM // 2)
    heads_per_row = LANES // HEAD_DIM
    return (jnp.asarray(np.tile(cos, (1, heads_per_row)), F32), jnp.asarray(np.tile(sin, (1, heads_per_row)), F32))


def kernel(x, c, ctx, c_ctx, w_ada, b_ada, norm_pre_mix, norm_post_mix, norm_pre_ffn, norm_post_ffn,
           w_in, w_out, w_fourier, sink, w_gate, w_up, w_down):
    batch, seq, d = x.shape
    depth = w_ada.shape[0]
    assert batch + 1 <= MOD_ROWS

    cc = jnp.zeros((MOD_ROWS, d), F32).at[:batch].set(c).at[batch].set(c_ctx)
    mod = _ada(cc, w_ada, b_ada)
    w_four_r, w_four_i = _fold_fourier_weights(w_fourier, w_out)
    rope = _rope_tables(seq)

    q_scale = np.ones((w_in.shape[-1],), np.float32)
    q_scale[:ATTN_WIDTH] = HEAD_DIM ** -0.5 * LOG2_E
    w_in_b = (w_in * q_scale).astype(BF16)
    gains = [g.reshape(depth, 1, d) for g in (norm_pre_mix, norm_post_mix, norm_pre_ffn, norm_post_ffn)]
    g_pre_mix, g_post_mix, g_pre_ffn, g_post_ffn = gains

    ctx_len = ctx.shape[1]
    assert ctx_len == DFT_A
    xc = ctx.reshape(1, batch * ctx_len, d)
    for i in range(depth):
        q, k, v, u = _project(i, x, mod, g_pre_mix, w_in_b, rope)
        qc, kc, vc, uc = _project(i, xc, mod, g_pre_mix, w_in_b, None, ctx_row=batch)
        casts = [(w_out, i, ATTN_WIDTH), (w_gate, i, d), (w_up, i, d), (w_down, i, w_down.shape[1])]
        attn, tail_weights = _attention(q, k, v, kc, vc, sink[i], ctx_len, local=True, casts=casts)
        w_attn_out, wg, wu, wd = tail_weights

        def finish(stream, attn, ur, ui, ctx_row):
            return _tail(i, attn, ur, ui, stream, mod, g_post_mix, g_pre_ffn, g_post_ffn,
                         w_attn_out, w_four_r, w_four_i, wg, wu, wd, ctx_row)

        ur, ui = _position_dft(u)
        x = finish(x, attn, ur, ui, None)

        if i < depth - 1:
            attn_c, _ = _attention(qc, None, None, kc, vc, sink[i], ctx_len, local=False)
            urc, uic = _position_dft(uc)
            xc = finish(xc, attn_c, urc, uic, batch)
    return x
```

```python
import functools
import math

import numpy as np
import jax
import jax.numpy as jnp
from jax import lax
from jax.experimental import pallas as pl
from jax.experimental.pallas import tpu as pltpu

F32 = jnp.float32
BF16 = jnp.bfloat16

GRID_W = 64
HEAD_DIM = 64
N_Q_HEADS = 8
N_KV_HEADS = 2
Q_PER_KV = N_Q_HEADS // N_KV_HEADS
ATTN_WIDTH = N_Q_HEADS * HEAD_DIM
KV_WIDTH = N_KV_HEADS * HEAD_DIM
N_FOURIER_GROUPS = 8
FOURIER_GROUP_DIM = 64
FOURIER_WIDTH = N_FOURIER_GROUPS * FOURIER_GROUP_DIM
WINDOW = 128
ROPE_THETA = 10000.0
ROPE_PAIRS_PER_AXIS = HEAD_DIM // 4
RMS_EPS = 1e-6
N_MOD = 6
NEG_INF = -1e30
LOG2_E = math.log2(math.e)

LANES = 128
BF16_SUBLANES = 16
VMEM_LIMIT_BYTES = 60 * 1024 * 1024

DFT_A = 256
DFT_C = 16
DFT_CK = 8

MOD_ROWS = 8
ADA_COLS = 1536
PROJ_TILE = 1024
PROJ_SLICE_ROWS = 512
FFN_CHUNK = 512
TAIL_TILE = 1024
TAIL_SLICE_ROWS = 512
ATTN_Q_TILE = 2048
ATTN_LOOKAHEAD = 1
VT_HEAD_ROWS = HEAD_DIM + BF16_SUBLANES
VT_ROWS = N_KV_HEADS * VT_HEAD_ROWS


def _resident(shape):
    zeros = (0,) * len(shape)
    return pl.BlockSpec(shape, lambda *_: zeros, pipeline_mode=pl.Buffered(1))


def _layer_resident(layer, shape, index=None):
    index = (0,) * len(shape) if index is None else index
    return pl.BlockSpec((None,) + tuple(shape), lambda *_: (layer,) + tuple(index), pipeline_mode=pl.Buffered(1))


def _mod_spec(layer, which, d):
    return pl.BlockSpec((None, MOD_ROWS, d), lambda *_: (layer, 0, which))


def _mod_row(ref, ctx_row):
    row = pl.program_id(0) if ctx_row is None else ctx_row
    return ref[pl.ds(row, 1), :]


def _rms(xf, g):
    ms = jnp.mean(xf * xf, axis=-1, keepdims=True)
    return xf * lax.rsqrt(ms + RMS_EPS) * g


def _split_bf16(x):
    hi = x.astype(BF16)
    return hi, (x - hi.astype(F32)).astype(BF16)


def _dot3(a, b):
    (a_hi, a_lo), (b_hi, b_lo) = a, b
    dot = functools.partial(jnp.dot, preferred_element_type=F32)
    return dot(a_hi, b_hi) + dot(a_lo, b_hi) + dot(a_hi, b_lo)


def _ada_kernel(c_ref, w_ref, b_ref, o_ref):
    c = c_ref[...]
    a = c * jax.nn.sigmoid(c)
    a_hi, a_lo = _split_bf16(a)
    w_hi, w_lo = _split_bf16(w_ref[0])
    rows = a.shape[0]
    r = jnp.dot(jnp.concatenate([a_hi, a_lo], axis=0), w_hi, preferred_element_type=F32)
    o_ref[0] = r[:rows] + r[rows:] + jnp.dot(a_hi, w_lo, preferred_element_type=F32) + b_ref[0]


def _ada(cc, w_ada, b_ada):
    depth, d, n = w_ada.shape
    tn = ADA_COLS
    assert n % tn == 0
    return pl.pallas_call(
        _ada_kernel,
        grid=(depth, n // tn),
        in_specs=[
            pl.BlockSpec((MOD_ROWS, d), lambda i, j: (0, 0)),
            pl.BlockSpec((1, d, tn), lambda i, j: (i, 0, j)),
            pl.BlockSpec((1, 1, tn), lambda i, j: (i, 0, j)),
        ],
        out_specs=pl.BlockSpec((1, MOD_ROWS, tn), lambda i, j: (i, 0, j)),
        out_shape=jax.ShapeDtypeStruct((depth, MOD_ROWS, n), F32),
        compiler_params=pltpu.CompilerParams(vmem_limit_bytes=VMEM_LIMIT_BYTES),
        name="ada",
    )(cc, w_ada, b_ada.reshape(depth, 1, n))


def _fold_kernel(c64_ref, s64_ref, wf_ref, wo_ref, wr_ref, wi_ref):
    gd = c64_ref.shape[0]
    for g in range(wf_ref.shape[0]):
        rows = slice(g * gd, (g + 1) * gd)
        t = _split_bf16(_dot3(_split_bf16(wf_ref[g]), _split_bf16(wo_ref[rows, :])))
        wr_ref[rows, :] = _dot3(_split_bf16(c64_ref[...]), t).astype(BF16)
        wi_ref[rows, :] = _dot3(_split_bf16(s64_ref[...]), t).astype(BF16)


def _fold_fourier_weights(w_fourier, w_out):
    depth, groups, gd, _ = w_fourier.shape
    d = w_out.shape[-1]
    width = groups * gd
    assert w_out.shape[1] == ATTN_WIDTH + width and ATTN_WIDTH == width
    j = np.arange(gd)
    ang = 2.0 * np.pi * ((j[:, None] * j[None, :]) % gd) / gd
    c64 = jnp.asarray(np.cos(ang), F32)
    s64 = jnp.asarray(np.sin(ang), F32)
    out = jax.ShapeDtypeStruct((depth, width, d), BF16)
    return pl.pallas_call(
        _fold_kernel,
        grid=(depth,),
        in_specs=[
            pl.BlockSpec((gd, gd), lambda i: (0, 0)),
            pl.BlockSpec((gd, gd), lambda i: (0, 0)),
            pl.BlockSpec((None, groups, gd, gd), lambda i: (i, 0, 0, 0)),
            pl.BlockSpec((None, width, d), lambda i: (i, 1, 0)),
        ],
        out_specs=[pl.BlockSpec((None, width, d), lambda i: (i, 0, 0))] * 2,
        out_shape=[out, out],
        name="fold_fourier",
    )(c64, s64, w_fourier, w_out)


def _proj_kernel(x_ref, sh_ref, sc_ref, g_ref, w_ref, *rest, rope, ctx_row):
    if rope:
        cos_ref, sin_ref, q_ref, k_ref, v_ref, u_ref, u_scratch = rest
    else:
        q_ref, k_ref, v_ref, u_ref = rest
    gain = (1.0 + _mod_row(sc_ref, ctx_row)) * g_ref[...]
    shift = _mod_row(sh_ref, ctx_row)
    k_lo = ATTN_WIDTH
    v_lo = ATTN_WIDTH + KV_WIDTH
    u_lo = ATTN_WIDTH + 2 * KV_WIDTH
    tm = x_ref.shape[1]
    per = min(PROJ_SLICE_ROWS, tm)
    assert tm % per == 0 and per % DFT_C == 0
    if rope:
        even_lane = (lax.broadcasted_iota(jnp.int32, (per, LANES), 1) & 1) == 0

    def norm(rows):
        return (_rms(x_ref[0, rows, :], gain) + shift).astype(BF16)

    def emit(rows, p):
        if rope:
            cos = cos_ref[rows, :]
            sin = sin_ref[rows, :]

            def rot(xs):
                partner = jnp.where(even_lane, pltpu.roll(xs, LANES - 1, 1), pltpu.roll(xs, 1, 1))
                return xs * cos + partner * sin

            for j in range(ATTN_WIDTH // LANES):
                q_ref[0, rows, j * LANES:(j + 1) * LANES] = rot(p[:, j * LANES:(j + 1) * LANES]).astype(BF16)
            k_ref[0, rows, :] = rot(p[:, k_lo:v_lo]).astype(BF16)
        else:
            q_ref[0, rows, :] = p[:, :k_lo].astype(BF16)
            k_ref[0, rows, :] = p[:, k_lo:v_lo].astype(BF16)
        vt = p[:, v_lo:u_lo].T.astype(BF16)
        ones = jnp.ones((VT_HEAD_ROWS - HEAD_DIM, per), BF16)
        for h_kv in range(N_KV_HEADS):
            v_ref[0, h_kv * VT_HEAD_ROWS:h_kv * VT_HEAD_ROWS + HEAD_DIM, rows] = (
                vt[h_kv * HEAD_DIM:(h_kv + 1) * HEAD_DIM])
            v_ref[0, h_kv * VT_HEAD_ROWS + HEAD_DIM:(h_kv + 1) * VT_HEAD_ROWS, rows] = ones
        if rope:
            n1 = slice(rows.start // DFT_C, rows.stop // DFT_C)
            for c in range(u_scratch.shape[0]):
                u_scratch[c, rows, :] = p[:, u_lo + c * LANES:u_lo + (c + 1) * LANES]
                for n2 in range(DFT_C):
                    u_ref[0, n2, n1, c * LANES:(c + 1) * LANES] = (
                        u_scratch[c, pl.ds(rows.start + n2, per // DFT_C, stride=DFT_C), :].astype(BF16))
        else:
            u_ref[0, rows, :] = p[:, u_lo:].astype(BF16)

    slices = [slice(r, r + per) for r in range(0, tm, per)]
    h = norm(slices[0])
    pending = None
    for i, rows in enumerate(slices):
        p = jnp.dot(h, w_ref[...], preferred_element_type=F32)
        if i + 1 < len(slices):
            h = norm(slices[i + 1])
        if pending is not None:
            emit(*pending)
        pending = (rows, p)
    emit(*pending)


def _project(layer, x, mod, gains, w_in, rope_tables, ctx_row=None):
    bx, t, d = x.shape
    tm = min(PROJ_TILE, t)
    n_in = w_in.shape[-1]
    rope = rope_tables is not None
    tok = lambda width: pl.BlockSpec((1, tm, width), lambda b, i: (b, i, 0))
    in_specs = [tok(d), _mod_spec(layer, 0, d), _mod_spec(layer, 1, d),
                _layer_resident(layer, (1, d)), _layer_resident(layer, (d, n_in))]
    args = [x, mod, mod, gains, w_in]
    if rope:
        in_specs += [pl.BlockSpec((tm, LANES), lambda b, i: (i, 0))] * 2
        args += list(rope_tables)
    widths = (ATTN_WIDTH, KV_WIDTH, KV_WIDTH, FOURIER_WIDTH)
    out_specs = [tok(w) for w in widths]
    out_shape = [jax.ShapeDtypeStruct((bx, t, w), BF16) for w in widths]
    out_specs[2] = pl.BlockSpec((1, VT_ROWS, tm), lambda b, i: (b, 0, i))
    out_shape[2] = jax.ShapeDtypeStruct((bx, VT_ROWS, t), BF16)
    scratch = []
    if rope:
        assert t == DFT_A * DFT_C and tm % DFT_C == 0
        out_specs[3] = pl.BlockSpec((1, DFT_C, tm // DFT_C, FOURIER_WIDTH), lambda b, i: (b, 0, i, 0))
        out_shape[3] = jax.ShapeDtypeStruct((bx, DFT_C, DFT_A, FOURIER_WIDTH), BF16)
        scratch = [pltpu.VMEM((FOURIER_WIDTH // LANES, tm, LANES), F32)]
    return pl.pallas_call(
        functools.partial(_proj_kernel, rope=rope, ctx_row=ctx_row),
        grid=(bx, t // tm),
        in_specs=in_specs,
        out_specs=out_specs,
        out_shape=out_shape,
        scratch_shapes=scratch,
        compiler_params=pltpu.CompilerParams(vmem_limit_bytes=VMEM_LIMIT_BYTES),
        name="project_rope" if rope else "project",
    )(*args)


def _attn_kernel(sink_ref, q_ref, *rest, local, n_cast):
    w = WINDOW
    n_in = len(rest) - 1 - 2 * n_cast
    cast_src = rest[n_in:n_in + n_cast]
    cast_dst = rest[n_in + n_cast + 1:]
    rest = rest[:n_in] + rest[n_in + n_cast:n_in + n_cast + 1]
    for src, dst in zip(cast_src, cast_dst):
        dst[...] = src[...].astype(BF16)
    if local:
        kp_ref, kc_ref, kn_ref, vp_ref, vc_ref, vn_ref, kx_ref, vx_ref, o_ref = rest
        k_all = jnp.concatenate([kp_ref[0], kc_ref[0], kn_ref[0]], axis=0)
        vt_all = jnp.concatenate([vp_ref[0], vc_ref[0], vn_ref[0]], axis=1)
        step = pl.program_id(1)
        last = pl.num_programs(1) - 1
    else:
        kx_ref, vx_ref, o_ref = rest
    kx = kx_ref[0]
    vxt = vx_ref[0]
    n_sub = q_ref.shape[1] // w
    cols = Q_PER_KV * w
    group = lax.shift_right_logical(lax.broadcasted_iota(jnp.int32, (1, cols), 1), int(math.log2(w)))
    if local:
        kj = lax.broadcasted_iota(jnp.int32, (w, cols), 0)
        qi = lax.broadcasted_iota(jnp.int32, (w, cols), 1) & (w - 1)
        far = 1 << 20
        bias_prev = jnp.where(kj >= qi, 0.0, NEG_INF)
        bias_next = jnp.where(kj <= qi, 0.0, NEG_INF)
        bias_prev_first = jnp.where(kj >= qi + jnp.where(step > 0, 0, far), 0.0, NEG_INF)
        bias_next_last = jnp.where(kj <= qi - jnp.where(step < last, 0, far), 0.0, NEG_INF)
    contract_last = (((1,), (1,)), ((), ()))

    def sink_row(h):
        sink = jnp.full((1, cols), sink_ref[Q_PER_KV * h] * LOG2_E, F32)
        for g in range(1, Q_PER_KV):
            sink = jnp.where(group == g, sink_ref[Q_PER_KV * h + g] * LOG2_E, sink)
        return sink

    sinks = [sink_row(h) for h in range(N_KV_HEADS)]

    def scores(h, s):
        hs = slice(h * HEAD_DIM, (h + 1) * HEAD_DIM)
        q = q_ref[0, s * w:(s + 1) * w, :]
        qs = jnp.concatenate([q[:, (Q_PER_KV * h + g) * HEAD_DIM:(Q_PER_KV * h + g + 1) * HEAD_DIM]
                              for g in range(Q_PER_KV)], axis=0)
        pieces = []
        if local:
            sl = lax.dot_general(k_all[s * w:(s + 3) * w, hs], qs, contract_last, preferred_element_type=F32)
            pieces += [sl[:w] + (bias_prev_first if s == 0 else bias_prev), sl[w:2 * w],
                       sl[2 * w:] + (bias_next_last if s == n_sub - 1 else bias_next)]
        pieces.append(lax.dot_general(kx[:, hs], qs, contract_last, preferred_element_type=F32))
        m = sinks[h]
        for x in pieces:
            m = jnp.maximum(m, jnp.max(x, axis=0, keepdims=True))
        return pieces, m

    def finish(h, s, pieces, m):
        vs = slice(h * VT_HEAD_ROWS, (h + 1) * VT_HEAD_ROWS)
        probs = [jnp.exp2(x - m).astype(BF16) for x in pieces]
        ot = jnp.dot(vxt[vs, :], probs[-1], preferred_element_type=F32)
        if local:
            ot = ot + jnp.dot(vt_all[vs, s * w:(s + 3) * w], jnp.concatenate(probs[:3], axis=0),
                              preferred_element_type=F32)
        denom = ot[HEAD_DIM:HEAD_DIM + 1] + jnp.exp2(sinks[h] - m)
        ot = ot[:HEAD_DIM] / denom
        o = jnp.concatenate([ot[:, g * w:(g + 1) * w].T for g in range(Q_PER_KV)], axis=1)
        lo = h * Q_PER_KV * HEAD_DIM
        o_ref[0, s * w:(s + 1) * w, lo:lo + Q_PER_KV * HEAD_DIM] = o.astype(BF16)

    work = [(h, s) for h in range(N_KV_HEADS) for s in range(n_sub)]
    queue = [scores(*item) for item in work[:ATTN_LOOKAHEAD]]
    for i, (h, s) in enumerate(work):
        if i + ATTN_LOOKAHEAD < len(work):
            queue.append(scores(*work[i + ATTN_LOOKAHEAD]))
        finish(h, s, *queue.pop(0))


def _attention(q, k, v, kx, vx, sink, cx, local, casts=()):
    smem = pl.BlockSpec(memory_space=pltpu.SMEM)
    kx_spec = pl.BlockSpec((1, cx, KV_WIDTH), lambda b, n: (0, b, 0))
    vx_spec = pl.BlockSpec((1, VT_ROWS, cx), lambda b, n: (0, 0, b))
    if local:
        bx, t, _ = q.shape
        tq = min(ATTN_Q_TILE, t)
        q_spec = pl.BlockSpec((1, tq, ATTN_WIDTH), lambda b, n: (b, n, 0))
    else:
        bx, t, tq = q.shape[1] // cx, cx, cx
        q_spec = pl.BlockSpec((1, tq, ATTN_WIDTH), lambda b, n: (0, b, 0))
    nb = t // tq
    if local:
        per = tq // WINDOW
        n_win = t // WINDOW
        before = lambda n: jnp.maximum(n * per - 1, 0)
        after = lambda n: jnp.minimum((n + 1) * per, n_win - 1)
        k_specs = [pl.BlockSpec((1, WINDOW, KV_WIDTH), lambda b, n: (b, before(n), 0)),
                   pl.BlockSpec((1, tq, KV_WIDTH), lambda b, n: (b, n, 0)),
                   pl.BlockSpec((1, WINDOW, KV_WIDTH), lambda b, n: (b, after(n), 0))]
        v_specs = [pl.BlockSpec((1, VT_ROWS, WINDOW), lambda b, n: (b, 0, before(n))),
                   pl.BlockSpec((1, VT_ROWS, tq), lambda b, n: (b, 0, n)),
                   pl.BlockSpec((1, VT_ROWS, WINDOW), lambda b, n: (b, 0, after(n)))]
        in_specs = [smem, q_spec] + k_specs + v_specs + [kx_spec, vx_spec]
        args = [sink, q, k, k, k, v, v, v, kx, vx]
    else:
        in_specs = [smem, q_spec, kx_spec, vx_spec]
        args = [sink, q, kx, vx]
    out_specs = [q_spec]
    out_shape = [jax.ShapeDtypeStruct(q.shape, BF16)]
    steps = bx * nb
    for w, layer, rows in casts:
        slab = rows // steps
        assert rows % steps == 0 and slab % BF16_SUBLANES == 0
        in_specs.append(pl.BlockSpec((None, slab, w.shape[-1]), lambda b, n, layer=layer: (layer, b * nb + n, 0)))
        out_specs.append(pl.BlockSpec((slab, w.shape[-1]), lambda b, n: (b * nb + n, 0)))
        out_shape.append(jax.ShapeDtypeStruct((rows, w.shape[-1]), BF16))
        args.append(w)
    outs = pl.pallas_call(
        functools.partial(_attn_kernel, local=local, n_cast=len(casts)),
        grid=(bx, nb),
        in_specs=in_specs,
        out_specs=out_specs,
        out_shape=out_shape,
        compiler_params=pltpu.CompilerParams(vmem_limit_bytes=VMEM_LIMIT_BYTES),
        name="attn_window" if local else "attn_context",
    )(*args)
    return outs[0], outs[1:]


def _dft_short_kernel(fa_ref, u_ref, ur_ref, ui_ref):
    y = jnp.dot(fa_ref[...].astype(BF16), u_ref[0], preferred_element_type=F32)
    ur_ref[0] = y[:DFT_A].astype(BF16)
    ui_ref[0] = y[DFT_A:].astype(BF16)


def _dft_long_kernel(fa_ref, g_ref, tr_ref, ti_ref, u_ref, ur_ref, ui_ref, yr_s, yi_s):
    kb = DFT_CK
    fa = fa_ref[...].astype(BF16)
    for a in range(DFT_C // 2):
        ys = [jnp.dot(fa, u_ref[0, 2 * a + e], preferred_element_type=F32) for e in range(2)]
        for c in range(ys[0].shape[1] // LANES):
            sl = slice(c * LANES, (c + 1) * LANES)
            re_im = []
            for e in range(2):
                tr, ti = tr_ref[2 * a + e], ti_ref[2 * a + e]
                yr, yi = ys[e][:DFT_A, sl], ys[e][DFT_A:, sl]
                re_im.append(((yr * tr - yi * ti).reshape(DFT_A // kb, kb, LANES),
                              (yr * ti + yi * tr).reshape(DFT_A // kb, kb, LANES)))
            yr_s[:, a, :, sl] = jnp.concatenate([re_im[0][0], re_im[1][0]], axis=1).astype(BF16)
            yi_s[:, a, :, sl] = jnp.concatenate([re_im[0][1], re_im[1][1]], axis=1).astype(BF16)
    g = g_ref[...].astype(BF16)
    half = DFT_C * kb
    w = ur_ref.shape[-1]
    for j in range(DFT_A // (2 * kb)):
        rs = []
        for e in range(2):
            y = jnp.concatenate([yr_s[2 * j + e].reshape(half, w), yi_s[2 * j + e].reshape(half, w)], axis=0)
            rs.append(jnp.dot(g, y, preferred_element_type=F32))
        sl = slice(j * 2 * kb, (j + 1) * 2 * kb)
        for ref, lo in ((ur_ref, 0), (ui_ref, half)):
            ref[0, :, sl, :] = jnp.concatenate(
                [r[lo:lo + half].reshape(DFT_C, kb, w) for r in rs], axis=1).astype(BF16)


def _dft_tables(n_pos):
    a = np.arange(DFT_A)
    ang = 2.0 * np.pi * ((a[:, None] * a[None, :]) % DFT_A) / DFT_A
    norm = 1.0 / math.sqrt(n_pos * FOURIER_GROUP_DIM)
    fa = jnp.asarray(np.concatenate([np.cos(ang), -np.sin(ang)], axis=0) * norm, F32)
    if n_pos == DFT_A:
        return fa, None, None, None
    assert n_pos == DFT_A * DFT_C
    n2 = np.arange(DFT_C)
    tw = 2.0 * np.pi * (n2[:, None] * a[None, :]) / n_pos
    tr = jnp.asarray(np.repeat(np.cos(tw)[:, :, None], LANES, axis=2), F32)
    ti = jnp.asarray(np.repeat(-np.sin(tw)[:, :, None], LANES, axis=2), F32)
    ang16 = 2.0 * np.pi * ((n2[:, None] * n2[None, :]) % DFT_C) / DFT_C
    c16, s16 = np.cos(ang16), np.sin(ang16)
    eye = np.eye(DFT_CK)
    kron = lambda f: np.einsum("kn,ab->kanb", f, eye).reshape(DFT_C * DFT_CK, DFT_C * DFT_CK)
    gmat = np.block([[kron(c16), kron(s16)], [kron(-s16), kron(c16)]])
    return fa, tr, ti, jnp.asarray(gmat, F32)


def _position_dft(u):
    bx, w = u.shape[0], u.shape[-1]
    n_pos = DFT_A if u.ndim == 3 else math.prod(u.shape[1:-1])
    fa, tr, ti, gmat = _dft_tables(n_pos)
    fa_spec = _resident((2 * DFT_A, DFT_A))
    if u.ndim == 3:
        blk = pl.BlockSpec((1, DFT_A, w), lambda b: (0, b, 0))
        out = jax.ShapeDtypeStruct(u.shape, BF16)
        return pl.pallas_call(
            _dft_short_kernel,
            grid=(u.shape[1] // DFT_A,), in_specs=[fa_spec, blk], out_specs=[blk, blk], out_shape=[out, out],
            name="dft_context",
        )(fa, u)
    blk = pl.BlockSpec((1, DFT_C, DFT_A, w), lambda b: (b, 0, 0, 0))
    out = jax.ShapeDtypeStruct((bx, DFT_C, DFT_A, w), BF16)
    ur, ui = pl.pallas_call(
        _dft_long_kernel,
        grid=(bx,),
        in_specs=[fa_spec, _resident(gmat.shape), _resident(tr.shape), _resident(ti.shape), blk],
        out_specs=[blk, blk], out_shape=[out, out],
        scratch_shapes=[pltpu.VMEM((DFT_A // DFT_CK, DFT_C // 2, 2 * DFT_CK, w), BF16)] * 2,
        compiler_params=pltpu.CompilerParams(vmem_limit_bytes=VMEM_LIMIT_BYTES),
        name="dft_long",
    )(fa, gmat, tr, ti, u)
    return ur.reshape(bx, n_pos, w), ui.reshape(bx, n_pos, w)


def _tail_kernel(a_ref, ur_ref, ui_ref, x_ref, gate1_ref, sh_ref, sc_ref, gate2_ref,
                 gpost_mix_ref, gpre_ref, gpost_ref, wa_ref, wr_ref, wi_ref, wg_ref, wu_ref, wd_ref, o_ref,
                 *, ctx_row):
    gate1, shift, scale, gate2 = (_mod_row(r, ctx_row) for r in (gate1_ref, sh_ref, sc_ref, gate2_ref))
    gain_mix = gate1 * gpost_mix_ref[...]
    gain_in = (1.0 + scale) * gpre_ref[...]
    gain_out = gate2 * gpost_ref[...]
    hidden = wg_ref.shape[1]
    tm = x_ref.shape[1]
    per = min(TAIL_SLICE_ROWS, tm)
    assert tm % per == 0

    def mix_dots(rows):
        mix = jnp.dot(a_ref[0, rows, :], wa_ref[...], preferred_element_type=F32)
        mix += jnp.dot(ur_ref[0, rows, :], wr_ref[...], preferred_element_type=F32)
        mix += jnp.dot(ui_ref[0, rows, :], wi_ref[...], preferred_element_type=F32)
        return mix

    def norms(rows, mix):
        x_mid = x_ref[0, rows, :] + _rms(mix, gain_mix)
        return x_mid, (_rms(x_mid, gain_in) + shift).astype(BF16)

    def ffn(h):
        acc = jnp.zeros((h.shape[0], o_ref.shape[2]), F32)
        for lo in range(0, hidden, FFN_CHUNK):
            hi = min(lo + FFN_CHUNK, hidden)
            g = jnp.dot(h, wg_ref[:, lo:hi], preferred_element_type=F32)
            u = jnp.dot(h, wu_ref[:, lo:hi], preferred_element_type=F32)
            a = (g * jax.nn.sigmoid(g) * u).astype(BF16)
            acc += jnp.dot(a, wd_ref[lo:hi, :], preferred_element_type=F32)
        return acc

    def epilogue(rows, x_mid, acc):
        o_ref[0, rows, :] = x_mid + _rms(acc, gain_out)

    slices = [slice(r, r + per) for r in range(0, tm, per)]
    mixes = [mix_dots(rows) for rows in slices]
    staged = norms(slices[0], mixes[0])
    pending = None
    for i, rows in enumerate(slices):
        x_mid, h = staged
        acc = ffn(h)
        if i + 1 < len(slices):
            staged = norms(slices[i + 1], mixes[i + 1])
        if pending is not None:
            epilogue(*pending)
        pending = (rows, x_mid, acc)
    epilogue(*pending)


def _tail(layer, attn, ur, ui, x, mod, g_post_mix, g_pre_ffn, g_post_ffn, w_out, w_four_r, w_four_i,
          wg, wu, wd, ctx_row=None):
    bx, t, d = x.shape
    tm = min(TAIL_TILE, max(t // 2, TAIL_SLICE_ROWS) if bx == 1 else t)
    hidden = wg.shape[-1]
    tok = lambda width: pl.BlockSpec((1, tm, width), lambda b, i: (b, i, 0))
    gain = _layer_resident(layer, (1, d))
    wfour = _layer_resident(layer, (FOURIER_WIDTH, d))
    return pl.pallas_call(
        functools.partial(_tail_kernel, ctx_row=ctx_row),
        grid=(bx, t // tm),
        in_specs=[tok(ATTN_WIDTH), tok(FOURIER_WIDTH), tok(FOURIER_WIDTH), tok(d),
                  _mod_spec(layer, 2, d), _mod_spec(layer, 3, d), _mod_spec(layer, 4, d), _mod_spec(layer, 5, d),
                  gain, gain, gain,
                  _resident((ATTN_WIDTH, d)), wfour, wfour,
                  _resident((d, hidden)), _resident((d, hidden)), _resident((hidden, d))],
        out_specs=tok(d),
        out_shape=jax.ShapeDtypeStruct((bx, t, d), F32),
        compiler_params=pltpu.CompilerParams(vmem_limit_bytes=VMEM_LIMIT_BYTES),
        name="tail",
    )(attn, ur, ui, x, mod, mod, mod, mod, g_post_mix, g_pre_ffn, g_post_ffn,
      w_out, w_four_r, w_four_i, wg, wu, wd)


def _rope_tables(seq):
    t = np.arange(seq)
    freqs = ROPE_THETA ** (-np.arange(ROPE_PAIRS_PER_AXIS, dtype=np.float64) / ROPE_PAIRS_PER_AXIS)
    ang = np.concatenate([(t // GRID_W)[:, None] * freqs, (t % GRID_W)[:, None] * freqs], axis=-1)
    cos = np.repeat(np.cos(ang), 2, axis=-1)
    sin = np.repeat(np.sin(ang), 2, axis=-1) * np.tile([-1.0, 1.0], HEAD_DIM // 2)
    heads_per_row = LANES // HEAD_DIM
    return (jnp.asarray(np.tile(cos, (1, heads_per_row)), F32), jnp.asarray(np.tile(sin, (1, heads_per_row)), F32))


def kernel(x, c, ctx, c_ctx, w_ada, b_ada, norm_pre_mix, norm_post_mix, norm_pre_ffn, norm_post_ffn,
           w_in, w_out, w_fourier, sink, w_gate, w_up, w_down):
    batch, seq, d = x.shape
    depth = w_ada.shape[0]
    assert batch + 1 <= MOD_ROWS

    cc = jnp.zeros((MOD_ROWS, d), F32).at[:batch].set(c).at[batch].set(c_ctx)
    mod = _ada(cc, w_ada, b_ada)
    w_four_r, w_four_i = _fold_fourier_weights(w_fourier, w_out)
    rope = _rope_tables(seq)

    q_scale = np.ones((w_in.shape[-1],), np.float32)
    q_scale[:ATTN_WIDTH] = HEAD_DIM ** -0.5 * LOG2_E
    w_in_b = (w_in * q_scale).astype(BF16)
    gains = [g.reshape(depth, 1, d) for g in (norm_pre_mix, norm_post_mix, norm_pre_ffn, norm_post_ffn)]
    g_pre_mix, g_post_mix, g_pre_ffn, g_post_ffn = gains

    ctx_len = ctx.shape[1]
    assert ctx_len == DFT_A
    xc = ctx.reshape(1, batch * ctx_len, d)
    for i in range(depth):
        q, k, v, u = _project(i, x, mod, g_pre_mix, w_in_b, rope)
        qc, kc, vc, uc = _project(i, xc, mod, g_pre_mix, w_in_b, None, ctx_row=batch)
        casts = [(w_out, i, ATTN_WIDTH), (w_gate, i, d), (w_up, i, d), (w_down, i, w_down.shape[1])]
        attn, tail_weights = _attention(q, k, v, kc, vc, sink[i], ctx_len, local=True, casts=casts)
        w_attn_out, wg, wu, wd = tail_weights

        def finish(stream, attn, ur, ui, ctx_row):
            return _tail(i, attn, ur, ui, stream, mod, g_post_mix, g_pre_ffn, g_post_ffn,
                         w_attn_out, w_four_r, w_four_i, wg, wu, wd, ctx_row)

        ur, ui = _position_dft(u)
        x = finish(x, attn, ur, ui, None)

        if i < depth - 1:
            attn_c, _ = _attention(qc, None, None, kc, vc, sink[i], ctx_len, local=False)
            urc, uic = _position_dft(uc)
            xc = finish(xc, attn_c, urc, uic, batch)
    return x
```

```python
import functools
import math

import numpy as np
import jax
import jax.numpy as jnp
from jax import lax
from jax.experimental import pallas as pl
from jax.experimental.pallas import tpu as pltpu

F32 = jnp.float32
BF16 = jnp.bfloat16

GRID_W = 64
HEAD_DIM = 64
N_Q_HEADS = 8
N_KV_HEADS = 2
Q_PER_KV = N_Q_HEADS // N_KV_HEADS
ATTN_WIDTH = N_Q_HEADS * HEAD_DIM
KV_WIDTH = N_KV_HEADS * HEAD_DIM
N_FOURIER_GROUPS = 8
FOURIER_GROUP_DIM = 64
FOURIER_WIDTH = N_FOURIER_GROUPS * FOURIER_GROUP_DIM
WINDOW = 128
ROPE_THETA = 10000.0
ROPE_PAIRS_PER_AXIS = HEAD_DIM // 4
RMS_EPS = 1e-6
N_MOD = 6
NEG_INF = -1e30
LOG2_E = math.log2(math.e)

LANES = 128
BF16_SUBLANES = 16
VMEM_LIMIT_BYTES = 60 * 1024 * 1024

DFT_A = 256
DFT_C = 16
DFT_CK = 8

MOD_ROWS = 8
ADA_COLS = 1536
PROJ_TILE = 2048
PROJ_SLICE_ROWS = 1024
FFN_CHUNK = 512
TAIL_TILE = 1024
TAIL_SLICE_ROWS = 512
ATTN_Q_TILE = 2048
ATTN_LOOKAHEAD = 2
VT_HEAD_ROWS = HEAD_DIM + BF16_SUBLANES
VT_ROWS = N_KV_HEADS * VT_HEAD_ROWS


def _resident(shape):
    zeros = (0,) * len(shape)
    return pl.BlockSpec(shape, lambda *_: zeros, pipeline_mode=pl.Buffered(1))


def _layer_resident(layer, shape, index=None):
    index = (0,) * len(shape) if index is None else index
    return pl.BlockSpec((None,) + tuple(shape), lambda *_: (layer,) + tuple(index), pipeline_mode=pl.Buffered(1))


def _mod_spec(layer, which, d):
    return pl.BlockSpec((None, MOD_ROWS, d), lambda *_: (layer, 0, which))


def _mod_row(ref, ctx_row):
    row = pl.program_id(0) if ctx_row is None else ctx_row
    return ref[pl.ds(row, 1), :]


def _rms(xf, g):
    ms = jnp.mean(xf * xf, axis=-1, keepdims=True)
    return xf * lax.rsqrt(ms + RMS_EPS) * g


def _split_bf16(x):
    hi = x.astype(BF16)
    return hi, (x - hi.astype(F32)).astype(BF16)


def _dot3(a, b):
    (a_hi, a_lo), (b_hi, b_lo) = a, b
    dot = functools.partial(jnp.dot, preferred_element_type=F32)
    return dot(a_hi, b_hi) + dot(a_lo, b_hi) + dot(a_hi, b_lo)


def _ada_kernel(c_ref, w_ref, b_ref, o_ref):
    c = c_ref[...]
    a = c * jax.nn.sigmoid(c)
    a_hi, a_lo = _split_bf16(a)
    w_hi, w_lo = _split_bf16(w_ref[0])
    rows = a.shape[0]
    r = jnp.dot(jnp.concatenate([a_hi, a_lo], axis=0), w_hi, preferred_element_type=F32)
    o_ref[0] = r[:rows] + r[rows:] + jnp.dot(a_hi, w_lo, preferred_element_type=F32) + b_ref[0]


def _ada(cc, w_ada, b_ada):
    depth, d, n = w_ada.shape
    tn = ADA_COLS
    assert n % tn == 0
    return pl.pallas_call(
        _ada_kernel,
        grid=(depth, n // tn),
        in_specs=[
            pl.BlockSpec((MOD_ROWS, d), lambda i, j: (0, 0)),
            pl.BlockSpec((1, d, tn), lambda i, j: (i, 0, j)),
            pl.BlockSpec((1, 1, tn), lambda i, j: (i, 0, j)),
        ],
        out_specs=pl.BlockSpec((1, MOD_ROWS, tn), lambda i, j: (i, 0, j)),
        out_shape=jax.ShapeDtypeStruct((depth, MOD_ROWS, n), F32),
        compiler_params=pltpu.CompilerParams(vmem_limit_bytes=VMEM_LIMIT_BYTES),
        name="ada",
    )(cc, w_ada, b_ada.reshape(depth, 1, n))


def _fold_kernel(c64_ref, s64_ref, wf_ref, wo_ref, wr_ref, wi_ref):
    gd = c64_ref.shape[0]
    for g in range(wf_ref.shape[0]):
        rows = slice(g * gd, (g + 1) * gd)
        t = _split_bf16(_dot3(_split_bf16(wf_ref[g]), _split_bf16(wo_ref[rows, :])))
        wr_ref[rows, :] = _dot3(_split_bf16(c64_ref[...]), t).astype(BF16)
        wi_ref[rows, :] = _dot3(_split_bf16(s64_ref[...]), t).astype(BF16)


def _fold_fourier_weights(w_fourier, w_out):
    depth, groups, gd, _ = w_fourier.shape
    d = w_out.shape[-1]
    width = groups * gd
    assert w_out.shape[1] == ATTN_WIDTH + width and ATTN_WIDTH == width
    j = np.arange(gd)
    ang = 2.0 * np.pi * ((j[:, None] * j[None, :]) % gd) / gd
    c64 = jnp.asarray(np.cos(ang), F32)
    s64 = jnp.asarray(np.sin(ang), F32)
    out = jax.ShapeDtypeStruct((depth, width, d), BF16)
    return pl.pallas_call(
        _fold_kernel,
        grid=(depth,),
        in_specs=[
            pl.BlockSpec((gd, gd), lambda i: (0, 0)),
            pl.BlockSpec((gd, gd), lambda i: (0, 0)),
            pl.BlockSpec((None, groups, gd, gd), lambda i: (i, 0, 0, 0)),
            pl.BlockSpec((None, width, d), lambda i: (i, 1, 0)),
        ],
        out_specs=[pl.BlockSpec((None, width, d), lambda i: (i, 0, 0))] * 2,
        out_shape=[out, out],
        name="fold_fourier",
    )(c64, s64, w_fourier, w_out)


def _proj_kernel(x_ref, sh_ref, sc_ref, g_ref, w_ref, *rest, rope, ctx_row):
    if rope:
        cos_ref, sin_ref, q_ref, k_ref, v_ref, u_ref, u_scratch = rest
    else:
        q_ref, k_ref, v_ref, u_ref = rest
    gain = (1.0 + _mod_row(sc_ref, ctx_row)) * g_ref[...]
    shift = _mod_row(sh_ref, ctx_row)
    k_lo = ATTN_WIDTH
    v_lo = ATTN_WIDTH + KV_WIDTH
    u_lo = ATTN_WIDTH + 2 * KV_WIDTH
    tm = x_ref.shape[1]
    per = min(PROJ_SLICE_ROWS, tm)
    assert tm % per == 0 and per % DFT_C == 0
    if rope:
        even_lane = (lax.broadcasted_iota(jnp.int32, (per, LANES), 1) & 1) == 0

    def norm(rows):
        return (_rms(x_ref[0, rows, :], gain) + shift).astype(BF16)

    def emit(rows, p):
        if rope:
            cos = cos_ref[rows, :]
            sin = sin_ref[rows, :]

            def rot(xs):
                partner = jnp.where(even_lane, pltpu.roll(xs, LANES - 1, 1), pltpu.roll(xs, 1, 1))
                return xs * cos + partner * sin

            for j in range(ATTN_WIDTH // LANES):
                q_ref[0, rows, j * LANES:(j + 1) * LANES] = rot(p[:, j * LANES:(j + 1) * LANES]).astype(BF16)
            k_ref[0, rows, :] = rot(p[:, k_lo:v_lo]).astype(BF16)
        else:
            q_ref[0, rows, :] = p[:, :k_lo].astype(BF16)
            k_ref[0, rows, :] = p[:, k_lo:v_lo].astype(BF16)
        vt = p[:, v_lo:u_lo].T.astype(BF16)
        ones = jnp.ones((VT_HEAD_ROWS - HEAD_DIM, per), BF16)
        for h_kv in range(N_KV_HEADS):
            v_ref[0, h_kv * VT_HEAD_ROWS:h_kv * VT_HEAD_ROWS + HEAD_DIM, rows] = (
                vt[h_kv * HEAD_DIM:(h_kv + 1) * HEAD_DIM])
            v_ref[0, h_kv * VT_HEAD_ROWS + HEAD_DIM:(h_kv + 1) * VT_HEAD_ROWS, rows] = ones
        if rope:
            n1 = slice(rows.start // DFT_C, rows.stop // DFT_C)
            for c in range(u_scratch.shape[0]):
                u_scratch[c, rows, :] = p[:, u_lo + c * LANES:u_lo + (c + 1) * LANES]
                for n2 in range(DFT_C):
                    u_ref[0, n2, n1, c * LANES:(c + 1) * LANES] = (
                        u_scratch[c, pl.ds(rows.start + n2, per // DFT_C, stride=DFT_C), :].astype(BF16))
        else:
            u_ref[0, rows, :] = p[:, u_lo:].astype(BF16)

    slices = [slice(r, r + per) for r in range(0, tm, per)]
    h = norm(slices[0])
    pending = None
    for i, rows in enumerate(slices):
        p = jnp.dot(h, w_ref[...], preferred_element_type=F32)
        if i + 1 < len(slices):
            h = norm(slices[i + 1])
        if pending is not None:
            emit(*pending)
        pending = (rows, p)
    emit(*pending)


def _project(layer, x, mod, gains, w_in, rope_tables, ctx_row=None):
    bx, t, d = x.shape
    tm = min(PROJ_TILE, t)
    n_in = w_in.shape[-1]
    rope = rope_tables is not None
    tok = lambda width: pl.BlockSpec((1, tm, width), lambda b, i: (b, i, 0))
    in_specs = [tok(d), _mod_spec(layer, 0, d), _mod_spec(layer, 1, d),
                _layer_resident(layer, (1, d)), _layer_resident(layer, (d, n_in))]
    args = [x, mod, mod, gains, w_in]
    if rope:
        in_specs += [pl.BlockSpec((tm, LANES), lambda b, i: (i, 0))] * 2
        args += list(rope_tables)
    widths = (ATTN_WIDTH, KV_WIDTH, KV_WIDTH, FOURIER_WIDTH)
    out_specs = [tok(w) for w in widths]
    out_shape = [jax.ShapeDtypeStruct((bx, t, w), BF16) for w in widths]
    out_specs[2] = pl.BlockSpec((1, VT_ROWS, tm), lambda b, i: (b, 0, i))
    out_shape[2] = jax.ShapeDtypeStruct((bx, VT_ROWS, t), BF16)
    scratch = []
    if rope:
        assert t == DFT_A * DFT_C and tm % DFT_C == 0
        out_specs[3] = pl.BlockSpec((1, DFT_C, tm // DFT_C, FOURIER_WIDTH), lambda b, i: (b, 0, i, 0))
        out_shape[3] = jax.ShapeDtypeStruct((bx, DFT_C, DFT_A, FOURIER_WIDTH), BF16)
        scratch = [pltpu.VMEM((FOURIER_WIDTH // LANES, tm, LANES), F32)]
    return pl.pallas_call(
        functools.partial(_proj_kernel, rope=rope, ctx_row=ctx_row),
        grid=(bx, t // tm),
        in_specs=in_specs,
        out_specs=out_specs,
        out_shape=out_shape,
        scratch_shapes=scratch,
        compiler_params=pltpu.CompilerParams(vmem_limit_bytes=VMEM_LIMIT_BYTES),
        name="project_rope" if rope else "project",
    )(*args)


def _attn_kernel(sink_ref, q_ref, *rest, local, n_cast):
    w = WINDOW
    n_in = len(rest) - 1 - 2 * n_cast
    cast_src = rest[n_in:n_in + n_cast]
    cast_dst = rest[n_in + n_cast + 1:]
    rest = rest[:n_in] + rest[n_in + n_cast:n_in + n_cast + 1]
    for src, dst in zip(cast_src, cast_dst):
        dst[...] = src[...].astype(BF16)
    if local:
        kp_ref, kc_ref, kn_ref, vp_ref, vc_ref, vn_ref, kx_ref, vx_ref, o_ref = rest
        k_all = jnp.concatenate([kp_ref[0], kc_ref[0], kn_ref[0]], axis=0)
        vt_all = jnp.concatenate([vp_ref[0], vc_ref[0], vn_ref[0]], axis=1)
        step = pl.program_id(1)
        last = pl.num_programs(1) - 1
    else:
        kx_ref, vx_ref, o_ref = rest
    kx = kx_ref[0]
    vxt = vx_ref[0]
    n_sub = q_ref.shape[1] // w
    cols = Q_PER_KV * w
    group = lax.shift_right_logical(lax.broadcasted_iota(jnp.int32, (1, cols), 1), int(math.log2(w)))
    if local:
        kj = lax.broadcasted_iota(jnp.int32, (w, cols), 0)
        qi = lax.broadcasted_iota(jnp.int32, (w, cols), 1) & (w - 1)
        far = 1 << 20
        bias_prev = jnp.where(kj >= qi, 0.0, NEG_INF)
        bias_next = jnp.where(kj <= qi, 0.0, NEG_INF)
        bias_prev_first = jnp.where(kj >= qi + jnp.where(step > 0, 0, far), 0.0, NEG_INF)
        bias_next_last = jnp.where(kj <= qi - jnp.where(step < last, 0, far), 0.0, NEG_INF)
    contract_last = (((1,), (1,)), ((), ()))

    def sink_row(h):
        sink = jnp.full((1, cols), sink_ref[Q_PER_KV * h] * LOG2_E, F32)
        for g in range(1, Q_PER_KV):
            sink = jnp.where(group == g, sink_ref[Q_PER_KV * h + g] * LOG2_E, sink)
        return sink

    sinks = [sink_row(h) for h in range(N_KV_HEADS)]

    def scores(h, s):
        hs = slice(h * HEAD_DIM, (h + 1) * HEAD_DIM)
        q = q_ref[0, s * w:(s + 1) * w, :]
        qs = jnp.concatenate([q[:, (Q_PER_KV * h + g) * HEAD_DIM:(Q_PER_KV * h + g + 1) * HEAD_DIM]
                              for g in range(Q_PER_KV)], axis=0)
        pieces = []
        if local:
            sl = lax.dot_general(k_all[s * w:(s + 3) * w, hs], qs, contract_last, preferred_element_type=F32)
            pieces += [sl[:w] + (bias_prev_first if s == 0 else bias_prev), sl[w:2 * w],
                       sl[2 * w:] + (bias_next_last if s == n_sub - 1 else bias_next)]
        pieces.append(lax.dot_general(kx[:, hs], qs, contract_last, preferred_element_type=F32))
        m = sinks[h]
        for x in pieces:
            m = jnp.maximum(m, jnp.max(x, axis=0, keepdims=True))
        return pieces, m

    def finish(h, s, pieces, m):
        vs = slice(h * VT_HEAD_ROWS, (h + 1) * VT_HEAD_ROWS)
        probs = [jnp.exp2(x - m).astype(BF16) for x in pieces]
        ot = jnp.dot(vxt[vs, :], probs[-1], preferred_element_type=F32)
        if local:
            ot = ot + jnp.dot(vt_all[vs, s * w:(s + 3) * w], jnp.concatenate(probs[:3], axis=0),
                              preferred_element_type=F32)
        denom = ot[HEAD_DIM:HEAD_DIM + 1] + jnp.exp2(sinks[h] - m)
        ot = ot[:HEAD_DIM] / denom
        o = jnp.concatenate([ot[:, g * w:(g + 1) * w].T for g in range(Q_PER_KV)], axis=1)
        lo = h * Q_PER_KV * HEAD_DIM
        o_ref[0, s * w:(s + 1) * w, lo:lo + Q_PER_KV * HEAD_DIM] = o.astype(BF16)

    work = [(h, s) for h in range(N_KV_HEADS) for s in range(n_sub)]
    queue = [scores(*item) for item in work[:ATTN_LOOKAHEAD]]
    for i, (h, s) in enumerate(work):
        if i + ATTN_LOOKAHEAD < len(work):
            queue.append(scores(*work[i + ATTN_LOOKAHEAD]))
        finish(h, s, *queue.pop(0))


def _attention(q, k, v, kx, vx, sink, cx, local, casts=()):
    smem = pl.BlockSpec(memory_space=pltpu.SMEM)
    kx_spec = pl.BlockSpec((1, cx, KV_WIDTH), lambda b, n: (0, b, 0))
    vx_spec = pl.BlockSpec((1, VT_ROWS, cx), lambda b, n: (0, 0, b))
    if local:
        bx, t, _ = q.shape
        tq = min(ATTN_Q_TILE, t)
        q_spec = pl.BlockSpec((1, tq, ATTN_WIDTH), lambda b, n: (b, n, 0))
    else:
        bx, t, tq = q.shape[1] // cx, cx, cx
        q_spec = pl.BlockSpec((1, tq, ATTN_WIDTH), lambda b, n: (0, b, 0))
    nb = t // tq
    if local:
        per = tq // WINDOW
        n_win = t // WINDOW
        before = lambda n: jnp.maximum(n * per - 1, 0)
        after = lambda n: jnp.minimum((n + 1) * per, n_win - 1)
        k_specs = [pl.BlockSpec((1, WINDOW, KV_WIDTH), lambda b, n: (b, before(n), 0)),
                   pl.BlockSpec((1, tq, KV_WIDTH), lambda b, n: (b, n, 0)),
                   pl.BlockSpec((1, WINDOW, KV_WIDTH), lambda b, n: (b, after(n), 0))]
        v_specs = [pl.BlockSpec((1, VT_ROWS, WINDOW), lambda b, n: (b, 0, before(n))),
                   pl.BlockSpec((1, VT_ROWS, tq), lambda b, n: (b, 0, n)),
                   pl.BlockSpec((1, VT_ROWS, WINDOW), lambda b, n: (b, 0, after(n)))]
        in_specs = [smem, q_spec] + k_specs + v_specs + [kx_spec, vx_spec]
        args = [sink, q, k, k, k, v, v, v, kx, vx]
    else:
        in_specs = [smem, q_spec, kx_spec, vx_spec]
        args = [sink, q, kx, vx]
    out_specs = [q_spec]
    out_shape = [jax.ShapeDtypeStruct(q.shape, BF16)]
    steps = bx * nb
    for w, layer, rows in casts:
        slab = rows // steps
        assert rows % steps == 0 and slab % BF16_SUBLANES == 0
        in_specs.append(pl.BlockSpec((None, slab, w.shape[-1]), lambda b, n, layer=layer: (layer, b * nb + n, 0)))
        out_specs.append(pl.BlockSpec((slab, w.shape[-1]), lambda b, n: (b * nb + n, 0)))
        out_shape.append(jax.ShapeDtypeStruct((rows, w.shape[-1]), BF16))
        args.append(w)
    outs = pl.pallas_call(
        functools.partial(_attn_kernel, local=local, n_cast=len(casts)),
        grid=(bx, nb),
        in_specs=in_specs,
        out_specs=out_specs,
        out_shape=out_shape,
        compiler_params=pltpu.CompilerParams(vmem_limit_bytes=VMEM_LIMIT_BYTES),
        name="attn_window" if local else "attn_context",
    )(*args)
    return outs[0], outs[1:]


def _dft_short_kernel(fa_ref, u_ref, ur_ref, ui_ref):
    y = jnp.dot(fa_ref[...].astype(BF16), u_ref[0], preferred_element_type=F32)
    ur_ref[0] = y[:DFT_A].astype(BF16)
    ui_ref[0] = y[DFT_A:].astype(BF16)


def _dft_long_kernel(fa_ref, g_ref, tr_ref, ti_ref, u_ref, ur_ref, ui_ref, yr_s, yi_s):
    kb = DFT_CK
    fa = fa_ref[...].astype(BF16)
    for a in range(DFT_C // 2):
        ys = [jnp.dot(fa, u_ref[0, 2 * a + e], preferred_element_type=F32) for e in range(2)]
        for c in range(ys[0].shape[1] // LANES):
            sl = slice(c * LANES, (c + 1) * LANES)
            re_im = []
            for e in range(2):
                tr, ti = tr_ref[2 * a + e], ti_ref[2 * a + e]
                yr, yi = ys[e][:DFT_A, sl], ys[e][DFT_A:, sl]
                re_im.append(((yr * tr - yi * ti).reshape(DFT_A // kb, kb, LANES),
                              (yr * ti + yi * tr).reshape(DFT_A // kb, kb, LANES)))
            yr_s[:, a, :, sl] = jnp.concatenate([re_im[0][0], re_im[1][0]], axis=1).astype(BF16)
            yi_s[:, a, :, sl] = jnp.concatenate([re_im[0][1], re_im[1][1]], axis=1).astype(BF16)
    g = g_ref[...].astype(BF16)
    half = DFT_C * kb
    w = ur_ref.shape[-1]
    for j in range(DFT_A // (2 * kb)):
        rs = []
        for e in range(2):
            y = jnp.concatenate([yr_s[2 * j + e].reshape(half, w), yi_s[2 * j + e].reshape(half, w)], axis=0)
            rs.append(jnp.dot(g, y, preferred_element_type=F32))
        sl = slice(j * 2 * kb, (j + 1) * 2 * kb)
        for ref, lo in ((ur_ref, 0), (ui_ref, half)):
            ref[0, :, sl, :] = jnp.concatenate(
                [r[lo:lo + half].reshape(DFT_C, kb, w) for r in rs], axis=1).astype(BF16)


def _dft_tables(n_pos):
    a = np.arange(DFT_A)
    ang = 2.0 * np.pi * ((a[:, None] * a[None, :]) % DFT_A) / DFT_A
    norm = 1.0 / math.sqrt(n_pos * FOURIER_GROUP_DIM)
    fa = jnp.asarray(np.concatenate([np.cos(ang), -np.sin(ang)], axis=0) * norm, F32)
    if n_pos == DFT_A:
        return fa, None, None, None
    assert n_pos == DFT_A * DFT_C
    n2 = np.arange(DFT_C)
    tw = 2.0 * np.pi * (n2[:, None] * a[None, :]) / n_pos
    tr = jnp.asarray(np.repeat(np.cos(tw)[:, :, None], LANES, axis=2), F32)
    ti = jnp.asarray(np.repeat(-np.sin(tw)[:, :, None], LANES, axis=2), F32)
    ang16 = 2.0 * np.pi * ((n2[:, None] * n2[None, :]) % DFT_C) / DFT_C
    c16, s16 = np.cos(ang16), np.sin(ang16)
    eye = np.eye(DFT_CK)
    kron = lambda f: np.einsum("kn,ab->kanb", f, eye).reshape(DFT_C * DFT_CK, DFT_C * DFT_CK)
    gmat = np.block([[kron(c16), kron(s16)], [kron(-s16), kron(c16)]])
    return fa, tr, ti, jnp.asarray(gmat, F32)


def _position_dft(u):
    bx, w = u.shape[0], u.shape[-1]
    n_pos = DFT_A if u.ndim == 3 else math.prod(u.shape[1:-1])
    fa, tr, ti, gmat = _dft_tables(n_pos)
    fa_spec = _resident((2 * DFT_A, DFT_A))
    if u.ndim == 3:
        blk = pl.BlockSpec((1, DFT_A, w), lambda b: (0, b, 0))
        out = jax.ShapeDtypeStruct(u.shape, BF16)
        return pl.pallas_call(
            _dft_short_kernel,
            grid=(u.shape[1] // DFT_A,), in_specs=[fa_spec, blk], out_specs=[blk, blk], out_shape=[out, out],
            name="dft_context",
        )(fa, u)
    blk = pl.BlockSpec((1, DFT_C, DFT_A, w), lambda b: (b, 0, 0, 0))
    out = jax.ShapeDtypeStruct((bx, DFT_C, DFT_A, w), BF16)
    ur, ui = pl.pallas_call(
        _dft_long_kernel,
        grid=(bx,),
        in_specs=[fa_spec, _resident(gmat.shape), _resident(tr.shape), _resident(ti.shape), blk],
        out_specs=[blk, blk], out_shape=[out, out],
        scratch_shapes=[pltpu.VMEM((DFT_A // DFT_CK, DFT_C // 2, 2 * DFT_CK, w), BF16)] * 2,
        compiler_params=pltpu.CompilerParams(vmem_limit_bytes=VMEM_LIMIT_BYTES),
        name="dft_long",
    )(fa, gmat, tr, ti, u)
    return ur.reshape(bx, n_pos, w), ui.reshape(bx, n_pos, w)


def _tail_kernel(a_ref, ur_ref, ui_ref, x_ref, gate1_ref, sh_ref, sc_ref, gate2_ref,
                 gpost_mix_ref, gpre_ref, gpost_ref, wa_ref, wr_ref, wi_ref, wg_ref, wu_ref, wd_ref, o_ref,
                 *, ctx_row):
    gate1, shift, scale, gate2 = (_mod_row(r, ctx_row) for r in (gate1_ref, sh_ref, sc_ref, gate2_ref))
    gain_mix = gate1 * gpost_mix_ref[...]
    gain_in = (1.0 + scale) * gpre_ref[...]
    gain_out = gate2 * gpost_ref[...]
    hidden = wg_ref.shape[1]
    tm = x_ref.shape[1]
    per = min(TAIL_SLICE_ROWS, tm)
    assert tm % per == 0

    def mix_dots(rows):
        mix = jnp.dot(a_ref[0, rows, :], wa_ref[...], preferred_element_type=F32)
        mix += jnp.dot(ur_ref[0, rows, :], wr_ref[...], preferred_element_type=F32)
        mix += jnp.dot(ui_ref[0, rows, :], wi_ref[...], preferred_element_type=F32)
        return mix

    def norms(rows, mix):
        x_mid = x_ref[0, rows, :] + _rms(mix, gain_mix)
        return x_mid, (_rms(x_mid, gain_in) + shift).astype(BF16)

    def ffn(h):
        acc = jnp.zeros((h.shape[0], o_ref.shape[2]), F32)
        for lo in range(0, hidden, FFN_CHUNK):
            hi = min(lo + FFN_CHUNK, hidden)
            g = jnp.dot(h, wg_ref[:, lo:hi], preferred_element_type=F32)
            u = jnp.dot(h, wu_ref[:, lo:hi], preferred_element_type=F32)
            a = (g * jax.nn.sigmoid(g) * u).astype(BF16)
            acc += jnp.dot(a, wd_ref[lo:hi, :], preferred_element_type=F32)
        return acc

    def epilogue(rows, x_mid, acc):
        o_ref[0, rows, :] = x_mid + _rms(acc, gain_out)

    slices = [slice(r, r + per) for r in range(0, tm, per)]
    mixes = [mix_dots(rows) for rows in slices]
    staged = norms(slices[0], mixes[0])
    pending = None
    for i, rows in enumerate(slices):
        x_mid, h = staged
        acc = ffn(h)
        if i + 1 < len(slices):
            staged = norms(slices[i + 1], mixes[i + 1])
        if pending is not None:
            epilogue(*pending)
        pending = (rows, x_mid, acc)
    epilogue(*pending)


def _tail(layer, attn, ur, ui, x, mod, g_post_mix, g_pre_ffn, g_post_ffn, w_out, w_four_r, w_four_i,
          wg, wu, wd, ctx_row=None):
    bx, t, d = x.shape
    tm = min(TAIL_TILE, max(t // 2, TAIL_SLICE_ROWS) if bx == 1 else t)
    hidden = wg.shape[-1]
    tok = lambda width: pl.BlockSpec((1, tm, width), lambda b, i: (b, i, 0))
    gain = _layer_resident(layer, (1, d))
    wfour = _layer_resident(layer, (FOURIER_WIDTH, d))
    return pl.pallas_call(
        functools.partial(_tail_kernel, ctx_row=ctx_row),
        grid=(bx, t // tm),
        in_specs=[tok(ATTN_WIDTH), tok(FOURIER_WIDTH), tok(FOURIER_WIDTH), tok(d),
                  _mod_spec(layer, 2, d), _mod_spec(layer, 3, d), _mod_spec(layer, 4, d), _mod_spec(layer, 5, d),
                  gain, gain, gain,
                  _resident((ATTN_WIDTH, d)), wfour, wfour,
                  _resident((d, hidden)), _resident((d, hidden)), _resident((hidden, d))],
        out_specs=tok(d),
        out_shape=jax.ShapeDtypeStruct((bx, t, d), F32),
        compiler_params=pltpu.CompilerParams(vmem_limit_bytes=VMEM_LIMIT_BYTES),
        name="tail",
    )(attn, ur, ui, x, mod, mod, mod, mod, g_post_mix, g_pre_ffn, g_post_ffn,
      w_out, w_four_r, w_four_i, wg, wu, wd)


def _rope_tables(seq):
    t = np.arange(seq)
    freqs = ROPE_THETA ** (-np.arange(ROPE_PAIRS_PER_AXIS, dtype=np.float64) / ROPE_PAIRS_PER_AXIS)
    ang = np.concatenate([(t // GRID_W)[:, None] * freqs, (t % GRID_W)[:, None] * freqs], axis=-1)
    cos = np.repeat(np.cos(ang), 2, axis=-1)
    sin = np.repeat(np.sin(ang), 2, axis=-1) * np.tile([-1.0, 1.0], HEAD_DIM // 2)
    heads_per_row = LANES // HEAD_DIM
    return (jnp.asarray(np.tile(cos, (1, heads_per_row)), F32), jnp.asarray(np.tile(sin, (1, heads_per_row)), F32))


def kernel(x, c, ctx, c_ctx, w_ada, b_ada, norm_pre_mix, norm_post_mix, norm_pre_ffn, norm_post_ffn,
           w_in, w_out, w_fourier, sink, w_gate, w_up, w_down):
    batch, seq, d = x.shape
    depth = w_ada.shape[0]
    assert batch + 1 <= MOD_ROWS

    cc = jnp.zeros((MOD_ROWS, d), F32).at[:batch].set(c).at[batch].set(c_ctx)
    mod = _ada(cc, w_ada, b_ada)
    w_four_r, w_four_i = _fold_fourier_weights(w_fourier, w_out)
    rope = _rope_tables(seq)

    q_scale = np.ones((w_in.shape[-1],), np.float32)
    q_scale[:ATTN_WIDTH] = HEAD_DIM ** -0.5 * LOG2_E
    w_in_b = (w_in * q_scale).astype(BF16)
    gains = [g.reshape(depth, 1, d) for g in (norm_pre_mix, norm_post_mix, norm_pre_ffn, norm_post_ffn)]
    g_pre_mix, g_post_mix, g_pre_ffn, g_post_ffn = gains

    ctx_len = ctx.shape[1]
    assert ctx_len == DFT_A
    xc = ctx.reshape(1, batch * ctx_len, d)
    for i in range(depth):
        q, k, v, u = _project(i, x, mod, g_pre_mix, w_in_b, rope)
        qc, kc, vc, uc = _project(i, xc, mod, g_pre_mix, w_in_b, None, ctx_row=batch)
        casts = [(w_out, i, ATTN_WIDTH), (w_gate, i, d), (w_up, i, d), (w_down, i, w_down.shape[1])]
        attn, tail_weights = _attention(q, k, v, kc, vc, sink[i], ctx_len, local=True, casts=casts)
        w_attn_out, wg, wu, wd = tail_weights

        def finish(stream, attn, ur, ui, ctx_row):
            return _tail(i, attn, ur, ui, stream, mod, g_post_mix, g_pre_ffn, g_post_ffn,
                         w_attn_out, w_four_r, w_four_i, wg, wu, wd, ctx_row)

        ur, ui = _position_dft(u)
        x = finish(x, attn, ur, ui, None)

        if i < depth - 1:
            attn_c, _ = _attention(qc, None, None, kc, vc, sink[i], ctx_len, local=False)
            urc, uic = _position_dft(uc)
            xc = finish(xc, attn_c, urc, uic, batch)
    return x
```

```python
import functools
import math

import numpy as np
import jax
import jax.numpy as jnp
from jax import lax
from jax.experimental import pallas as pl
from jax.experimental.pallas import tpu as pltpu

F32 = jnp.float32
BF16 = jnp.bfloat16

GRID_W = 64
HEAD_DIM = 64
N_Q_HEADS = 8
N_KV_HEADS = 2
Q_PER_KV = N_Q_HEADS // N_KV_HEADS
ATTN_WIDTH = N_Q_HEADS * HEAD_DIM
KV_WIDTH = N_KV_HEADS * HEAD_DIM
N_FOURIER_GROUPS = 8
FOURIER_GROUP_DIM = 64
FOURIER_WIDTH = N_FOURIER_GROUPS * FOURIER_GROUP_DIM
WINDOW = 128
ROPE_THETA = 10000.0
ROPE_PAIRS_PER_AXIS = HEAD_DIM // 4
RMS_EPS = 1e-6
N_MOD = 6
NEG_INF = -1e30
LOG2_E = math.log2(math.e)

LANES = 128
BF16_SUBLANES = 16
VMEM_LIMIT_BYTES = 60 * 1024 * 1024

DFT_A = 256
DFT_C = 16
DFT_CK = 8

MOD_ROWS = 8
ADA_COLS = 1536
PROJ_TILE = 1024
PROJ_SLICE_ROWS = 512
FFN_CHUNK = 512
TAIL_TILE = 1024
TAIL_SLICE_ROWS = 512
ATTN_Q_TILE = 2048
ATTN_LOOKAHEAD = 2
VT_HEAD_ROWS = HEAD_DIM + BF16_SUBLANES
VT_ROWS = N_KV_HEADS * VT_HEAD_ROWS


def _resident(shape):
    zeros = (0,) * len(shape)
    return pl.BlockSpec(shape, lambda *_: zeros, pipeline_mode=pl.Buffered(1))


def _layer_resident(layer, shape, index=None):
    index = (0,) * len(shape) if index is None else index
    return pl.BlockSpec((None,) + tuple(shape), lambda *_: (layer,) + tuple(index), pipeline_mode=pl.Buffered(1))


def _mod_spec(layer, which, d):
    return pl.BlockSpec((None, MOD_ROWS, d), lambda *_: (layer, 0, which))


def _mod_row(ref, ctx_row):
    row = pl.program_id(0) if ctx_row is None else ctx_row
    return ref[pl.ds(row, 1), :]


def _rms(xf, g):
    ms = jnp.mean(xf * xf, axis=-1, keepdims=True)
    return xf * lax.rsqrt(ms + RMS_EPS) * g


def _split_bf16(x):
    hi = x.astype(BF16)
    return hi, (x - hi.astype(F32)).astype(BF16)


def _dot3(a, b):
    (a_hi, a_lo), (b_hi, b_lo) = a, b
    dot = functools.partial(jnp.dot, preferred_element_type=F32)
    return dot(a_hi, b_hi) + dot(a_lo, b_hi) + dot(a_hi, b_lo)


def _ada_kernel(c_ref, w_ref, b_ref, o_ref):
    c = c_ref[...]
    a = c * jax.nn.sigmoid(c)
    a_hi, a_lo = _split_bf16(a)
    w_hi, w_lo = _split_bf16(w_ref[0])
    rows = a.shape[0]
    r = jnp.dot(jnp.concatenate([a_hi, a_lo], axis=0), w_hi, preferred_element_type=F32)
    o_ref[0] = r[:rows] + r[rows:] + jnp.dot(a_hi, w_lo, preferred_element_type=F32) + b_ref[0]


def _ada(cc, w_ada, b_ada):
    depth, d, n = w_ada.shape
    tn = ADA_COLS
    assert n % tn == 0
    return pl.pallas_call(
        _ada_kernel,
        grid=(depth, n // tn),
        in_specs=[
            pl.BlockSpec((MOD_ROWS, d), lambda i, j: (0, 0)),
            pl.BlockSpec((1, d, tn), lambda i, j: (i, 0, j)),
            pl.BlockSpec((1, 1, tn), lambda i, j: (i, 0, j)),
        ],
        out_specs=pl.BlockSpec((1, MOD_ROWS, tn), lambda i, j: (i, 0, j)),
        out_shape=jax.ShapeDtypeStruct((depth, MOD_ROWS, n), F32),
        compiler_params=pltpu.CompilerParams(vmem_limit_bytes=VMEM_LIMIT_BYTES),
        name="ada",
    )(cc, w_ada, b_ada.reshape(depth, 1, n))


def _fold_kernel(c64_ref, s64_ref, wf_ref, wo_ref, wr_ref, wi_ref):
    gd = c64_ref.shape[0]
    for g in range(wf_ref.shape[0]):
        rows = slice(g * gd, (g + 1) * gd)
        t = _split_bf16(_dot3(_split_bf16(wf_ref[g]), _split_bf16(wo_ref[rows, :])))
        wr_ref[rows, :] = _dot3(_split_bf16(c64_ref[...]), t).astype(BF16)
        wi_ref[rows, :] = _dot3(_split_bf16(s64_ref[...]), t).astype(BF16)


def _fold_fourier_weights(w_fourier, w_out):
    depth, groups, gd, _ = w_fourier.shape
    d = w_out.shape[-1]
    width = groups * gd
    assert w_out.shape[1] == ATTN_WIDTH + width and ATTN_WIDTH == width
    j = np.arange(gd)
    ang = 2.0 * np.pi * ((j[:, None] * j[None, :]) % gd) / gd
    c64 = jnp.asarray(np.cos(ang), F32)
    s64 = jnp.asarray(np.sin(ang), F32)
    out = jax.ShapeDtypeStruct((depth, width, d), BF16)
    return pl.pallas_call(
        _fold_kernel,
        grid=(depth,),
        in_specs=[
            pl.BlockSpec((gd, gd), lambda i: (0, 0)),
            pl.BlockSpec((gd, gd), lambda i: (0, 0)),
            pl.BlockSpec((None, groups, gd, gd), lambda i: (i, 0, 0, 0)),
            pl.BlockSpec((None, width, d), lambda i: (i, 1, 0)),
        ],
        out_specs=[pl.BlockSpec((None, width, d), lambda i: (i, 0, 0))] * 2,
        out_shape=[out, out],
        name="fold_fourier",
    )(c64, s64, w_fourier, w_out)


def _proj_kernel(x_ref, sh_ref, sc_ref, g_ref, w_ref, *rest, rope, ctx_row):
    if rope:
        cos_ref, sin_ref, q_ref, k_ref, v_ref, u_ref, u_scratch = rest
    else:
        q_ref, k_ref, v_ref, u_ref = rest
    gain = (1.0 + _mod_row(sc_ref, ctx_row)) * g_ref[...]
    shift = _mod_row(sh_ref, ctx_row)
    k_lo = ATTN_WIDTH
    v_lo = ATTN_WIDTH + KV_WIDTH
    u_lo = ATTN_WIDTH + 2 * KV_WIDTH
    tm = x_ref.shape[1]
    per = min(PROJ_SLICE_ROWS, tm)
    assert tm % per == 0 and per % DFT_C == 0
    if rope:
        even_lane = (lax.broadcasted_iota(jnp.int32, (per, LANES), 1) & 1) == 0

    def norm(rows):
        return (_rms(x_ref[0, rows, :], gain) + shift).astype(BF16)

    def emit(rows, p):
        if rope:
            cos = cos_ref[rows, :]
            sin = sin_ref[rows, :]

            def rot(xs):
                partner = jnp.where(even_lane, pltpu.roll(xs, LANES - 1, 1), pltpu.roll(xs, 1, 1))
                return xs * cos + partner * sin

            for j in range(ATTN_WIDTH // LANES):
                q_ref[0, rows, j * LANES:(j + 1) * LANES] = rot(p[:, j * LANES:(j + 1) * LANES]).astype(BF16)
            k_ref[0, rows, :] = rot(p[:, k_lo:v_lo]).astype(BF16)
        else:
            q_ref[0, rows, :] = p[:, :k_lo].astype(BF16)
            k_ref[0, rows, :] = p[:, k_lo:v_lo].astype(BF16)
        vt = p[:, v_lo:u_lo].T.astype(BF16)
        ones = jnp.ones((VT_HEAD_ROWS - HEAD_DIM, per), BF16)
        for h_kv in range(N_KV_HEADS):
            v_ref[0, h_kv * VT_HEAD_ROWS:h_kv * VT_HEAD_ROWS + HEAD_DIM, rows] = (
                vt[h_kv * HEAD_DIM:(h_kv + 1) * HEAD_DIM])
            v_ref[0, h_kv * VT_HEAD_ROWS + HEAD_DIM:(h_kv + 1) * VT_HEAD_ROWS, rows] = ones
        if rope:
            n1 = slice(rows.start // DFT_C, rows.stop // DFT_C)
            for c in range(u_scratch.shape[0]):
                u_scratch[c, rows, :] = p[:, u_lo + c * LANES:u_lo + (c + 1) * LANES]
                for n2 in range(DFT_C):
                    u_ref[0, n2, n1, c * LANES:(c + 1) * LANES] = (
                        u_scratch[c, pl.ds(rows.start + n2, per // DFT_C, stride=DFT_C), :].astype(BF16))
        else:
            u_ref[0, rows, :] = p[:, u_lo:].astype(BF16)

    slices = [slice(r, r + per) for r in range(0, tm, per)]
    h = norm(slices[0])
    pending = None
    for i, rows in enumerate(slices):
        p = jnp.dot(h, w_ref[...], preferred_element_type=F32)
        if i + 1 < len(slices):
            h = norm(slices[i + 1])
        if pending is not None:
            emit(*pending)
        pending = (rows, p)
    emit(*pending)


def _project(layer, x, mod, gains, w_in, rope_tables, ctx_row=None):
    bx, t, d = x.shape
    tm = min(PROJ_TILE, t)
    n_in = w_in.shape[-1]
    rope = rope_tables is not None
    tok = lambda width: pl.BlockSpec((1, tm, width), lambda b, i: (b, i, 0))
    in_specs = [tok(d), _mod_spec(layer, 0, d), _mod_spec(layer, 1, d),
                _layer_resident(layer, (1, d)), _layer_resident(layer, (d, n_in))]
    args = [x, mod, mod, gains, w_in]
    if rope:
        in_specs += [pl.BlockSpec((tm, LANES), lambda b, i: (i, 0))] * 2
        args += list(rope_tables)
    widths = (ATTN_WIDTH, KV_WIDTH, KV_WIDTH, FOURIER_WIDTH)
    out_specs = [tok(w) for w in widths]
    out_shape = [jax.ShapeDtypeStruct((bx, t, w), BF16) for w in widths]
    out_specs[2] = pl.BlockSpec((1, VT_ROWS, tm), lambda b, i: (b, 0, i))
    out_shape[2] = jax.ShapeDtypeStruct((bx, VT_ROWS, t), BF16)
    scratch = []
    if rope:
        assert t == DFT_A * DFT_C and tm % DFT_C == 0
        out_specs[3] = pl.BlockSpec((1, DFT_C, tm // DFT_C, FOURIER_WIDTH), lambda b, i: (b, 0, i, 0))
        out_shape[3] = jax.ShapeDtypeStruct((bx, DFT_C, DFT_A, FOURIER_WIDTH), BF16)
        scratch = [pltpu.VMEM((FOURIER_WIDTH // LANES, tm, LANES), F32)]
    return pl.pallas_call(
        functools.partial(_proj_kernel, rope=rope, ctx_row=ctx_row),
        grid=(bx, t // tm),
        in_specs=in_specs,
        out_specs=out_specs,
        out_shape=out_shape,
        scratch_shapes=scratch,
        compiler_params=pltpu.CompilerParams(vmem_limit_bytes=VMEM_LIMIT_BYTES),
        name="project_rope" if rope else "project",
    )(*args)


def _attn_kernel(sink_ref, q_ref, *rest, local, n_cast):
    w = WINDOW
    n_in = len(rest) - 1 - 2 * n_cast
    cast_src = rest[n_in:n_in + n_cast]
    cast_dst = rest[n_in + n_cast + 1:]
    rest = rest[:n_in] + rest[n_in + n_cast:n_in + n_cast + 1]
    for src, dst in zip(cast_src, cast_dst):
        dst[...] = src[...].astype(BF16)
    if local:
        kp_ref, kc_ref, kn_ref, vp_ref, vc_ref, vn_ref, kx_ref, vx_ref, o_ref = rest
        n_blocks = q_ref.shape[1] // w

        def k_window(s, hs):
            lo, hi = max(s - 1, 0) * w, min(s + 2, n_blocks) * w
            parts = ([kp_ref[0]] if s == 0 else []) + [kc_ref[0, lo:hi, :]]
            parts += [kn_ref[0]] if s == n_blocks - 1 else []
            return jnp.concatenate(parts, axis=0)[:, hs]

        def vt_window(s, vs):
            lo, hi = max(s - 1, 0) * w, min(s + 2, n_blocks) * w
            parts = ([vp_ref[0, vs, :]] if s == 0 else []) + [vc_ref[0, vs, lo:hi]]
            parts += [vn_ref[0, vs, :]] if s == n_blocks - 1 else []
            return jnp.concatenate(parts, axis=1)
        step = pl.program_id(1)
        last = pl.num_programs(1) - 1
    else:
        kx_ref, vx_ref, o_ref = rest
    kx = kx_ref[0]
    vxt = vx_ref[0]
    n_sub = q_ref.shape[1] // w
    cols = Q_PER_KV * w
    group = lax.shift_right_logical(lax.broadcasted_iota(jnp.int32, (1, cols), 1), int(math.log2(w)))
    if local:
        kj = lax.broadcasted_iota(jnp.int32, (w, cols), 0)
        qi = lax.broadcasted_iota(jnp.int32, (w, cols), 1) & (w - 1)
        far = 1 << 20
        bias_prev = jnp.where(kj >= qi, 0.0, NEG_INF)
        bias_next = jnp.where(kj <= qi, 0.0, NEG_INF)
        bias_prev_first = jnp.where(kj >= qi + jnp.where(step > 0, 0, far), 0.0, NEG_INF)
        bias_next_last = jnp.where(kj <= qi - jnp.where(step < last, 0, far), 0.0, NEG_INF)
    contract_last = (((1,), (1,)), ((), ()))

    def sink_row(h):
        sink = jnp.full((1, cols), sink_ref[Q_PER_KV * h] * LOG2_E, F32)
        for g in range(1, Q_PER_KV):
            sink = jnp.where(group == g, sink_ref[Q_PER_KV * h + g] * LOG2_E, sink)
        return sink

    sinks = [sink_row(h) for h in range(N_KV_HEADS)]

    def scores(h, s):
        hs = slice(h * HEAD_DIM, (h + 1) * HEAD_DIM)
        q = q_ref[0, s * w:(s + 1) * w, :]
        qs = jnp.concatenate([q[:, (Q_PER_KV * h + g) * HEAD_DIM:(Q_PER_KV * h + g + 1) * HEAD_DIM]
                              for g in range(Q_PER_KV)], axis=0)
        pieces = []
        if local:
            sl = lax.dot_general(k_window(s, hs), qs, contract_last, preferred_element_type=F32)
            pieces += [sl[:w] + (bias_prev_first if s == 0 else bias_prev), sl[w:2 * w],
                       sl[2 * w:] + (bias_next_last if s == n_sub - 1 else bias_next)]
        pieces.append(lax.dot_general(kx[:, hs], qs, contract_last, preferred_element_type=F32))
        m = sinks[h]
        for x in pieces:
            m = jnp.maximum(m, jnp.max(x, axis=0, keepdims=True))
        return pieces, m

    def finish(h, s, pieces, m):
        vs = slice(h * VT_HEAD_ROWS, (h + 1) * VT_HEAD_ROWS)
        probs = [jnp.exp2(x - m).astype(BF16) for x in pieces]
        ot = jnp.dot(vxt[vs, :], probs[-1], preferred_element_type=F32)
        if local:
            ot = ot + jnp.dot(vt_window(s, vs), jnp.concatenate(probs[:3], axis=0),
                              preferred_element_type=F32)
        denom = ot[HEAD_DIM:HEAD_DIM + 1] + jnp.exp2(sinks[h] - m)
        ot = ot[:HEAD_DIM] / denom
        o = jnp.concatenate([ot[:, g * w:(g + 1) * w].T for g in range(Q_PER_KV)], axis=1)
        lo = h * Q_PER_KV * HEAD_DIM
        o_ref[0, s * w:(s + 1) * w, lo:lo + Q_PER_KV * HEAD_DIM] = o.astype(BF16)

    work = [(h, s) for h in range(N_KV_HEADS) for s in range(n_sub)]
    queue = [scores(*item) for item in work[:ATTN_LOOKAHEAD]]
    for i, (h, s) in enumerate(work):
        if i + ATTN_LOOKAHEAD < len(work):
            queue.append(scores(*work[i + ATTN_LOOKAHEAD]))
        finish(h, s, *queue.pop(0))


def _attention(q, k, v, kx, vx, sink, cx, local, casts=()):
    smem = pl.BlockSpec(memory_space=pltpu.SMEM)
    kx_spec = pl.BlockSpec((1, cx, KV_WIDTH), lambda b, n: (0, b, 0))
    vx_spec = pl.BlockSpec((1, VT_ROWS, cx), lambda b, n: (0, 0, b))
    if local:
        bx, t, _ = q.shape
        tq = min(ATTN_Q_TILE, t)
        q_spec = pl.BlockSpec((1, tq, ATTN_WIDTH), lambda b, n: (b, n, 0))
    else:
        bx, t, tq = q.shape[1] // cx, cx, cx
        q_spec = pl.BlockSpec((1, tq, ATTN_WIDTH), lambda b, n: (0, b, 0))
    nb = t // tq
    if local:
        per = tq // WINDOW
        n_win = t // WINDOW
        before = lambda n: jnp.maximum(n * per - 1, 0)
        after = lambda n: jnp.minimum((n + 1) * per, n_win - 1)
        k_specs = [pl.BlockSpec((1, WINDOW, KV_WIDTH), lambda b, n: (b, before(n), 0)),
                   pl.BlockSpec((1, tq, KV_WIDTH), lambda b, n: (b, n, 0)),
                   pl.BlockSpec((1, WINDOW, KV_WIDTH), lambda b, n: (b, after(n), 0))]
        v_specs = [pl.BlockSpec((1, VT_ROWS, WINDOW), lambda b, n: (b, 0, before(n))),
                   pl.BlockSpec((1, VT_ROWS, tq), lambda b, n: (b, 0, n)),
                   pl.BlockSpec((1, VT_ROWS, WINDOW), lambda b, n: (b, 0, after(n)))]
        in_specs = [smem, q_spec] + k_specs + v_specs + [kx_spec, vx_spec]
        args = [sink, q, k, k, k, v, v, v, kx, vx]
    else:
        in_specs = [smem, q_spec, kx_spec, vx_spec]
        args = [sink, q, kx, vx]
    out_specs = [q_spec]
    out_shape = [jax.ShapeDtypeStruct(q.shape, BF16)]
    steps = bx * nb
    for w, layer, rows in casts:
        slab = rows // steps
        assert rows % steps == 0 and slab % BF16_SUBLANES == 0
        in_specs.append(pl.BlockSpec((None, slab, w.shape[-1]), lambda b, n, layer=layer: (layer, b * nb + n, 0)))
        out_specs.append(pl.BlockSpec((slab, w.shape[-1]), lambda b, n: (b * nb + n, 0)))
        out_shape.append(jax.ShapeDtypeStruct((rows, w.shape[-1]), BF16))
        args.append(w)
    outs = pl.pallas_call(
        functools.partial(_attn_kernel, local=local, n_cast=len(casts)),
        grid=(bx, nb),
        in_specs=in_specs,
        out_specs=out_specs,
        out_shape=out_shape,
        compiler_params=pltpu.CompilerParams(vmem_limit_bytes=VMEM_LIMIT_BYTES),
        name="attn_window" if local else "attn_context",
    )(*args)
    return outs[0], outs[1:]


def _dft_short_kernel(fa_ref, u_ref, ur_ref, ui_ref):
    y = jnp.dot(fa_ref[...].astype(BF16), u_ref[0], preferred_element_type=F32)
    ur_ref[0] = y[:DFT_A].astype(BF16)
    ui_ref[0] = y[DFT_A:].astype(BF16)


def _dft_long_kernel(fa_ref, g_ref, tr_ref, ti_ref, u_ref, ur_ref, ui_ref, yr_s, yi_s):
    kb = DFT_CK
    fa = fa_ref[...].astype(BF16)
    for a in range(DFT_C // 2):
        ys = [jnp.dot(fa, u_ref[0, 2 * a + e], preferred_element_type=F32) for e in range(2)]
        for c in range(ys[0].shape[1] // LANES):
            sl = slice(c * LANES, (c + 1) * LANES)
            re_im = []
            for e in range(2):
                tr, ti = tr_ref[2 * a + e], ti_ref[2 * a + e]
                yr, yi = ys[e][:DFT_A, sl], ys[e][DFT_A:, sl]
                re_im.append(((yr * tr - yi * ti).reshape(DFT_A // kb, kb, LANES),
                              (yr * ti + yi * tr).reshape(DFT_A // kb, kb, LANES)))
            yr_s[:, a, :, sl] = jnp.concatenate([re_im[0][0], re_im[1][0]], axis=1).astype(BF16)
            yi_s[:, a, :, sl] = jnp.concatenate([re_im[0][1], re_im[1][1]], axis=1).astype(BF16)
    g = g_ref[...].astype(BF16)
    half = DFT_C * kb
    w = ur_ref.shape[-1]
    for j in range(DFT_A // (2 * kb)):
        rs = []
        for e in range(2):
            y = jnp.concatenate([yr_s[2 * j + e].reshape(half, w), yi_s[2 * j + e].reshape(half, w)], axis=0)
            rs.append(jnp.dot(g, y, preferred_element_type=F32))
        sl = slice(j * 2 * kb, (j + 1) * 2 * kb)
        for ref, lo in ((ur_ref, 0), (ui_ref, half)):
            ref[0, :, sl, :] = jnp.concatenate(
                [r[lo:lo + half].reshape(DFT_C, kb, w) for r in rs], axis=1).astype(BF16)


def _dft_tables(n_pos):
    a = np.arange(DFT_A)
    ang = 2.0 * np.pi * ((a[:, None] * a[None, :]) % DFT_A) / DFT_A
    norm = 1.0 / math.sqrt(n_pos * FOURIER_GROUP_DIM)
    fa = jnp.asarray(np.concatenate([np.cos(ang), -np.sin(ang)], axis=0) * norm, F32)
    if n_pos == DFT_A:
        return fa, None, None, None
    assert n_pos == DFT_A * DFT_C
    n2 = np.arange(DFT_C)
    tw = 2.0 * np.pi * (n2[:, None] * a[None, :]) / n_pos
    tr = jnp.asarray(np.repeat(np.cos(tw)[:, :, None], LANES, axis=2), F32)
    ti = jnp.asarray(np.repeat(-np.sin(tw)[:, :, None], LANES, axis=2), F32)
    ang16 = 2.0 * np.pi * ((n2[:, None] * n2[None, :]) % DFT_C) / DFT_C
    c16, s16 = np.cos(ang16), np.sin(ang16)
    eye = np.eye(DFT_CK)
    kron = lambda f: np.einsum("kn,ab->kanb", f, eye).reshape(DFT_C * DFT_CK, DFT_C * DFT_CK)
    gmat = np.block([[kron(c16), kron(s16)], [kron(-s16), kron(c16)]])
    return fa, tr, ti, jnp.asarray(gmat, F32)


def _position_dft(u):
    bx, w = u.shape[0], u.shape[-1]
    n_pos = DFT_A if u.ndim == 3 else math.prod(u.shape[1:-1])
    fa, tr, ti, gmat = _dft_tables(n_pos)
    fa_spec = _resident((2 * DFT_A, DFT_A))
    if u.ndim == 3:
        blk = pl.BlockSpec((1, DFT_A, w), lambda b: (0, b, 0))
        out = jax.ShapeDtypeStruct(u.shape, BF16)
        return pl.pallas_call(
            _dft_short_kernel,
            grid=(u.shape[1] // DFT_A,), in_specs=[fa_spec, blk], out_specs=[blk, blk], out_shape=[out, out],
            name="dft_context",
        )(fa, u)
    blk = pl.BlockSpec((1, DFT_C, DFT_A, w), lambda b: (b, 0, 0, 0))
    out = jax.ShapeDtypeStruct((bx, DFT_C, DFT_A, w), BF16)
    ur, ui = pl.pallas_call(
        _dft_long_kernel,
        grid=(bx,),
        in_specs=[fa_spec, _resident(gmat.shape), _resident(tr.shape), _resident(ti.shape), blk],
        out_specs=[blk, blk], out_shape=[out, out],
        scratch_shapes=[pltpu.VMEM((DFT_A // DFT_CK, DFT_C // 2, 2 * DFT_CK, w), BF16)] * 2,
        compiler_params=pltpu.CompilerParams(vmem_limit_bytes=VMEM_LIMIT_BYTES),
        name="dft_long",
    )(fa, gmat, tr, ti, u)
    return ur.reshape(bx, n_pos, w), ui.reshape(bx, n_pos, w)


def _tail_kernel(a_ref, ur_ref, ui_ref, x_ref, gate1_ref, sh_ref, sc_ref, gate2_ref,
                 gpost_mix_ref, gpre_ref, gpost_ref, wa_ref, wr_ref, wi_ref, wg_ref, wu_ref, wd_ref, o_ref,
                 *, ctx_row):
    gate1, shift, scale, gate2 = (_mod_row(r, ctx_row) for r in (gate1_ref, sh_ref, sc_ref, gate2_ref))
    gain_mix = gate1 * gpost_mix_ref[...]
    gain_in = (1.0 + scale) * gpre_ref[...]
    gain_out = gate2 * gpost_ref[...]
    hidden = wg_ref.shape[1]
    tm = x_ref.shape[1]
    per = min(TAIL_SLICE_ROWS, tm)
    assert tm % per == 0

    def mix_dots(rows):
        mix = jnp.dot(a_ref[0, rows, :], wa_ref[...], preferred_element_type=F32)
        mix += jnp.dot(ur_ref[0, rows, :], wr_ref[...], preferred_element_type=F32)
        mix += jnp.dot(ui_ref[0, rows, :], wi_ref[...], preferred_element_type=F32)
        return mix

    def norms(rows, mix):
        x_mid = x_ref[0, rows, :] + _rms(mix, gain_mix)
        return x_mid, (_rms(x_mid, gain_in) + shift).astype(BF16)

    def ffn(h):
        acc = jnp.zeros((h.shape[0], o_ref.shape[2]), F32)
        for lo in range(0, hidden, FFN_CHUNK):
            hi = min(lo + FFN_CHUNK, hidden)
            g = jnp.dot(h, wg_ref[:, lo:hi], preferred_element_type=F32)
            u = jnp.dot(h, wu_ref[:, lo:hi], preferred_element_type=F32)
            a = (g * jax.nn.sigmoid(g) * u).astype(BF16)
            acc += jnp.dot(a, wd_ref[lo:hi, :], preferred_element_type=F32)
        return acc

    def epilogue(rows, x_mid, acc):
        o_ref[0, rows, :] = x_mid + _rms(acc, gain_out)

    slices = [slice(r, r + per) for r in range(0, tm, per)]
    mixes = [mix_dots(rows) for rows in slices]
    staged = norms(slices[0], mixes[0])
    pending = None
    for i, rows in enumerate(slices):
        x_mid, h = staged
        acc = ffn(h)
        if i + 1 < len(slices):
            staged = norms(slices[i + 1], mixes[i + 1])
        if pending is not None:
            epilogue(*pending)
        pending = (rows, x_mid, acc)
    epilogue(*pending)


def _tail(layer, attn, ur, ui, x, mod, g_post_mix, g_pre_ffn, g_post_ffn, w_out, w_four_r, w_four_i,
          wg, wu, wd, ctx_row=None):
    bx, t, d = x.shape
    tm = min(TAIL_TILE, max(t // 2, TAIL_SLICE_ROWS) if bx == 1 else t)
    hidden = wg.shape[-1]
    tok = lambda width: pl.BlockSpec((1, tm, width), lambda b, i: (b, i, 0))
    gain = _layer_resident(layer, (1, d))
    wfour = _layer_resident(layer, (FOURIER_WIDTH, d))
    return pl.pallas_call(
        functools.partial(_tail_kernel, ctx_row=ctx_row),
        grid=(bx, t // tm),
        in_specs=[tok(ATTN_WIDTH), tok(FOURIER_WIDTH), tok(FOURIER_WIDTH), tok(d),
                  _mod_spec(layer, 2, d), _mod_spec(layer, 3, d), _mod_spec(layer, 4, d), _mod_spec(layer, 5, d),
                  gain, gain, gain,
                  _resident((ATTN_WIDTH, d)), wfour, wfour,
                  _resident((d, hidden)), _resident((d, hidden)), _resident((hidden, d))],
        out_specs=tok(d),
        out_shape=jax.ShapeDtypeStruct((bx, t, d), F32),
        compiler_params=pltpu.CompilerParams(vmem_limit_bytes=VMEM_LIMIT_BYTES),
        name="tail",
    )(attn, ur, ui, x, mod, mod, mod, mod, g_post_mix, g_pre_ffn, g_post_ffn,
      w_out, w_four_r, w_four_i, wg, wu, wd)


def _rope_tables(seq):
    t = np.arange(seq)
    freqs = ROPE_THETA ** (-np.arange(ROPE_PAIRS_PER_AXIS, dtype=np.float64) / ROPE_PAIRS_PER_AXIS)
    ang = np.concatenate([(t // GRID_W)[:, None] * freqs, (t % GRID_W)[:, None] * freqs], axis=-1)
    cos = np.repeat(np.cos(ang), 2, axis=-1)
    sin = np.repeat(np.sin(ang), 2, axis=-1) * np.tile([-1.0, 1.0], HEAD_DIM // 2)
    heads_per_row = LANES // HEAD_DIM
    return (jnp.asarray(np.tile(cos, (1, heads_per_row)), F32), jnp.asarray(np.tile(sin, (1, heads_per_row)), F32))


def kernel(x, c, ctx, c_ctx, w_ada, b_ada, norm_pre_mix, norm_post_mix, norm_pre_ffn, norm_post_ffn,
           w_in, w_out, w_fourier, sink, w_gate, w_up, w_down):
    batch, seq, d = x.shape
    depth = w_ada.shape[0]
    assert batch + 1 <= MOD_ROWS

    cc = jnp.zeros((MOD_ROWS, d), F32).at[:batch].set(c).at[batch].set(c_ctx)
    mod = _ada(cc, w_ada, b_ada)
    w_four_r, w_four_i = _fold_fourier_weights(w_fourier, w_out)
    rope = _rope_tables(seq)

    q_scale = np.ones((w_in.shape[-1],), np.float32)
    q_scale[:ATTN_WIDTH] = HEAD_DIM ** -0.5 * LOG2_E
    w_in_b = (w_in * q_scale).astype(BF16)
    gains = [g.reshape(depth, 1, d) for g in (norm_pre_mix, norm_post_mix, norm_pre_ffn, norm_post_ffn)]
    g_pre_mix, g_post_mix, g_pre_ffn, g_post_ffn = gains

    ctx_len = ctx.shape[1]
    assert ctx_len == DFT_A
    xc = ctx.reshape(1, batch * ctx_len, d)
    for i in range(depth):
        q, k, v, u = _project(i, x, mod, g_pre_mix, w_in_b, rope)
        qc, kc, vc, uc = _project(i, xc, mod, g_pre_mix, w_in_b, None, ctx_row=batch)
        casts = [(w_out, i, ATTN_WIDTH), (w_gate, i, d), (w_up, i, d), (w_down, i, w_down.shape[1])]
        attn, tail_weights = _attention(q, k, v, kc, vc, sink[i], ctx_len, local=True, casts=casts)
        w_attn_out, wg, wu, wd = tail_weights

        def finish(stream, attn, ur, ui, ctx_row):
            return _tail(i, attn, ur, ui, stream, mod, g_post_mix, g_pre_ffn, g_post_ffn,
                         w_attn_out, w_four_r, w_four_i, wg, wu, wd, ctx_row)

        ur, ui = _position_dft(u)
        x = finish(x, attn, ur, ui, None)

        if i < depth - 1:
            attn_c, _ = _attention(qc, None, None, kc, vc, sink[i], ctx_len, local=False)
            urc, uic = _position_dft(uc)
            xc = finish(xc, attn_c, urc, uic, batch)
    return x
```
